```python
import math
import jax, jax.numpy as jnp
from jax import lax
import numpy as np

D_MODEL = 4096
BATCH = 4
SEQ = 2048
DEPTH = 4
DEC_BATCH = 128
DEC_SEQ = 4
PAST_LEN = 16384
PAGE_SIZE = 128

N_META = 16
N_MIXERS = 4
GROUP_WIDTH = D_MODEL // N_MIXERS
EPS = 1e-6
SSD_HEAD_DIM = 64
SSD_HEADS = GROUP_WIDTH // SSD_HEAD_DIM
SSD_GROUPS = 2
SSD_HPG = SSD_HEADS // SSD_GROUPS
SSD_STATE = 128
SSD_CONV = 4
SSD_CHUNK = 128
SSD_XBC = GROUP_WIDTH + 2 * SSD_GROUPS * SSD_STATE
RWKV_HEAD_DIM = 64
RWKV_HEADS = GROUP_WIDTH // RWKV_HEAD_DIM
RWKV_DECAY_LORA = max(32, int(round(1.8 * GROUP_WIDTH ** 0.5 / 32)) * 32)
RWKV_AAA_LORA = max(32, int(round(1.8 * GROUP_WIDTH ** 0.5 / 32)) * 32)
RWKV_GATE_LORA = max(32, int(round(0.6 * GROUP_WIDTH ** 0.8 / 32)) * 32)
RWKV_PROJ = 3 * GROUP_WIDTH + RWKV_DECAY_LORA + RWKV_AAA_LORA + RWKV_GATE_LORA
RWKV_GN_EPS = 64e-5
RET_HEADS = 4
RET_V_DIM = GROUP_WIDTH // RET_HEADS
RET_QK_DIM = RET_V_DIM // 2
RET_QK_W = RET_HEADS * RET_QK_DIM
RET_CHUNK = 128
RET_GN_EPS = 1e-5
ROPE_BASE = 10000.0
S5_GROUP = 16
S5_GROUPS = GROUP_WIDTH // S5_GROUP
S5_STATE = 64
D_FF = -(-8 * D_MODEL // (3 * 256)) * 256
IN_SPLITS = (GROUP_WIDTH, SSD_XBC, SSD_HEADS,
             RWKV_PROJ,
             RET_QK_W, RET_QK_W, GROUP_WIDTH, GROUP_WIDTH,
             GROUP_WIDTH)
IN_WIDTH = sum(IN_SPLITS)

kernel_name = 'hymba_ssd_rwkv7_retention_s5_step'


def split_last(x, sizes):
    out, o = [], 0
    for s in sizes:
        out.append(x[..., o:o + s])
        o += s
    return out


def rmsnorm(x, g):
    xf = x.astype(jnp.float32)
    y = xf * lax.rsqrt(jnp.mean(xf * xf, axis=-1, keepdims=True) + EPS)
    return (y * g.astype(jnp.float32)).astype(x.dtype)


def head_norm(x, n_heads, g, b, eps):
    shp = x.shape
    xf = x.astype(jnp.float32).reshape(shp[:-1] + (n_heads, shp[-1] // n_heads))
    xc = xf - jnp.mean(xf, axis=-1, keepdims=True)
    y = (xc * lax.rsqrt(jnp.mean(xc * xc, axis=-1, keepdims=True) + eps)).reshape(shp)
    return (y * g.astype(jnp.float32) + b.astype(jnp.float32)).astype(x.dtype)


def causal_dwconv(x, buf, w, b):
    xp = jnp.concatenate([buf.astype(x.dtype), x], axis=1)
    y = lax.conv_general_dilated(xp, w[:, None, :].astype(x.dtype), (1,), 'VALID',
                                 dimension_numbers=('NWC', 'WIO', 'NWC'),
                                 feature_group_count=x.shape[-1])
    return y + b, xp[:, -(w.shape[0] - 1):]


def ssd_chunked(x, bm, cm, dt, log_a, h0):
    bsz, n = x.shape[:2]
    c = min(SSD_CHUNK, n)
    nc = n // c
    dtype = x.dtype
    rs = lambda a: a.reshape((bsz, nc, c) + a.shape[2:])
    x, bm, cm, dt, log_a = rs(x), rs(bm), rs(cm), rs(dt), rs(log_a)
    cum = jnp.cumsum(log_a, axis=2)
    causal = jnp.tril(jnp.ones((c, c), bool))[:, :, None, None]
    seg = cum[:, :, :, None] - cum[:, :, None, :]
    lmat = jnp.exp(jnp.where(causal, seg, -jnp.inf)).astype(dtype)
    cb = jnp.einsum('bcqgn,bcsgn->bcqsg', cm, bm)
    y_diag = jnp.einsum('bcqsgh,bcsghp->bcqghp', cb[..., None] * lmat, x * dt[..., None])
    decay_to_end = jnp.exp(cum[:, :, -1:] - cum).astype(dtype)
    chunk_states = jnp.einsum('bcsgn,bcsghp->bcghpn', bm, x * (decay_to_end * dt)[..., None])
    chunk_decay = jnp.exp(cum[:, :, -1]).astype(dtype)

    def step(h, inp):
        cs, cd = inp
        return (h * cd[..., None, None] + cs).astype(h.dtype), h

    h_last, h_in = lax.scan(step, h0, (jnp.moveaxis(chunk_states, 1, 0),
                                       jnp.moveaxis(chunk_decay, 1, 0)))
    y_off = jnp.einsum('bcqgn,cbghpn->bcqghp', cm, h_in) * jnp.exp(cum).astype(dtype)[..., None]
    return (y_diag + y_off).reshape((bsz, n) + x.shape[3:]), h_last


def ssd_mixer(z, xbc, dt_raw, conv_buf, h0, conv_w, conv_b, dt_bias, a_log, d_skip, norm_w, segments):
    bsz, n, _ = z.shape
    xbc, conv_new = causal_dwconv(xbc, conv_buf, conv_w, conv_b)
    xbc = jax.nn.silu(xbc)
    xs, bm, cm = split_last(xbc, (GROUP_WIDTH, SSD_GROUPS * SSD_STATE, SSD_GROUPS * SSD_STATE))
    xs = xs.reshape(bsz, n, SSD_GROUPS, SSD_HPG, SSD_HEAD_DIM)
    bm = bm.reshape(bsz, n, SSD_GROUPS, SSD_STATE)
    cm = cm.reshape(bsz, n, SSD_GROUPS, SSD_STATE)
    dt = jax.nn.softplus((dt_raw + dt_bias).astype(jnp.float32)).reshape(bsz, n, SSD_GROUPS, SSD_HPG)
    log_a = dt * (-jnp.exp(a_log.astype(jnp.float32))).reshape(SSD_GROUPS, SSD_HPG)
    h = h0.reshape(bsz, SSD_GROUPS, SSD_HPG, SSD_HEAD_DIM, SSD_STATE)
    ys, t0 = [], 0
    for seg_len in segments:
        sl = slice(t0, t0 + seg_len)
        y, h = ssd_chunked(xs[:, sl], bm[:, sl], cm[:, sl], dt[:, sl].astype(xs.dtype), log_a[:, sl], h)
        ys.append(y)
        t0 += seg_len
    y = jnp.concatenate(ys, axis=1) + xs * d_skip.reshape(SSD_GROUPS, SSD_HPG, 1)
    y = y.reshape(bsz, n, GROUP_WIDTH) * jax.nn.silu(z)
    return rmsnorm(y, norm_w), h.reshape(bsz, SSD_HEADS, SSD_HEAD_DIM, SSD_STATE), conv_new


def rwkv7_mixer(pr, shift_buf, s0, mu, w0, w2, a0, a2, g2, k_k, k_a, r_k, ln_w, ln_b):
    bsz, n, _ = pr.shape
    dtype = pr.dtype
    prev = jnp.concatenate([shift_buf[:, None].astype(dtype), pr[:, :-1]], axis=1)
    pm = pr + (prev - pr) * mu
    r, k, v, wl, al, gl = split_last(pm, (GROUP_WIDTH, GROUP_WIDTH, GROUP_WIDTH,
                                          RWKV_DECAY_LORA, RWKV_AAA_LORA, RWKV_GATE_LORA))
    w_log = -jax.nn.softplus(-(w0 + jnp.tanh(wl) @ w2).astype(jnp.float32)) - 0.5
    decay = jnp.exp(-jnp.exp(w_log)).astype(dtype)
    a = jax.nn.sigmoid(a0 + al @ a2)
    g = jax.nn.sigmoid(gl) @ g2
    hd = lambda t: t.reshape(bsz, n, RWKV_HEADS, RWKV_HEAD_DIM)
    kkf = hd(k * k_k).astype(jnp.float32)
    kk = (kkf / jnp.maximum(jnp.sqrt(jnp.sum(kkf * kkf, -1, keepdims=True)), 1e-12)).astype(dtype)
    k = hd(k * (1.0 + (a - 1.0) * k_a))
    r, v, decay, a = hd(r), hd(v), hd(decay), hd(a)

    def step(S, inp):
        r_t, k_t, v_t, w_t, kk_t, a_t = inp
        sa = jnp.einsum('bhvk,bhk->bhv', S, kk_t)
        S = (S * w_t[:, :, None, :] - sa[..., None] * (kk_t * a_t)[:, :, None, :]
             + v_t[..., None] * k_t[:, :, None, :]).astype(S.dtype)
        return S, jnp.einsum('bhvk,bhk->bhv', S, r_t)

    tm = lambda t: jnp.moveaxis(t, 1, 0)
    s_last, o = lax.scan(step, s0, (tm(r), tm(k), tm(v), tm(decay), tm(kk), tm(a)))
    o = jnp.moveaxis(o, 0, 1).reshape(bsz, n, GROUP_WIDTH)
    o = head_norm(o, RWKV_HEADS, ln_w, ln_b, RWKV_GN_EPS)
    bonus = (jnp.sum(r * k * r_k, axis=-1, keepdims=True) * v).reshape(bsz, n, GROUP_WIDTH)
    return (o + bonus) * g, s_last, pr[:, -1]


def ret_log_gamma():
    return jnp.log(1.0 - 2.0 ** (-5.0 - jnp.arange(RET_HEADS, dtype=jnp.float32)))


def rotate(x, cos, sin):
    x1, x2 = x[..., :x.shape[-1] // 2], x[..., x.shape[-1] // 2:]
    return jnp.concatenate([x1 * cos - x2 * sin, x1 * sin + x2 * cos], axis=-1).astype(x.dtype)


def retention_chunked(q, k, v, r0):
    bsz, n = q.shape[:2]
    c = min(RET_CHUNK, n)
    nc = n // c
    dtype = q.dtype
    rs = lambda a: a.reshape((bsz, nc, c) + a.shape[2:])
    q, k, v = rs(q), rs(k), rs(v)
    lg = ret_log_gamma()
    idx = jnp.arange(c, dtype=jnp.float32)
    diff = idx[:, None] - idx[None, :]
    dmat = jnp.where((diff >= 0)[..., None], jnp.exp(jnp.maximum(diff, 0.0)[..., None] * lg), 0.0).astype(dtype)
    inner = jnp.einsum('bcqhd,bcshd->bcqsh', q, k) * dmat
    y_in = jnp.einsum('bcqsh,bcshe->bcqhe', inner, v)
    to_end = jnp.exp((c - 1.0 - idx)[:, None] * lg).astype(dtype)
    chunk_states = jnp.einsum('bcshd,bcshe,sh->bchde', k, v, to_end)
    chunk_decay = jnp.exp(c * lg).astype(dtype)

    def step(rst, cs):
        return (rst * chunk_decay[:, None, None] + cs).astype(rst.dtype), rst

    r_last, r_in = lax.scan(step, r0, jnp.moveaxis(chunk_states, 1, 0))
    from_start = jnp.exp((idx + 1.0)[:, None] * lg).astype(dtype)
    y_x = jnp.einsum('bcqhd,cbhde,qh->bcqhe', q, r_in, from_start)
    return (y_in + y_x).reshape((bsz, n) + v.shape[3:]), r_last


def retention_mixer(q, k, v, gate, pos, r0, gn_w, gn_b, segments):
    bsz, n, _ = q.shape
    q = q.reshape(bsz, n, RET_HEADS, RET_QK_DIM)
    k = k.reshape(bsz, n, RET_HEADS, RET_QK_DIM)
    v = v.reshape(bsz, n, RET_HEADS, RET_V_DIM)
    theta = 1.0 / (ROPE_BASE ** jnp.linspace(0.0, 1.0, RET_QK_DIM // 2, dtype=jnp.float32))
    ang = pos.astype(jnp.float32)[:, None] * theta
    cos, sin = jnp.cos(ang)[:, None, :].astype(q.dtype), jnp.sin(ang)[:, None, :].astype(q.dtype)
    q = rotate(q, cos, sin)
    k = rotate(k, cos, sin) * (RET_QK_DIM ** -0.5)
    ys, t0, rst = [], 0, r0
    for seg_len in segments:
        sl = slice(t0, t0 + seg_len)
        y, rst = retention_chunked(q[:, sl], k[:, sl], v[:, sl], rst)
        ys.append(y)
        t0 += seg_len
    o = jnp.concatenate(ys, axis=1).reshape(bsz, n, GROUP_WIDTH)
    return jax.nn.silu(gate) * head_norm(o, RET_HEADS, gn_w, gn_b, RET_GN_EPS), rst


def _complex_affine_combine(e1, e2):
    a1r, a1i, b1r, b1i = e1
    a2r, a2i, b2r, b2i = e2
    return (a1r * a2r - a1i * a2i, a1r * a2i + a1i * a2r,
            a2r * b1r - a2i * b1i + b2r, a2r * b1i + a2i * b1r + b2i)


def s5_mixer(u, s0_re, s0_im, a_re, a_im, log_dt, b_re, b_im, c_re, c_im, d, glu_w, glu_b, norm_w):
    bsz, n, _ = u.shape
    dtype = u.dtype
    f32 = jnp.float32
    dt = jnp.exp(log_dt.astype(f32))[:, None]
    lr, li = a_re.astype(f32), a_im.astype(f32)
    mag = jnp.exp(lr * dt)
    abar_re, abar_im = mag * jnp.cos(li * dt), mag * jnp.sin(li * dt)
    den = lr * lr + li * li
    nr, ni = abar_re - 1.0, abar_im
    e_re, e_im = (nr * lr + ni * li) / den, (ni * lr - nr * li) / den
    br, bi = b_re.astype(f32), b_im.astype(f32)
    bb_re = (e_re[..., None] * br - e_im[..., None] * bi).astype(dtype)
    bb_im = (e_re[..., None] * bi + e_im[..., None] * br).astype(dtype)
    ug = u.reshape(bsz, n, S5_GROUPS, S5_GROUP)
    bu_re = jnp.einsum('gnc,blgc->blgn', bb_re, ug)
    bu_im = jnp.einsum('gnc,blgc->blgn', bb_im, ug)
    ar_t = jnp.broadcast_to(abar_re.astype(dtype), bu_re.shape)
    ai_t = jnp.broadcast_to(abar_im.astype(dtype), bu_re.shape)
    cr, ci, xr, xi = lax.associative_scan(_complex_affine_combine, (ar_t, ai_t, bu_re, bu_im), axis=1)
    s0r, s0i = s0_re[:, None].astype(dtype), s0_im[:, None].astype(dtype)
    xr, xi = xr + cr * s0r - ci * s0i, xi + cr * s0i + ci * s0r
    y = jnp.einsum('gcn,blgn->blgc', c_re, xr) - jnp.einsum('gcn,blgn->blgc', c_im, xi)
    y = y.reshape(bsz, n, GROUP_WIDTH) + d * u
    gy = jax.nn.gelu(y)
    out = gy * jax.nn.sigmoid(gy @ glu_w + glu_b)
    return rmsnorm(out, norm_w), xr[:, -1], xi[:, -1]


def trunk(h, states, pos, segments, prm, ln_f):
    outs = tuple([] for _ in states)
    for l in range(DEPTH):
        P = {name: w[l] for name, w in prm.items()}
        st = [s[l] for s in states]
        hn = rmsnorm(h, P['ln_mix'])
        z, xbc, dtr, pr, rq, rk, rv, rg, u = split_last(hn @ P['w_in'], IN_SPLITS)
        y_a, ssd_n, conv_n = ssd_mixer(z, xbc, dtr, st[1], st[0], P['ssd_conv_w'], P['ssd_conv_b'],
                                       P['ssd_dt_bias'], P['ssd_a_log'], P['ssd_d'], P['ssd_norm'], segments)
        y_b, rwkv_n, shift_n = rwkv7_mixer(pr, st[3], st[2], P['rwkv_mu'], P['rwkv_w0'], P['rwkv_w2'],
                                           P['rwkv_a0'], P['rwkv_a2'], P['rwkv_g2'], P['rwkv_k_k'],
                                           P['rwkv_k_a'], P['rwkv_r_k'], P['rwkv_ln_w'], P['rwkv_ln_b'])
        y_c, ret_n = retention_mixer(rq, rk, rv, rg, pos, st[4], P['ret_gn_w'], P['ret_gn_b'], segments)
        y_d, s5r_n, s5i_n = s5_mixer(u, st[5], st[6], P['s5_a_re'], P['s5_a_im'], P['s5_log_dt'],
                                     P['s5_b_re'], P['s5_b_im'], P['s5_c_re'], P['s5_c_im'], P['s5_d'],
                                     P['s5_glu_w'], P['s5_glu_b'], P['s5_norm'])
        h = h + jnp.concatenate([y_a, y_b, y_c, y_d], axis=-1) @ P['w_out']
        hn = rmsnorm(h, P['ln_ffn'])
        h = h + (jax.nn.silu(hn @ P['w_gate']) * (hn @ P['w_up'])) @ P['w_down']
        for acc, s in zip(outs, (ssd_n, conv_n, rwkv_n, shift_n, ret_n, s5r_n, s5i_n)):
            acc.append(s)
    return rmsnorm(h, ln_f), [jnp.stack(acc) for acc in outs]


def setup_inputs(seed: int = 0) -> dict:
    key = jax.random.key(seed)
    ks = iter(jax.random.split(key, 64))
    f32 = jnp.float32

    def nrm(shape, scale=1.0):
        return jax.random.normal(next(ks), shape, f32) * scale

    def uni(shape, lo, hi):
        return jax.random.uniform(next(ks), shape, f32, lo, hi)

    def gain(shape):
        return 1.0 + nrm(shape, 0.02)

    L, D, GW = DEPTH, D_MODEL, GROUP_WIDTH
    ssd_dt = jnp.exp(uni((L, SSD_HEADS), math.log(1e-3), math.log(1e-1)))
    n_idx = jnp.arange(S5_STATE, dtype=f32)
    return {
        'x_prompt': nrm((BATCH, SEQ, D)),
        'x_sample': nrm((DEC_BATCH, DEC_SEQ, D)),
        'state_ssd': nrm((L, DEC_BATCH, SSD_HEADS, SSD_HEAD_DIM, SSD_STATE), 0.5),
        'state_ssd_conv': nrm((L, DEC_BATCH, SSD_CONV - 1, SSD_XBC)),
        'state_rwkv': nrm((L, DEC_BATCH, RWKV_HEADS, RWKV_HEAD_DIM, RWKV_HEAD_DIM), 0.3),
        'state_rwkv_shift': nrm((L, DEC_BATCH, RWKV_PROJ)),
        'state_ret': nrm((L, DEC_BATCH, RET_HEADS, RET_QK_DIM, RET_V_DIM), 0.5),
        'state_s5_re': nrm((L, DEC_BATCH, S5_GROUPS, S5_STATE), 0.1),
        'state_s5_im': nrm((L, DEC_BATCH, S5_GROUPS, S5_STATE), 0.1),
        'meta': nrm((N_META, D)),
        'ln_mix': gain((L, D)),
        'w_in': nrm((L, D, IN_WIDTH), D ** -0.5),
        'ssd_conv_w': nrm((L, SSD_CONV, SSD_XBC), SSD_CONV ** -0.5),
        'ssd_conv_b': nrm((L, SSD_XBC), 0.01),
        'ssd_dt_bias': ssd_dt + jnp.log(-jnp.expm1(-ssd_dt)),
        'ssd_a_log': jnp.log(uni((L, SSD_HEADS), 1.0, 16.0)),
        'ssd_d': gain((L, SSD_HEADS)),
        'ssd_norm': gain((L, GW)),
        'rwkv_mu': uni((L, RWKV_PROJ), 0.0, 1.0),
        'rwkv_w0': uni((L, GW), -6.0, -1.0),
        'rwkv_w2': nrm((L, RWKV_DECAY_LORA, GW), 0.1),
        'rwkv_a0': nrm((L, GW), 0.1),
        'rwkv_a2': nrm((L, RWKV_AAA_LORA, GW), 0.1),
        'rwkv_g2': nrm((L, RWKV_GATE_LORA, GW), RWKV_GATE_LORA ** -0.5),
        'rwkv_k_k': 0.85 + nrm((L, GW), 0.05),
        'rwkv_k_a': 1.0 + nrm((L, GW), 0.05),
        'rwkv_r_k': nrm((L, RWKV_HEADS, RWKV_HEAD_DIM), 0.1),
        'rwkv_ln_w': gain((L, GW)),
        'rwkv_ln_b': nrm((L, GW), 0.01),
        'ret_gn_w': gain((L, GW)),
        'ret_gn_b': nrm((L, GW), 0.01),
        's5_a_re': -0.5 * jnp.exp(nrm((L, S5_GROUPS, S5_STATE), 0.01)),
        's5_a_im': math.pi * n_idx + nrm((L, S5_GROUPS, S5_STATE), 0.01),
        's5_log_dt': uni((L, S5_GROUPS), math.log(1e-3), math.log(1e-1)),
        's5_b_re': nrm((L, S5_GROUPS, S5_STATE, S5_GROUP), (2 * S5_GROUP) ** -0.5),
        's5_b_im': nrm((L, S5_GROUPS, S5_STATE, S5_GROUP), (2 * S5_GROUP) ** -0.5),
        's5_c_re': nrm((L, S5_GROUPS, S5_GROUP, S5_STATE), (2 * S5_STATE) ** -0.5),
        's5_c_im': nrm((L, S5_GROUPS, S5_GROUP, S5_STATE), (2 * S5_STATE) ** -0.5),
        's5_d': nrm((L, GW)),
        's5_glu_w': nrm((L, GW, GW), GW ** -0.5),
        's5_glu_b': nrm((L, GW), 0.01),
        's5_norm': gain((L, GW)),
        'w_out': nrm((L, D, D), D ** -0.5),
        'ln_ffn': gain((L, D)),
        'w_gate': nrm((L, D, D_FF), D ** -0.5),
        'w_up': nrm((L, D, D_FF), D ** -0.5),
        'w_down': nrm((L, D_FF, D), D_FF ** -0.5),
        'ln_f': gain((D,)),
    }


def reference(x_prompt, x_sample, state_ssd, state_ssd_conv, state_rwkv, state_rwkv_shift, state_ret,
              state_s5_re, state_s5_im, meta, ln_mix, w_in, ssd_conv_w, ssd_conv_b, ssd_dt_bias, ssd_a_log,
              ssd_d, ssd_norm, rwkv_mu, rwkv_w0, rwkv_w2, rwkv_a0, rwkv_a2, rwkv_g2, rwkv_k_k, rwkv_k_a,
              rwkv_r_k, rwkv_ln_w, rwkv_ln_b, ret_gn_w, ret_gn_b, s5_a_re, s5_a_im, s5_log_dt, s5_b_re,
              s5_b_im, s5_c_re, s5_c_im, s5_d, s5_glu_w, s5_glu_b, s5_norm, w_out, ln_ffn, w_gate, w_up,
              w_down, ln_f):
    prm = dict(ln_mix=ln_mix, w_in=w_in, ssd_conv_w=ssd_conv_w, ssd_conv_b=ssd_conv_b,
               ssd_dt_bias=ssd_dt_bias, ssd_a_log=ssd_a_log, ssd_d=ssd_d, ssd_norm=ssd_norm,
               rwkv_mu=rwkv_mu, rwkv_w0=rwkv_w0, rwkv_w2=rwkv_w2, rwkv_a0=rwkv_a0, rwkv_a2=rwkv_a2,
               rwkv_g2=rwkv_g2, rwkv_k_k=rwkv_k_k, rwkv_k_a=rwkv_k_a, rwkv_r_k=rwkv_r_k,
               rwkv_ln_w=rwkv_ln_w, rwkv_ln_b=rwkv_ln_b, ret_gn_w=ret_gn_w, ret_gn_b=ret_gn_b,
               s5_a_re=s5_a_re, s5_a_im=s5_a_im, s5_log_dt=s5_log_dt, s5_b_re=s5_b_re, s5_b_im=s5_b_im,
               s5_c_re=s5_c_re, s5_c_im=s5_c_im, s5_d=s5_d, s5_glu_w=s5_glu_w, s5_glu_b=s5_glu_b,
               s5_norm=s5_norm, w_out=w_out, ln_ffn=ln_ffn, w_gate=w_gate, w_up=w_up, w_down=w_down)
    sample_states = (state_ssd, state_ssd_conv, state_rwkv, state_rwkv_shift, state_ret,
                     state_s5_re, state_s5_im)

    bp, sp = x_prompt.shape[0], x_prompt.shape[1]
    meta_b = jnp.broadcast_to(meta.astype(x_prompt.dtype)[None], (bp, N_META, D_MODEL))
    hp = jnp.concatenate([meta_b, x_prompt], axis=1)
    zero_states = tuple(jnp.zeros((s.shape[0], bp) + s.shape[2:], x_prompt.dtype) for s in sample_states)
    pos_p = jnp.arange(N_META + sp)
    yp, ps = trunk(hp, zero_states, pos_p, (N_META, sp), prm, ln_f)
    y_prompt = yp[:, N_META:]
    p_ssd, p_ssd_conv, p_rwkv, p_rwkv_shift, p_ret, p_s5_re, p_s5_im = ps

    pos_s = PAST_LEN + jnp.arange(x_sample.shape[1])
    y_sample, ss = trunk(x_sample, sample_states, pos_s, (x_sample.shape[1],), prm, ln_f)
    s_ssd, s_ssd_conv, s_rwkv, s_rwkv_shift, s_ret, s_s5_re, s_s5_im = ss

    return (y_prompt, y_sample, p_ssd, p_ssd_conv, p_rwkv, p_rwkv_shift, p_ret, p_s5_re, p_s5_im,
            s_ssd, s_ssd_conv, s_rwkv, s_rwkv_shift, s_ret, s_s5_re, s_s5_im)
```

```python
import functools
import math

import jax
import jax.numpy as jnp
from jax import lax
from jax.experimental import pallas as pl
from jax.experimental.pallas import tpu as pltpu

N_META = 16
EPS = 1e-6
SSD_GROUPS = 2
SSD_CHUNK = 128
RWKV_HEAD_DIM = 64
RWKV_GN_EPS = 64e-5
RET_CHUNK = 128
RET_GN_EPS = 1e-5
ROPE_BASE = 10000.0
S5_GROUP = 16
PAST_LEN = 16384
CHUNK = 128
LANE = 128
V7X_VMEM_CAP = 60 * 1024 * 1024

_F32 = jnp.float32
_BF16 = jnp.bfloat16


def _pick(n, cands):
    for c in cands:
        if n % c == 0:
            return c
    raise ValueError(f"no tile in {cands} divides {n}")


def _round_up(n, m):
    return -(-n // m) * m


def _cparams(sem, *block_bytes):
    need = 2 * sum(block_bytes) + (6 << 20)
    return pltpu.CompilerParams(dimension_semantics=sem, vmem_limit_bytes=int(min(max(need, 16 << 20), V7X_VMEM_CAP)))


def _rmsnorm_kernel(x_ref, g_ref, o_ref, *, tr, lp, pad, mp):
    x = x_ref[...]
    y = x * lax.rsqrt(jnp.mean(x * x, axis=-1, keepdims=True) + EPS) * g_ref[...]
    if pad:
        row0 = pl.program_id(0) * tr
        pos0 = lax.rem(row0, lp)
        rows = lax.broadcasted_iota(jnp.int32, (tr, 1), 0)
        is_pad = jnp.logical_and(row0 < mp, pos0 + rows < pad)
        y = jnp.where(is_pad, 0.0, y)
    o_ref[...] = y.astype(o_ref.dtype)


def _rmsnorm(x, g, out_dtype, lp=0, pad=0, mp=0):
    m, d = x.shape
    tr = _pick(math.gcd(m, lp) if pad else m, (256, 128, 64, 32, 16, 8))
    kern = functools.partial(_rmsnorm_kernel, tr=tr, lp=lp, pad=pad, mp=mp)
    return pl.pallas_call(
        kern, grid=(m // tr,),
        in_specs=[pl.BlockSpec((tr, d), lambda i: (i, 0)), pl.BlockSpec((1, d), lambda i: (0, 0))],
        out_specs=pl.BlockSpec((tr, d), lambda i: (i, 0)),
        out_shape=jax.ShapeDtypeStruct((m, d), out_dtype),
        compiler_params=_cparams(("parallel",), tr * d * 4, tr * d * 4),
        name="rmsnorm",
    )(x, g.reshape(1, d).astype(_F32))


def _mm_kernel(*refs, has_res):
    if has_res:
        x_ref, w_ref, r_ref, o_ref = refs
    else:
        x_ref, w_ref, o_ref = refs
    acc = jnp.dot(x_ref[...], w_ref[...], preferred_element_type=_F32)
    if has_res:
        acc = acc + r_ref[...]
    o_ref[...] = acc.astype(o_ref.dtype)


def _mm(x, w, res=None, out_dtype=_F32, kb=0, tk=None, name="matmul"):
    m = x.shape[0]
    n = w.shape[1]
    tk = tk or w.shape[0]
    tm = _pick(m, (1024, 512, 256, 128))
    if tk > 4096:
        tm = _pick(m, (512, 256, 128))
    tn = _pick(n, (512, 256, 128))
    in_specs = [pl.BlockSpec((tm, tk), lambda i, j: (i, kb)), pl.BlockSpec((tk, tn), lambda i, j: (kb, j))]
    args = [x, w]
    blocks = [tm * tk * 2, tk * tn * 2, tm * tn * 4]
    if res is not None:
        in_specs.append(pl.BlockSpec((tm, tn), lambda i, j: (i, j)))
        args.append(res)
        blocks.append(tm * tn * 4)
    return pl.pallas_call(
        functools.partial(_mm_kernel, has_res=res is not None), grid=(m // tm, n // tn),
        in_specs=in_specs, out_specs=pl.BlockSpec((tm, tn), lambda i, j: (i, j)),
        out_shape=jax.ShapeDtypeStruct((m, n), out_dtype),
        compiler_params=_cparams(("parallel", "arbitrary"), *blocks),
        name=name,
    )(*args)


def _swiglu_kernel(x_ref, wg_ref, wu_ref, o_ref):
    x = x_ref[...]
    g = jnp.dot(x, wg_ref[...], preferred_element_type=_F32)
    u = jnp.dot(x, wu_ref[...], preferred_element_type=_F32)
    o_ref[...] = (g * jax.nn.sigmoid(g) * u).astype(o_ref.dtype)


def _swiglu(x, wg, wu):
    m, k = x.shape
    n = wg.shape[1]
    tm = _pick(m, (1024, 512, 256, 128))
    tn = _pick(n, (256, 128))
    return pl.pallas_call(
        _swiglu_kernel, grid=(m // tm, n // tn),
        in_specs=[pl.BlockSpec((tm, k), lambda i, j: (i, 0)), pl.BlockSpec((k, tn), lambda i, j: (0, j)),
                  pl.BlockSpec((k, tn), lambda i, j: (0, j))],
        out_specs=pl.BlockSpec((tm, tn), lambda i, j: (i, j)),
        out_shape=jax.ShapeDtypeStruct((m, n), _BF16),
        compiler_params=_cparams(("parallel", "arbitrary"), tm * k * 2, 2 * k * tn * 2, tm * tn * 2),
        name="swiglu",
    )(x, wg, wu)


def _split_last(x, sizes):
    out, o = [], 0
    for s in sizes:
        out.append(x[..., o:o + s])
        o += s
    return out


def _rms(x, g):
    return x * lax.rsqrt(jnp.mean(x * x, axis=-1, keepdims=True) + EPS) * g


def _head_norm(x, n_heads, g, b, eps):
    shp = x.shape
    xf = x.reshape(shp[:-1] + (n_heads, shp[-1] // n_heads))
    xc = xf - jnp.mean(xf, axis=-1, keepdims=True)
    y = (xc * lax.rsqrt(jnp.mean(xc * xc, axis=-1, keepdims=True) + eps)).reshape(shp)
    return y * g + b


def _jx_dwconv(x, buf, w, b):
    xp = jnp.concatenate([buf, x], axis=1)
    k = w.shape[0]
    n = x.shape[1]
    y = sum(xp[:, j:j + n] * w[j] for j in range(k))
    return y + b, xp[:, -(k - 1):]


def _jx_ssd_chunked(x, bm, cm, dt, log_a, h0):
    bsz, n = x.shape[:2]
    c = min(SSD_CHUNK, n)
    nc = n // c
    rs = lambda a: a.reshape((bsz, nc, c) + a.shape[2:])
    x, bm, cm, dt, log_a = rs(x), rs(bm), rs(cm), rs(dt), rs(log_a)
    cum = jnp.cumsum(log_a, axis=2)
    causal = jnp.tril(jnp.ones((c, c), bool))[:, :, None, None]
    seg = cum[:, :, :, None] - cum[:, :, None, :]
    lmat = jnp.exp(jnp.where(causal, seg, -jnp.inf))
    cb = jnp.einsum('bcqgn,bcsgn->bcqsg', cm, bm)
    y_diag = jnp.einsum('bcqsgh,bcsghp->bcqghp', cb[..., None] * lmat, x * dt[..., None])
    decay_to_end = jnp.exp(cum[:, :, -1:] - cum)
    chunk_states = jnp.einsum('bcsgn,bcsghp->bcghpn', bm, x * (decay_to_end * dt)[..., None])
    chunk_decay = jnp.exp(cum[:, :, -1])

    def step(h, inp):
        cs, cd = inp
        return h * cd[..., None, None] + cs, h

    h_last, h_in = lax.scan(step, h0, (jnp.moveaxis(chunk_states, 1, 0), jnp.moveaxis(chunk_decay, 1, 0)))
    y_off = jnp.einsum('bcqgn,cbghpn->bcqghp', cm, h_in) * jnp.exp(cum)[..., None]
    return (y_diag + y_off).reshape((bsz, n) + x.shape[3:]), h_last


def _jx_ssd(z, xbc, dt_raw, conv_buf, h0, conv_w, conv_b, dt_bias, a_log, d_skip, norm_w, segments):
    bsz, n, gw = z.shape
    heads, p, ns = h0.shape[1:]
    hpg = heads // SSD_GROUPS
    xbc, conv_new = _jx_dwconv(xbc, conv_buf, conv_w, conv_b)
    xbc = jax.nn.silu(xbc)
    xs, bm, cm = _split_last(xbc, (gw, SSD_GROUPS * ns, SSD_GROUPS * ns))
    xs = xs.reshape(bsz, n, SSD_GROUPS, hpg, p)
    bm = bm.reshape(bsz, n, SSD_GROUPS, ns)
    cm = cm.reshape(bsz, n, SSD_GROUPS, ns)
    dt = jax.nn.softplus(dt_raw + dt_bias).reshape(bsz, n, SSD_GROUPS, hpg)
    log_a = dt * (-jnp.exp(a_log)).reshape(SSD_GROUPS, hpg)
    h = h0.reshape(bsz, SSD_GROUPS, hpg, p, ns)
    ys, t0 = [], 0
    for seg_len in segments:
        sl = slice(t0, t0 + seg_len)
        y, h = _jx_ssd_chunked(xs[:, sl], bm[:, sl], cm[:, sl], dt[:, sl], log_a[:, sl], h)
        ys.append(y)
        t0 += seg_len
    y = jnp.concatenate(ys, axis=1) + xs * d_skip.reshape(SSD_GROUPS, hpg, 1)
    y = y.reshape(bsz, n, gw) * jax.nn.silu(z)
    return _rms(y, norm_w), h.reshape(bsz, heads, p, ns), conv_new


def _jx_rwkv(pr, shift_buf, s0, mu, w0, w2, a0, a2, g2, k_k, k_a, r_k, ln_w, ln_b):
    bsz, n, _ = pr.shape
    heads, hd_ = s0.shape[1], s0.shape[2]
    gw = heads * hd_
    prev = jnp.concatenate([shift_buf[:, None], pr[:, :-1]], axis=1)
    pm = pr + (prev - pr) * mu
    r, k, v, wl, al, gl = _split_last(pm, (gw, gw, gw, w2.shape[0], a2.shape[0], g2.shape[0]))
    w_log = -jax.nn.softplus(-(w0 + jnp.tanh(wl) @ w2)) - 0.5
    decay = jnp.exp(-jnp.exp(w_log))
    a = jax.nn.sigmoid(a0 + al @ a2)
    g = jax.nn.sigmoid(gl) @ g2
    hd = lambda t: t.reshape(bsz, n, heads, hd_)
    kkf = hd(k * k_k)
    kk = kkf / jnp.maximum(jnp.sqrt(jnp.sum(kkf * kkf, -1, keepdims=True)), 1e-12)
    k = hd(k * (1.0 + (a - 1.0) * k_a))
    r, v, decay, a = hd(r), hd(v), hd(decay), hd(a)

    def step(S, inp):
        r_t, k_t, v_t, w_t, kk_t, a_t = inp
        sa = jnp.einsum('bhvk,bhk->bhv', S, kk_t)
        S = S * w_t[:, :, None, :] - sa[..., None] * (kk_t * a_t)[:, :, None, :] + v_t[..., None] * k_t[:, :, None, :]
        return S, jnp.einsum('bhvk,bhk->bhv', S, r_t)

    tm = lambda t: jnp.moveaxis(t, 1, 0)
    s_last, o = lax.scan(step, s0, (tm(r), tm(k), tm(v), tm(decay), tm(kk), tm(a)))
    o = jnp.moveaxis(o, 0, 1).reshape(bsz, n, gw)
    o = _head_norm(o, heads, ln_w, ln_b, RWKV_GN_EPS)
    bonus = (jnp.sum(r * k * r_k, axis=-1, keepdims=True) * v).reshape(bsz, n, gw)
    return (o + bonus) * g, s_last, pr[:, -1]


def _rotate(x, cos, sin):
    x1, x2 = x[..., :x.shape[-1] // 2], x[..., x.shape[-1] // 2:]
    return jnp.concatenate([x1 * cos - x2 * sin, x1 * sin + x2 * cos], axis=-1)


def _jx_ret_chunked(q, k, v, r0):
    bsz, n = q.shape[:2]
    heads = q.shape[2]
    c = min(RET_CHUNK, n)
    nc = n // c
    rs = lambda a: a.reshape((bsz, nc, c) + a.shape[2:])
    q, k, v = rs(q), rs(k), rs(v)
    lg = jnp.log(1.0 - 2.0 ** (-5.0 - jnp.arange(heads, dtype=_F32)))
    idx = jnp.arange(c, dtype=_F32)
    diff = idx[:, None] - idx[None, :]
    dmat = jnp.where((diff >= 0)[..., None], jnp.exp(jnp.maximum(diff, 0.0)[..., None] * lg), 0.0)
    inner = jnp.einsum('bcqhd,bcshd->bcqsh', q, k) * dmat
    y_in = jnp.einsum('bcqsh,bcshe->bcqhe', inner, v)
    to_end = jnp.exp((c - 1.0 - idx)[:, None] * lg)
    chunk_states = jnp.einsum('bcshd,bcshe,sh->bchde', k, v, to_end)
    chunk_decay = jnp.exp(c * lg)

    def step(rst, cs):
        return rst * chunk_decay[:, None, None] + cs, rst

    r_last, r_in = lax.scan(step, r0, jnp.moveaxis(chunk_states, 1, 0))
    from_start = jnp.exp((idx + 1.0)[:, None] * lg)
    y_x = jnp.einsum('bcqhd,cbhde,qh->bcqhe', q, r_in, from_start)
    return (y_in + y_x).reshape((bsz, n) + v.shape[3:]), r_last


def _jx_ret(q, k, v, gate, pos, r0, gn_w, gn_b, segments):
    bsz, n, gw = v.shape
    heads, dk, dv = r0.shape[1:]
    q = q.reshape(bsz, n, heads, dk)
    k = k.reshape(bsz, n, heads, dk)
    v = v.reshape(bsz, n, heads, dv)
    theta = 1.0 / (ROPE_BASE ** jnp.linspace(0.0, 1.0, dk // 2, dtype=_F32))
    ang = pos.astype(_F32)[:, None] * theta
    cos, sin = jnp.cos(ang)[:, None, :], jnp.sin(ang)[:, None, :]
    q = _rotate(q, cos, sin)
    k = _rotate(k, cos, sin) * (dk ** -0.5)
    ys, t0, rst = [], 0, r0
    for seg_len in segments:
        sl = slice(t0, t0 + seg_len)
        y, rst = _jx_ret_chunked(q[:, sl], k[:, sl], v[:, sl], rst)
        ys.append(y)
        t0 += seg_len
    o = jnp.concatenate(ys, axis=1).reshape(bsz, n, gw)
    return jax.nn.silu(gate) * _head_norm(o, heads, gn_w, gn_b, RET_GN_EPS), rst


def _complex_affine_combine(e1, e2):
    a1r, a1i, b1r, b1i = e1
    a2r, a2i, b2r, b2i = e2
    return (a1r * a2r - a1i * a2i, a1r * a2i + a1i * a2r,
            a2r * b1r - a2i * b1i + b2r, a2r * b1i + a2i * b1r + b2i)


def _jx_s5(u, s0_re, s0_im, a_re, a_im, log_dt, b_re, b_im, c_re, c_im, d, glu_w, glu_b, norm_w):
    bsz, n, gw = u.shape
    groups, ns = a_re.shape
    dt = jnp.exp(log_dt)[:, None]
    lr, li = a_re, a_im
    mag = jnp.exp(lr * dt)
    abar_re, abar_im = mag * jnp.cos(li * dt), mag * jnp.sin(li * dt)
    den = lr * lr + li * li
    nr, ni = abar_re - 1.0, abar_im
    e_re, e_im = (nr * lr + ni * li) / den, (ni * lr - nr * li) / den
    bb_re = e_re[..., None] * b_re - e_im[..., None] * b_im
    bb_im = e_re[..., None] * b_im + e_im[..., None] * b_re
    ug = u.reshape(bsz, n, groups, S5_GROUP)
    bu_re = jnp.einsum('gnc,blgc->blgn', bb_re, ug)
    bu_im = jnp.einsum('gnc,blgc->blgn', bb_im, ug)
    ar_t = jnp.broadcast_to(abar_re, bu_re.shape)
    ai_t = jnp.broadcast_to(abar_im, bu_re.shape)
    cr, ci, xr, xi = lax.associative_scan(_complex_affine_combine, (ar_t, ai_t, bu_re, bu_im), axis=1)
    s0r, s0i = s0_re[:, None], s0_im[:, None]
    xr, xi = xr + cr * s0r - ci * s0i, xi + cr * s0i + ci * s0r
    y = jnp.einsum('gcn,blgn->blgc', c_re, xr) - jnp.einsum('gcn,blgn->blgc', c_im, xi)
    y = y.reshape(bsz, n, gw) + d * u
    gy = jax.nn.gelu(y)
    out = gy * jax.nn.sigmoid(gy @ glu_w + glu_b)
    return _rms(out, norm_w), xr[:, -1], xi[:, -1]


def kernel(x_prompt, x_sample, state_ssd, state_ssd_conv, state_rwkv, state_rwkv_shift, state_ret, state_s5_re,
           state_s5_im, meta, ln_mix, w_in, ssd_conv_w, ssd_conv_b, ssd_dt_bias, ssd_a_log, ssd_d, ssd_norm, rwkv_mu,
           rwkv_w0, rwkv_w2, rwkv_a0, rwkv_a2, rwkv_g2, rwkv_k_k, rwkv_k_a, rwkv_r_k, rwkv_ln_w, rwkv_ln_b, ret_gn_w,
           ret_gn_b, s5_a_re, s5_a_im, s5_log_dt, s5_b_re, s5_b_im, s5_c_re, s5_c_im, s5_d, s5_glu_w, s5_glu_b,
           s5_norm, w_out, ln_ffn, w_gate, w_up, w_down, ln_f):
    bp, sp, d = x_prompt.shape
    bs, ls, _ = x_sample.shape
    depth = w_in.shape[0]
    gw = d // 4
    ssd_heads, ssd_p, ssd_n = state_ssd.shape[2:]
    xbc_w = state_ssd_conv.shape[-1]
    rwkv_proj = state_rwkv_shift.shape[-1]
    ret_heads, ret_dk, ret_dv = state_ret.shape[2:]
    qkw = ret_heads * ret_dk
    dff = w_gate.shape[-1]

    lreal = N_META + sp
    lp = _round_up(lreal, CHUNK)
    pad = lp - lreal
    mp, ms = bp * lp, bs * ls
    m = mp + ms

    meta_b = jnp.broadcast_to(meta[None], (bp, N_META, d))
    hp = jnp.concatenate([jnp.zeros((bp, pad, d), _F32), meta_b, x_prompt], axis=1).reshape(mp, d)
    h = jnp.concatenate([hp, x_sample.reshape(ms, d)], axis=0)

    in_splits = (gw, xbc_w, ssd_heads, rwkv_proj, qkw, qkw, gw, gw, gw)
    offs = [0]
    for s in in_splits:
        offs.append(offs[-1] + s)
    dt_w = _round_up(ssd_heads, LANE)
    rw_w = _round_up(rwkv_proj, LANE)
    wz = lambda n: jnp.zeros((depth, d, n), w_in.dtype)
    w_in_p = jnp.concatenate([
        w_in[..., offs[0]:offs[3]], wz(dt_w - ssd_heads),
        w_in[..., offs[3]:offs[4]], wz(rw_w - rwkv_proj),
        w_in[..., offs[4]:offs[9]]], axis=-1).astype(_BF16)
    c_ssd = 0
    c_rw = gw + xbc_w + dt_w
    c_ret = c_rw + rw_w
    c_s5 = c_ret + 2 * qkw + 2 * gw
    w_out_b = w_out.astype(_BF16)
    w_gate_b = w_gate.astype(_BF16)
    w_up_b = w_up.astype(_BF16)
    w_down_b = w_down.astype(_BF16)
    half = dff // 2

    pos_p = jnp.arange(lreal)
    pos_s = PAST_LEN + jnp.arange(ls)
    zeros_like_b = lambda s: jnp.zeros((bp,) + s.shape[2:], _F32)

    outs_p = [[] for _ in range(7)]
    outs_s = [[] for _ in range(7)]
    for l in range(depth):
        hn = _rmsnorm(h, ln_mix[l], _BF16, lp=lp, pad=pad, mp=mp)
        proj = _mm(hn, w_in_p[l], name="in_proj")

        def grp(rows, c0, width):
            return rows[..., c0:c0 + width]

        ys = []
        for (rows, pos, segs, states, outs, bsz, n) in (
                (proj[:mp].reshape(bp, lp, -1)[:, pad:], pos_p, (N_META, sp),
                 [zeros_like_b(s) for s in (state_ssd, state_ssd_conv, state_rwkv, state_rwkv_shift, state_ret,
                                            state_s5_re, state_s5_im)], outs_p, bp, lreal),
                (proj[mp:].reshape(bs, ls, -1), pos_s, (ls,),
                 [s[l] for s in (state_ssd, state_ssd_conv, state_rwkv, state_rwkv_shift, state_ret,
                                 state_s5_re, state_s5_im)], outs_s, bs, ls)):
            z = grp(rows, c_ssd, gw)
            xbc = grp(rows, c_ssd + gw, xbc_w)
            dtr = grp(rows, c_ssd + gw + xbc_w, ssd_heads)
            pr = grp(rows, c_rw, rwkv_proj)
            rq = grp(rows, c_ret, qkw)
            rk = grp(rows, c_ret + qkw, qkw)
            rv = grp(rows, c_ret + 2 * qkw, gw)
            rg = grp(rows, c_ret + 2 * qkw + gw, gw)
            u = grp(rows, c_s5, gw)
            y_a, ssd_n_, conv_n = _jx_ssd(z, xbc, dtr, states[1], states[0], ssd_conv_w[l], ssd_conv_b[l],
                                          ssd_dt_bias[l], ssd_a_log[l], ssd_d[l], ssd_norm[l], segs)
            y_b, rwkv_n, shift_n = _jx_rwkv(pr, states[3], states[2], rwkv_mu[l], rwkv_w0[l], rwkv_w2[l],
                                            rwkv_a0[l], rwkv_a2[l], rwkv_g2[l], rwkv_k_k[l], rwkv_k_a[l],
                                            rwkv_r_k[l], rwkv_ln_w[l], rwkv_ln_b[l])
            y_c, ret_n = _jx_ret(rq, rk, rv, rg, pos, states[4], ret_gn_w[l], ret_gn_b[l], segs)
            y_d, s5r_n, s5i_n = _jx_s5(u, states[5], states[6], s5_a_re[l], s5_a_im[l], s5_log_dt[l], s5_b_re[l],
                                       s5_b_im[l], s5_c_re[l], s5_c_im[l], s5_d[l], s5_glu_w[l], s5_glu_b[l],
                                       s5_norm[l])
            ycat = jnp.concatenate([y_a, y_b, y_c, y_d], axis=-1)
            ys.append(ycat)
            for acc, s in zip(outs, (ssd_n_, conv_n, rwkv_n, shift_n, ret_n, s5r_n, s5i_n)):
                acc.append(s)
        yp = jnp.concatenate([jnp.zeros((bp, pad, d), _F32), ys[0]], axis=1).reshape(mp, d)
        ycat = jnp.concatenate([yp, ys[1].reshape(ms, d)], axis=0).astype(_BF16)
        h = _mm(ycat, w_out_b[l], res=h, name="out_proj")
        hn = _rmsnorm(h, ln_ffn[l], _BF16)
        ff = _swiglu(hn, w_gate_b[l], w_up_b[l])
        h = _mm(ff, w_down_b[l], res=h, kb=0, tk=half, name="ffn_down0")
        h = _mm(ff, w_down_b[l], res=h, kb=1, tk=half, name="ffn_down1")

    y = _rmsnorm(h, ln_f, _F32)
    y_prompt = y[:mp].reshape(bp, lp, d)[:, pad + N_META:]
    y_sample = y[mp:].reshape(bs, ls, d)
    return ((y_prompt, y_sample) + tuple(jnp.stack(a) for a in outs_p) + tuple(jnp.stack(a) for a in outs_s))
```

```python
import functools
import math

import jax
import jax.numpy as jnp
from jax import lax
from jax.experimental import pallas as pl
from jax.experimental.pallas import tpu as pltpu

N_META = 16
EPS = 1e-6
SSD_GROUPS = 2
RWKV_HEAD_DIM = 64
RWKV_GN_EPS = 64e-5
RET_GN_EPS = 1e-5
ROPE_BASE = 10000.0
S5_GROUP = 16
PAST_LEN = 16384
CHUNK = 128
LANE = 128
V7X_VMEM_CAP = 60 * 1024 * 1024

_F32 = jnp.float32
_BF16 = jnp.bfloat16


def _pick(n, cands):
    for c in cands:
        if n % c == 0:
            return c
    raise ValueError(f"no tile in {cands} divides {n}")


def _round_up(n, m):
    return -(-n // m) * m


def _cparams(sem, *block_bytes):
    need = 2 * sum(block_bytes) + (6 << 20)
    return pltpu.CompilerParams(dimension_semantics=sem, vmem_limit_bytes=int(min(max(need, 16 << 20), V7X_VMEM_CAP)))


def _rmsnorm_kernel(x_ref, g_ref, o_ref, *, tr, lp, pad, mp):
    x = x_ref[...]
    y = x * lax.rsqrt(jnp.mean(x * x, axis=-1, keepdims=True) + EPS) * g_ref[...]
    if pad:
        row0 = pl.program_id(0) * tr
        pos0 = lax.rem(row0, lp)
        rows = lax.broadcasted_iota(jnp.int32, (tr, 1), 0)
        is_pad = jnp.logical_and(row0 < mp, pos0 + rows < pad)
        y = jnp.where(is_pad, 0.0, y)
    o_ref[...] = y.astype(o_ref.dtype)


def _rmsnorm(x, g, out_dtype, lp=0, pad=0, mp=0):
    m, d = x.shape
    tr = _pick(math.gcd(m, lp) if pad else m, (256, 128, 64, 32, 16, 8))
    kern = functools.partial(_rmsnorm_kernel, tr=tr, lp=lp, pad=pad, mp=mp)
    return pl.pallas_call(
        kern, grid=(m // tr,),
        in_specs=[pl.BlockSpec((tr, d), lambda i: (i, 0)), pl.BlockSpec((1, d), lambda i: (0, 0))],
        out_specs=pl.BlockSpec((tr, d), lambda i: (i, 0)),
        out_shape=jax.ShapeDtypeStruct((m, d), out_dtype),
        compiler_params=_cparams(("parallel",), tr * d * 4, tr * d * 4),
        name="rmsnorm",
    )(x, g.reshape(1, d).astype(_F32))


def _mm_kernel(*refs, has_res):
    if has_res:
        x_ref, w_ref, r_ref, o_ref = refs
    else:
        x_ref, w_ref, o_ref = refs
    acc = jnp.dot(x_ref[...], w_ref[...], preferred_element_type=_F32)
    if has_res:
        acc = acc + r_ref[...]
    o_ref[...] = acc.astype(o_ref.dtype)


def _mm(x, w, res=None, out_dtype=_F32, kb=0, tk=None, name="matmul"):
    m = x.shape[0]
    n = w.shape[1]
    tk = tk or w.shape[0]
    tm = _pick(m, (1024, 512, 256, 128))
    if tk > 4096:
        tm = _pick(m, (512, 256, 128))
    tn = _pick(n, (512, 256, 128))
    in_specs = [pl.BlockSpec((tm, tk), lambda i, j: (i, kb)), pl.BlockSpec((tk, tn), lambda i, j: (kb, j))]
    args = [x, w]
    blocks = [tm * tk * 2, tk * tn * 2, tm * tn * 4]
    if res is not None:
        in_specs.append(pl.BlockSpec((tm, tn), lambda i, j: (i, j)))
        args.append(res)
        blocks.append(tm * tn * 4)
    return pl.pallas_call(
        functools.partial(_mm_kernel, has_res=res is not None), grid=(m // tm, n // tn),
        in_specs=in_specs, out_specs=pl.BlockSpec((tm, tn), lambda i, j: (i, j)),
        out_shape=jax.ShapeDtypeStruct((m, n), out_dtype),
        compiler_params=_cparams(("parallel", "arbitrary"), *blocks),
        name=name,
    )(*args)


def _swiglu_kernel(x_ref, wg_ref, wu_ref, o_ref):
    x = x_ref[...]
    g = jnp.dot(x, wg_ref[...], preferred_element_type=_F32)
    u = jnp.dot(x, wu_ref[...], preferred_element_type=_F32)
    o_ref[...] = (g * jax.nn.sigmoid(g) * u).astype(o_ref.dtype)


def _swiglu(x, wg, wu):
    m, k = x.shape
    n = wg.shape[1]
    tm = _pick(m, (1024, 512, 256, 128))
    tn = _pick(n, (256, 128))
    return pl.pallas_call(
        _swiglu_kernel, grid=(m // tm, n // tn),
        in_specs=[pl.BlockSpec((tm, k), lambda i, j: (i, 0)), pl.BlockSpec((k, tn), lambda i, j: (0, j)),
                  pl.BlockSpec((k, tn), lambda i, j: (0, j))],
        out_specs=pl.BlockSpec((tm, tn), lambda i, j: (i, j)),
        out_shape=jax.ShapeDtypeStruct((m, n), _BF16),
        compiler_params=_cparams(("parallel", "arbitrary"), tm * k * 2, 2 * k * tn * 2, tm * tn * 2),
        name="swiglu",
    )(x, wg, wu)


_NT = (((1,), (1,)), ((), ()))
_TN = (((0,), (0,)), ((), ()))


def _silu(x):
    return x * jax.nn.sigmoid(x)


def _softplus(x):
    return jnp.maximum(x, 0.0) + jnp.log(1.0 + jnp.exp(-jnp.abs(x)))


def _split3(x):
    hi = x.astype(_BF16)
    r = x - hi.astype(_F32)
    mid = r.astype(_BF16)
    lo = (r - mid.astype(_F32)).astype(_BF16)
    return hi, mid, lo


def _dot_sel(sel, x, dims):
    parts = _split3(x) if _BF16 == jnp.bfloat16 else (x,)
    out = None
    for p in parts:
        t = lax.dot_general(sel.astype(p.dtype), p, dims, preferred_element_type=_F32)
        out = t if out is None else out + t
    return out


def _dot_sel_r(x, sel, dims):
    parts = _split3(x) if _BF16 == jnp.bfloat16 else (x,)
    out = None
    for p in parts:
        t = lax.dot_general(p, sel.astype(p.dtype), dims, preferred_element_type=_F32)
        out = t if out is None else out + t
    return out


def _full(shape):
    nd = len(shape)
    return pl.BlockSpec(shape, lambda *_: (0,) * nd)


def _ssd_kernel(p_ref, h0_ref, cw_ref, cb_ref, dtb_ref, alog_ref, dsk_ref, nw_ref, y_ref, ho_ref, hst, xbuf, *,
                T, npad, nch, gw, xbc_w, heads, P, N, K):
    c = pl.program_id(1)
    G = SSD_GROUPS
    hpg = heads // G

    @pl.when(c == 0)
    def _init():
        hst[...] = h0_ref[0]
        xbuf[0:8, :] = jnp.zeros((8, xbc_w), _F32)

    xbuf[8:8 + T, :] = p_ref[:, gw:gw + xbc_w]
    conv = cb_ref[...]
    for j in range(K):
        conv = conv + cw_ref[j:j + 1, :] * xbuf[8 - (K - 1) + j:8 - (K - 1) + j + T, :]
    hist = xbuf[8 + T - (K - 1):8 + T, :]
    xbuf[8 - (K - 1):8, :] = hist
    xbc = _silu(conv)
    xs = xbc[:, :gw]
    bm = xbc[:, gw:gw + G * N]
    cm = xbc[:, gw + G * N:gw + 2 * G * N]
    z = p_ref[:, 0:gw]

    dt = _softplus(p_ref[:, gw + xbc_w:gw + xbc_w + LANE] + dtb_ref[...])
    if npad:
        rows = lax.broadcasted_iota(jnp.int32, (T, 1), 0)
        dt = jnp.where(jnp.logical_and(c == 0, rows < npad), 0.0, dt)
    la = dt * (-jnp.exp(alog_ref[...]))
    ri = lax.broadcasted_iota(jnp.int32, (T, T), 0)
    ci = lax.broadcasted_iota(jnp.int32, (T, T), 1)
    causal = ri >= ci
    cum = _dot_sel(causal.astype(_F32), la, (((1,), (0,)), ((), ())))
    hq = lax.broadcasted_iota(jnp.int32, (heads * T, LANE), 0) // T
    ln = lax.broadcasted_iota(jnp.int32, (heads * T, LANE), 1)
    rowb = _dot_sel((ln == hq).astype(_F32), cum, _NT)
    ecum = jnp.exp(cum)
    clast = cum[T - 1:T, :]
    cdec = jnp.exp(clast)
    dte = jnp.exp(clast - cum) * dt

    ys = []
    for g in range(G):
        bm_g = bm[:, g * N:(g + 1) * N].astype(_BF16)
        cm_g = cm[:, g * N:(g + 1) * N].astype(_BF16)
        cb = lax.dot_general(cm_g, bm_g, _NT, preferred_element_type=_F32)
        for hh in range(hpg):
            h = g * hpg + hh
            seg = cum[:, h:h + 1] - rowb[h * T:(h + 1) * T, :]
            lm = jnp.exp(jnp.where(causal, seg, -jnp.inf))
            xh = xs[:, h * P:(h + 1) * P]
            y_diag = jnp.dot((cb * lm).astype(_BF16), (xh * dt[:, h:h + 1]).astype(_BF16),
                             preferred_element_type=_F32)
            hprev = hst[h]
            y_off = lax.dot_general(cm_g, hprev.astype(_BF16), _NT, preferred_element_type=_F32) * ecum[:, h:h + 1]
            xw = (xh * dte[:, h:h + 1]).astype(_BF16)
            hst[h] = hprev * cdec[:, h:h + 1] + lax.dot_general(xw, bm_g, _TN, preferred_element_type=_F32)
            ys.append(y_diag + y_off + xh * dsk_ref[:, h:h + 1])
    y = jnp.concatenate(ys, axis=-1) * _silu(z)
    y = y * lax.rsqrt(jnp.mean(y * y, axis=-1, keepdims=True) + EPS) * nw_ref[...]
    y_ref[...] = y.astype(y_ref.dtype)

    @pl.when(c == nch - 1)
    def _fin():
        ho_ref[0] = hst[...]


def _ssd_mixer(p, h0, conv_w, conv_b, dt_bias, a_log, d_skip, norm_w, *, nseq, nch, T, npad, gw):
    heads, P, N = h0.shape[1:]
    K, xbc_w = conv_w.shape
    wp = p.shape[1]
    padl = lambda v: jnp.pad(v.astype(_F32), (0, LANE - v.shape[0])).reshape(1, LANE)
    kern = functools.partial(_ssd_kernel, T=T, npad=npad, nch=nch, gw=gw, xbc_w=xbc_w, heads=heads, P=P, N=N, K=K)
    return pl.pallas_call(
        kern, grid=(nseq, nch),
        in_specs=[pl.BlockSpec((T, wp), lambda b, c: (b * nch + c, 0)),
                  pl.BlockSpec((1, heads, P, N), lambda b, c: (b, 0, 0, 0)),
                  _full((K, xbc_w)), _full((1, xbc_w)), _full((1, LANE)), _full((1, LANE)), _full((1, LANE)),
                  _full((1, gw))],
        out_specs=[pl.BlockSpec((T, gw), lambda b, c: (b * nch + c, 0)),
                   pl.BlockSpec((1, heads, P, N), lambda b, c: (b, 0, 0, 0))],
        out_shape=[jax.ShapeDtypeStruct((nseq * nch * T, gw), _BF16),
                   jax.ShapeDtypeStruct((nseq, heads, P, N), _F32)],
        scratch_shapes=[pltpu.VMEM((heads, P, N), _F32), pltpu.VMEM((T + 8, xbc_w), _F32)],
        compiler_params=_cparams(("parallel", "arbitrary"), T * wp * 4, 3 * heads * P * N * 4, T * gw * 2,
                                 (T + 8) * xbc_w * 4, 24 * T * max(T, LANE) * 4),
        name="ssd_mixer",
    )(p, h0, conv_w.astype(_F32), conv_b.reshape(1, xbc_w).astype(_F32), padl(dt_bias), padl(a_log), padl(d_skip),
      norm_w.reshape(1, gw).astype(_F32))


def _ret_kernel(p_ref, cc_ref, ss_ref, r0_ref, gw_ref, gb_ref, y_ref, ro_ref, rst, *, T, npad, nch, gw, heads, dk, dv):
    c = pl.program_id(1)
    qkw = heads * dk

    @pl.when(c == 0)
    def _init():
        rst[...] = r0_ref[0]

    npc = jnp.where(c == 0, npad, 0).astype(_F32)
    ri = lax.broadcasted_iota(jnp.int32, (T, T), 0)
    ci = lax.broadcasted_iota(jnp.int32, (T, T), 1)
    causal = ri >= ci
    dlt = (ri - ci).astype(_F32)
    idx = lax.broadcasted_iota(jnp.int32, (T, 1), 0).astype(_F32)
    cc = cc_ref[...]
    ss = ss_ref[...]
    ys = []
    for h in range(heads):
        lg = math.log(1.0 - 2.0 ** (-5.0 - h))
        qh = p_ref[:, h * dk:(h + 1) * dk]
        kh = p_ref[:, qkw + h * dk:qkw + (h + 1) * dk]
        vh = p_ref[:, 2 * qkw + h * dv:2 * qkw + (h + 1) * dv].astype(_BF16)
        qh = (qh * cc + pltpu.roll(qh, dk // 2, 1) * ss)
        kh = (kh * cc + pltpu.roll(kh, dk // 2, 1) * ss) * (dk ** -0.5)
        qb = qh.astype(_BF16)
        dmat = jnp.exp(jnp.where(causal, dlt * lg, -jnp.inf))
        inner = lax.dot_general(qb, kh.astype(_BF16), _NT, preferred_element_type=_F32) * dmat
        y_in = jnp.dot(inner.astype(_BF16), vh, preferred_element_type=_F32)
        rprev = rst[h]
        y_x = jnp.dot(qb, rprev.astype(_BF16), preferred_element_type=_F32) * jnp.exp((idx + 1.0 - npc) * lg)
        kw = (kh * jnp.exp((T - 1.0 - idx) * lg)).astype(_BF16)
        rst[h] = rprev * jnp.exp((T - npc) * lg) + lax.dot_general(kw, vh, _TN, preferred_element_type=_F32)
        o = y_in + y_x
        oc = o - jnp.mean(o, axis=-1, keepdims=True)
        ys.append(oc * lax.rsqrt(jnp.mean(oc * oc, axis=-1, keepdims=True) + RET_GN_EPS))
    gate = p_ref[:, 2 * qkw + gw:2 * qkw + 2 * gw]
    y = _silu(gate) * (jnp.concatenate(ys, axis=-1) * gw_ref[...] + gb_ref[...])
    y_ref[...] = y.astype(y_ref.dtype)

    @pl.when(c == nch - 1)
    def _fin():
        ro_ref[0] = rst[...]


def _ret_mixer(p, r0, gn_w, gn_b, pos0, *, nseq, nch, T, npad, gw):
    heads, dk, dv = r0.shape[1:]
    wp = p.shape[1]
    theta = 1.0 / (ROPE_BASE ** jnp.linspace(0.0, 1.0, dk // 2, dtype=_F32))
    ang = (pos0 + jnp.arange(nch * T) - npad).astype(_F32)[:, None] * theta
    cos, sin = jnp.cos(ang), jnp.sin(ang)
    cc = jnp.concatenate([cos, cos], axis=-1)
    ss = jnp.concatenate([-sin, sin], axis=-1)
    kern = functools.partial(_ret_kernel, T=T, npad=npad, nch=nch, gw=gw, heads=heads, dk=dk, dv=dv)
    return pl.pallas_call(
        kern, grid=(nseq, nch),
        in_specs=[pl.BlockSpec((T, wp), lambda b, c: (b * nch + c, 0)),
                  pl.BlockSpec((T, dk), lambda b, c: (c, 0)), pl.BlockSpec((T, dk), lambda b, c: (c, 0)),
                  pl.BlockSpec((1, heads, dk, dv), lambda b, c: (b, 0, 0, 0)),
                  _full((1, gw)), _full((1, gw))],
        out_specs=[pl.BlockSpec((T, gw), lambda b, c: (b * nch + c, 0)),
                   pl.BlockSpec((1, heads, dk, dv), lambda b, c: (b, 0, 0, 0))],
        out_shape=[jax.ShapeDtypeStruct((nseq * nch * T, gw), _BF16),
                   jax.ShapeDtypeStruct((nseq, heads, dk, dv), _F32)],
        scratch_shapes=[pltpu.VMEM((heads, dk, dv), _F32)],
        compiler_params=_cparams(("parallel", "arbitrary"), T * wp * 4, 3 * heads * dk * dv * 4, T * gw * 2,
                                 16 * T * max(T, dv) * 4),
        name="ret_mixer",
    )(p, cc, ss, r0, gn_w.reshape(1, gw).astype(_F32), gn_b.reshape(1, gw).astype(_F32))


S5_CB = 128


def _gelu_tanh(x):
    return 0.5 * x * (1.0 + jnp.tanh(math.sqrt(2.0 / math.pi) * (x + 0.044715 * (x * x * x))))


def _s5_kernel(u_ref, s0r_ref, s0i_ref, are_ref, aim_ref, ldt_ref, wbr_ref, wbi_ref, wcr_ref, wci_ref, d_ref,
               gluw_ref, glub_ref, nw_ref, y_ref, sor_ref, soi_ref, xr, xi, str_, sti, *, TC, nb, npad, nch, gw, sb):
    c = pl.program_id(1)
    nblk = gw // S5_CB

    @pl.when(c == 0)
    def _init():
        str_[...] = s0r_ref[...]
        sti[...] = s0i_ref[...]

    dt = jnp.exp(ldt_ref[...])
    lr, li = are_ref[...], aim_ref[...]
    mag = jnp.exp(lr * dt)
    abr, abi = mag * jnp.cos(li * dt), mag * jnp.sin(li * dt)
    den = lr * lr + li * li
    nr, ni = abr - 1.0, abi
    er, ei = (nr * lr + ni * li) / den, (ni * lr - nr * li) / den

    u = u_ref[...]
    ub = u.astype(_BF16)
    for g in range(nblk):
        us = ub[:, g * S5_CB:(g + 1) * S5_CB]
        br = jnp.dot(us, wbr_ref[g], preferred_element_type=_F32)
        bi = jnp.dot(us, wbi_ref[g], preferred_element_type=_F32)
        e_r, e_i = er[:, g * sb:(g + 1) * sb], ei[:, g * sb:(g + 1) * sb]
        xr[:, g * sb:(g + 1) * sb] = e_r * br - e_i * bi
        xi[:, g * sb:(g + 1) * sb] = e_r * bi + e_i * br

    rows = max(nb, 8)
    spi = rows // nb

    def body(i, carry):
        r0 = pl.multiple_of(i * rows, 8)
        b_r, b_i = xr[pl.ds(r0, rows), :], xi[pl.ds(r0, rows), :]
        s_r, s_i = str_[...], sti[...]
        outs_r, outs_i = [], []
        for j in range(spi):
            n_r = abr * s_r - abi * s_i + b_r[j * nb:(j + 1) * nb]
            n_i = abr * s_i + abi * s_r + b_i[j * nb:(j + 1) * nb]
            s_r, s_i = n_r, n_i
            outs_r.append(n_r)
            outs_i.append(n_i)
        xr[pl.ds(r0, rows), :] = outs_r[0] if spi == 1 else jnp.concatenate(outs_r, axis=0)
        xi[pl.ds(r0, rows), :] = outs_i[0] if spi == 1 else jnp.concatenate(outs_i, axis=0)
        str_[...] = s_r
        sti[...] = s_i
        return carry

    start = jnp.clip(npad - c * TC, 0, TC) // spi
    lax.fori_loop(start, TC // spi, body, 0)

    ys = []
    for g in range(nblk):
        xrb = xr[:, g * sb:(g + 1) * sb].astype(_BF16)
        xib = xi[:, g * sb:(g + 1) * sb].astype(_BF16)
        ys.append(jnp.dot(xrb, wcr_ref[g], preferred_element_type=_F32)
                  - jnp.dot(xib, wci_ref[g], preferred_element_type=_F32))
    y = jnp.concatenate(ys, axis=-1) + d_ref[...] * u
    gy = _gelu_tanh(y)
    out = gy * jax.nn.sigmoid(jnp.dot(gy.astype(_BF16), gluw_ref[...], preferred_element_type=_F32) + glub_ref[...])
    out = out * lax.rsqrt(jnp.mean(out * out, axis=-1, keepdims=True) + EPS) * nw_ref[...]
    y_ref[...] = out.astype(y_ref.dtype)

    @pl.when(c == nch - 1)
    def _fin():
        sor_ref[...] = str_[...]
        soi_ref[...] = sti[...]


def _s5_mixer(u, s0_re, s0_im, a_re, a_im, log_dt, b_re, b_im, c_re, c_im, d, glu_w, glu_b, norm_w, *,
              nseq, nch, T, npad, nb, TC):
    groups, ns = a_re.shape
    gw = groups * S5_GROUP
    gpb = S5_CB // S5_GROUP
    nblk = gw // S5_CB
    sb = gpb * ns
    nst = groups * ns
    nsb = nseq // nb
    ncc = nch * T // TC
    assert npad % max(1, 8 // nb) == 0
    ut = u[:, :gw].reshape(nsb, nb, ncc, TC, gw).transpose(0, 2, 3, 1, 4).reshape(nseq * nch * T, gw)
    eye = jnp.eye(gpb, dtype=_F32)

    def bd_in(w):
        w4 = w.reshape(nblk, gpb, ns, S5_GROUP)
        return jnp.einsum('bgnc,gh->bgchn', w4, eye).reshape(nblk, S5_CB, sb).astype(_BF16)

    def bd_out(w):
        w4 = w.reshape(nblk, gpb, S5_GROUP, ns)
        return jnp.einsum('bgcn,gh->bgnhc', w4, eye).reshape(nblk, sb, S5_CB).astype(_BF16)

    row = lambda v: v.reshape(1, -1).astype(_F32)
    kern = functools.partial(_s5_kernel, TC=TC, nb=nb, npad=npad, nch=ncc, gw=gw, sb=sb)
    R = TC * nb
    y, so_r, so_i = pl.pallas_call(
        kern, grid=(nsb, ncc),
        in_specs=[pl.BlockSpec((R, gw), lambda s, c: (s * ncc + c, 0)),
                  pl.BlockSpec((nb, nst), lambda s, c: (s, 0)), pl.BlockSpec((nb, nst), lambda s, c: (s, 0)),
                  _full((1, nst)), _full((1, nst)), _full((1, nst)),
                  _full((nblk, S5_CB, sb)), _full((nblk, S5_CB, sb)), _full((nblk, sb, S5_CB)),
                  _full((nblk, sb, S5_CB)), _full((1, gw)), _full((gw, gw)), _full((1, gw)), _full((1, gw))],
        out_specs=[pl.BlockSpec((R, gw), lambda s, c: (s * ncc + c, 0)),
                   pl.BlockSpec((nb, nst), lambda s, c: (s, 0)), pl.BlockSpec((nb, nst), lambda s, c: (s, 0))],
        out_shape=[jax.ShapeDtypeStruct((nseq * nch * T, gw), _BF16),
                   jax.ShapeDtypeStruct((nseq, nst), _F32), jax.ShapeDtypeStruct((nseq, nst), _F32)],
        scratch_shapes=[pltpu.VMEM((R, nst), _F32), pltpu.VMEM((R, nst), _F32),
                        pltpu.VMEM((nb, nst), _F32), pltpu.VMEM((nb, nst), _F32)],
        compiler_params=_cparams(("parallel", "arbitrary"), R * gw * 4, R * gw * 2, 4 * nblk * S5_CB * sb * 2,
                                 gw * gw * 2, R * nst * 4, 6 * max(nb, 8) * nst * 4),
        name="s5_mixer",
    )(ut, s0_re.reshape(nseq, nst), s0_im.reshape(nseq, nst), row(a_re), row(a_im),
      row(jnp.broadcast_to(log_dt[:, None], (groups, ns))), bd_in(b_re), bd_in(b_im), bd_out(c_re), bd_out(c_im),
      row(d), glu_w.astype(_BF16), row(glu_b), row(norm_w))
    y = y.reshape(nsb, ncc, TC, nb, gw).transpose(0, 3, 1, 2, 4).reshape(nseq * nch * T, gw)
    return y, so_r.reshape(nseq, groups, ns), so_i.reshape(nseq, groups, ns)


def _head_sums(x):
    ri = lax.broadcasted_iota(jnp.int32, (LANE, LANE), 0) // RWKV_HEAD_DIM
    ci = lax.broadcasted_iota(jnp.int32, (LANE, LANE), 1) // RWKV_HEAD_DIM
    e = (ri == ci).astype(_F32)
    nn = (((1,), (0,)), ((), ()))
    return jnp.concatenate([_dot_sel_r(x[:, j:j + LANE], e, nn) for j in range(0, x.shape[1], LANE)], axis=-1)


def _rwkv_pre_kernel(p_ref, mu_ref, w0_ref, w2_ref, a0_ref, a2_ref, g2_ref, kk_ref, ka_ref, rk_ref,
                     r_o, k_o, v_o, w_o, kk_o, b_o, g_o, bon_o, xbuf, *, T, gw, dl, da, dg):
    c = pl.program_id(1)

    @pl.when(c == 0)
    def _init():
        xbuf[0:8, :] = jnp.zeros((8, xbuf.shape[1]), _F32)

    p = p_ref[...]
    xbuf[8:8 + T, :] = p
    prev = xbuf[7:7 + T, :]
    last = xbuf[7 + T:8 + T, :]
    xbuf[7:8, :] = last
    pm = p + (prev - p) * mu_ref[...]
    r, k, v = pm[:, :gw], pm[:, gw:2 * gw], pm[:, 2 * gw:3 * gw]
    o1 = 3 * gw
    wl, al, gl = pm[:, o1:o1 + dl], pm[:, o1 + dl:o1 + dl + da], pm[:, o1 + dl + da:o1 + dl + da + dg]
    wx = w0_ref[...] + jnp.dot(jnp.tanh(wl).astype(_BF16), w2_ref[...], preferred_element_type=_F32)
    decay = jnp.exp(-jnp.exp(-_softplus(-wx) - 0.5))
    a = jax.nn.sigmoid(a0_ref[...] + jnp.dot(al.astype(_BF16), a2_ref[...], preferred_element_type=_F32))
    g = jnp.dot(jax.nn.sigmoid(gl).astype(_BF16), g2_ref[...], preferred_element_type=_F32)
    kkf = k * kk_ref[...]
    kk = kkf / jnp.maximum(jnp.sqrt(_head_sums(kkf * kkf)), 1e-12)
    k2 = k * (1.0 + (a - 1.0) * ka_ref[...])
    r_o[...] = r
    k_o[...] = k2
    v_o[...] = v
    w_o[...] = decay
    kk_o[...] = kk
    b_o[...] = kk * a
    g_o[...] = g
    bon_o[...] = _head_sums(r * k2 * rk_ref[...]) * v


def _rwkv_scan_kernel(r_ref, k_ref, v_ref, w_ref, kk_ref, b_ref, s0_ref, o_ref, so_ref, st, *, TC, VI, npad, nch):
    c = pl.program_id(1)

    @pl.when(c == 0)
    def _init():
        st[...] = s0_ref[...]

    start = jnp.clip(npad - c * TC, 0, TC)

    @pl.when(start > 0)
    def _zero():
        o_ref[...] = jnp.zeros(o_ref.shape, _F32)

    def body(t, carry):
        kk_t, w_t, b_t, k_t, r_t = kk_ref[t], w_ref[t], b_ref[t], k_ref[t], r_ref[t]
        for vi in range(VI):
            s = st[vi]
            sa = jnp.sum(s * kk_t, axis=0, keepdims=True)
            s = s * w_t - sa * b_t + v_ref[t, pl.ds(vi, 1), :] * k_t
            st[vi] = s
            o_ref[t, pl.ds(vi, 1), :] = jnp.sum(s * r_t, axis=0, keepdims=True)
        return carry

    lax.fori_loop(start, TC, body, 0)

    @pl.when(c == nch - 1)
    def _fin():
        so_ref[...] = st[...]


def _rwkv_post_kernel(o_ref, bon_ref, g_ref, lw_ref, lb_ref, y_ref):
    o = o_ref[...]
    oc = o - _head_sums(o) * (1.0 / RWKV_HEAD_DIM)
    var = _head_sums(oc * oc) * (1.0 / RWKV_HEAD_DIM)
    y = (oc * lax.rsqrt(var + RWKV_GN_EPS) * lw_ref[...] + lb_ref[...] + bon_ref[...]) * g_ref[...]
    y_ref[...] = y.astype(y_ref.dtype)


def _rwkv_mixer(p, s0, mu_p, w0, w2_p, a0, a2_p, g2_p, k_k, k_a, r_k, ln_w, ln_b, *, nseq, nch, T, npad, gw,
                dl, da, dg, J, TC):
    H, V, K = s0.shape[1:]
    wp = p.shape[1]
    rows = nseq * nch * T
    L = nch * T
    row = lambda v: v.reshape(1, -1).astype(_F32)
    f32rows = jax.ShapeDtypeStruct((rows, gw), _F32)
    blk = pl.BlockSpec((T, gw), lambda b, c: (b * nch + c, 0))
    pre = pl.pallas_call(
        functools.partial(_rwkv_pre_kernel, T=T, gw=gw, dl=dl, da=da, dg=dg), grid=(nseq, nch),
        in_specs=[pl.BlockSpec((T, wp), lambda b, c: (b * nch + c, 0)), _full((1, wp)), _full((1, gw)),
                  _full((dl, gw)), _full((1, gw)), _full((da, gw)), _full((dg, gw)), _full((1, gw)), _full((1, gw)),
                  _full((1, gw))],
        out_specs=[blk] * 8, out_shape=[f32rows] * 8,
        scratch_shapes=[pltpu.VMEM((T + 8, wp), _F32)],
        compiler_params=_cparams(("parallel", "arbitrary"), 2 * T * wp * 4, 8 * T * gw * 4, 12 * T * gw * 4),
        name="rwkv_pre",
    )(p, row(mu_p), row(w0), w2_p.astype(_BF16), row(a0), a2_p.astype(_BF16), g2_p.astype(_BF16), row(k_k),
      row(k_a), row(r_k))
    r, k2, v, decay, kk, bvec, g, bonus = pre

    VI = V // J
    NL = J * nseq * H
    assert NL % LANE == 0 or NL < LANE

    def kvec(x):
        y = x.reshape(nseq, L, H, K).transpose(1, 3, 0, 2).reshape(L, K, nseq * H)
        return jnp.concatenate([y] * J, axis=-1)

    vv = v.reshape(nseq, L, H, J, VI).transpose(1, 4, 3, 0, 2).reshape(L, VI, NL)
    st0 = s0.reshape(nseq, H, J, VI, K).transpose(3, 4, 2, 0, 1).reshape(VI, K, NL)
    lb = min(NL, LANE)
    ncc = L // TC
    kspec = pl.BlockSpec((TC, K, lb), lambda n, c: (c, 0, n))
    vspec = pl.BlockSpec((TC, VI, lb), lambda n, c: (c, 0, n))
    sspec = pl.BlockSpec((VI, K, lb), lambda n, c: (0, 0, n))
    o, st1 = pl.pallas_call(
        functools.partial(_rwkv_scan_kernel, TC=TC, VI=VI, npad=npad, nch=ncc), grid=(NL // lb, ncc),
        in_specs=[kspec, kspec, vspec, kspec, kspec, kspec, sspec],
        out_specs=[vspec, sspec],
        out_shape=[jax.ShapeDtypeStruct((L, VI, NL), _F32), jax.ShapeDtypeStruct((VI, K, NL), _F32)],
        scratch_shapes=[pltpu.VMEM((VI, K, lb), _F32)],
        compiler_params=_cparams(("parallel", "arbitrary"), 5 * TC * K * lb * 4, 2 * TC * VI * lb * 4,
                                 3 * VI * K * lb * 4),
        name="rwkv_scan",
    )(kvec(r), kvec(k2), vv, kvec(decay), kvec(kk), kvec(bvec), st0)
    o = o.reshape(L, VI, J, nseq, H).transpose(3, 0, 4, 2, 1).reshape(rows, gw)
    s_new = st1.reshape(VI, K, J, nseq, H).transpose(3, 4, 2, 0, 1).reshape(nseq, H, V, K)

    tr = _pick(rows, (256, 128, 64, 32, 16, 8))
    rblk = pl.BlockSpec((tr, gw), lambda i: (i, 0))
    y = pl.pallas_call(
        _rwkv_post_kernel, grid=(rows // tr,),
        in_specs=[rblk, rblk, rblk, _full((1, gw)), _full((1, gw))],
        out_specs=rblk, out_shape=jax.ShapeDtypeStruct((rows, gw), _BF16),
        compiler_params=_cparams(("parallel",), 4 * tr * gw * 4, 8 * tr * gw * 4),
        name="rwkv_post",
    )(o, bonus, g, row(ln_w), row(ln_b))
    return y, s_new


def kernel(x_prompt, x_sample, state_ssd, state_ssd_conv, state_rwkv, state_rwkv_shift, state_ret, state_s5_re,
           state_s5_im, meta, ln_mix, w_in, ssd_conv_w, ssd_conv_b, ssd_dt_bias, ssd_a_log, ssd_d, ssd_norm, rwkv_mu,
           rwkv_w0, rwkv_w2, rwkv_a0, rwkv_a2, rwkv_g2, rwkv_k_k, rwkv_k_a, rwkv_r_k, rwkv_ln_w, rwkv_ln_b, ret_gn_w,
           ret_gn_b, s5_a_re, s5_a_im, s5_log_dt, s5_b_re, s5_b_im, s5_c_re, s5_c_im, s5_d, s5_glu_w, s5_glu_b,
           s5_norm, w_out, ln_ffn, w_gate, w_up, w_down, ln_f):
    bp, sp, d = x_prompt.shape
    bs, ls, _ = x_sample.shape
    depth = w_in.shape[0]
    gw = d // 4
    ssd_heads, ssd_p, ssd_n = state_ssd.shape[2:]
    xbc_w = state_ssd_conv.shape[-1]
    rwkv_proj = state_rwkv_shift.shape[-1]
    ret_heads, ret_dk, ret_dv = state_ret.shape[2:]
    qkw = ret_heads * ret_dk
    dff = w_gate.shape[-1]

    lreal = N_META + sp
    lp = _round_up(lreal, CHUNK)
    pad = lp - lreal
    mp, ms = bp * lp, bs * ls
    m = mp + ms

    meta_b = jnp.broadcast_to(meta[None], (bp, N_META, d))
    hp = jnp.concatenate([jnp.zeros((bp, pad, d), _F32), meta_b, x_prompt], axis=1).reshape(mp, d)
    h = jnp.concatenate([hp, x_sample.reshape(ms, d)], axis=0)

    in_splits = (gw, xbc_w, ssd_heads, rwkv_proj, qkw, qkw, gw, gw, gw)
    offs = [0]
    for s in in_splits:
        offs.append(offs[-1] + s)
    ntile = 512 if gw % 512 == 0 else LANE

    def pack(c0, c1):
        w = w_in[..., c0:c1]
        return jnp.pad(w, ((0, 0), (0, 0), (0, _round_up(c1 - c0, ntile) - (c1 - c0)))).astype(_BF16)

    w_ssd_b, w_ret_b, w_s5_b = pack(offs[0], offs[3]), pack(offs[4], offs[8]), pack(offs[8], offs[9])

    lora_c = (rwkv_w2.shape[1], rwkv_a2.shape[1], rwkv_g2.shape[1])
    dl, da, dg = (_round_up(n, LANE) for n in lora_c)
    rw_w = _round_up(3 * gw + dl + da + dg, ntile)

    def rw_pad(x):
        parts, o = [x[..., :3 * gw]], 3 * gw
        for n, npd in zip(lora_c, (dl, da, dg)):
            parts.append(jnp.pad(x[..., o:o + n], [(0, 0)] * (x.ndim - 1) + [(0, npd - n)]))
            o += n
        y = jnp.concatenate(parts, axis=-1)
        return jnp.pad(y, [(0, 0)] * (x.ndim - 1) + [(0, rw_w - y.shape[-1])])

    def rw_compact(x):
        parts, o = [x[..., :3 * gw]], 3 * gw
        for n, npd in zip(lora_c, (dl, da, dg)):
            parts.append(x[..., o:o + n])
            o += npd
        return jnp.concatenate(parts, axis=-1)

    w_rw_b = rw_pad(w_in[..., offs[3]:offs[4]]).astype(_BF16)
    mu_p = rw_pad(rwkv_mu)
    padrows = lambda w, n: jnp.pad(w, ((0, 0), (0, n - w.shape[1]), (0, 0)))
    w2_p, a2_p, g2_p = padrows(rwkv_w2, dl), padrows(rwkv_a2, da), padrows(rwkv_g2, dg)
    w_out_b = w_out.astype(_BF16)
    w_gate_b = w_gate.astype(_BF16)
    w_up_b = w_up.astype(_BF16)
    w_down_b = w_down.astype(_BF16)
    half = dff // 2

    zeros_b = lambda s: jnp.zeros((bp,) + s.shape[2:], _F32)
    nch_p = lp // CHUNK
    ts = _round_up(ls + 3, 8)
    spad = ts - ls

    def sample_rows(rows, hist=None, c0=0):
        r3 = jnp.pad(rows.reshape(bs, ls, -1), ((0, 0), (spad, 0), (0, 0)))
        if hist is not None:
            k, w = hist.shape[1:]
            r3 = r3.at[:, spad - k:spad, c0:c0 + w].set(hist)
        return r3.reshape(bs * ts, -1)

    def unsample(y):
        return y.reshape(bs, ts, -1)[:, spad:].reshape(ms, -1)

    outs_p = [[] for _ in range(7)]
    outs_s = [[] for _ in range(7)]
    for l in range(depth):
        hn = _rmsnorm(h, ln_mix[l], _BF16, lp=lp, pad=pad, mp=mp)

        pa = _mm(hn, w_ssd_b[l], name="in_proj_ssd")
        ssd_w = (ssd_conv_w[l], ssd_conv_b[l], ssd_dt_bias[l], ssd_a_log[l], ssd_d[l], ssd_norm[l])
        ya_p, hp_new = _ssd_mixer(pa[:mp], zeros_b(state_ssd), *ssd_w, nseq=bp, nch=nch_p, T=CHUNK, npad=pad, gw=gw)
        pa_s = sample_rows(pa[mp:], state_ssd_conv[l], gw)
        ya_s, hs_new = _ssd_mixer(pa_s, state_ssd[l], *ssd_w, nseq=bs, nch=1, T=ts, npad=spad, gw=gw)
        y_a = jnp.concatenate([ya_p, unsample(ya_s)], axis=0)
        kc = ssd_conv_w.shape[1] - 1
        outs_p[0].append(hp_new)
        outs_p[1].append(pa[:mp].reshape(bp, lp, -1)[:, lp - kc:, gw:gw + xbc_w])
        outs_s[0].append(hs_new)
        outs_s[1].append(pa_s.reshape(bs, ts, -1)[:, ts - kc:, gw:gw + xbc_w])

        pb = _mm(hn, w_rw_b[l], name="in_proj_rwkv")
        rw_args = (mu_p[l], rwkv_w0[l], w2_p[l], rwkv_a0[l], a2_p[l], g2_p[l], rwkv_k_k[l], rwkv_k_a[l],
                   rwkv_r_k[l], rwkv_ln_w[l], rwkv_ln_b[l])
        rw_kw = dict(gw=gw, dl=dl, da=da, dg=dg)
        rw_heads = state_rwkv.shape[2]
        yb_p, sp_new = _rwkv_mixer(pb[:mp], zeros_b(state_rwkv), *rw_args, nseq=bp, nch=nch_p, T=CHUNK, npad=pad,
                                   J=max(1, LANE // (bp * rw_heads)), TC=CHUNK // 4, **rw_kw)
        pb_s = sample_rows(pb[mp:], rw_pad(state_rwkv_shift[l])[:, None, :], 0)
        yb_s, ss_new = _rwkv_mixer(pb_s, state_rwkv[l], *rw_args, nseq=bs, nch=1, T=ts, npad=spad,
                                   J=max(1, LANE // (bs * rw_heads)), TC=ts, **rw_kw)
        y_b = jnp.concatenate([yb_p, unsample(yb_s)], axis=0)
        outs_p[2].append(sp_new)
        outs_p[3].append(rw_compact(pb[:mp].reshape(bp, lp, -1)[:, -1]))
        outs_s[2].append(ss_new)
        outs_s[3].append(rw_compact(pb_s.reshape(bs, ts, -1)[:, -1]))

        pc = _mm(hn, w_ret_b[l], name="in_proj_ret")
        yc_p, rp_new = _ret_mixer(pc[:mp], zeros_b(state_ret), ret_gn_w[l], ret_gn_b[l], 0,
                                  nseq=bp, nch=nch_p, T=CHUNK, npad=pad, gw=gw)
        yc_s, rs_new = _ret_mixer(sample_rows(pc[mp:]), state_ret[l], ret_gn_w[l], ret_gn_b[l], PAST_LEN,
                                  nseq=bs, nch=1, T=ts, npad=spad, gw=gw)
        y_c = jnp.concatenate([yc_p, unsample(yc_s)], axis=0)
        outs_p[4].append(rp_new)
        outs_s[4].append(rs_new)

        pd = _mm(hn, w_s5_b[l], name="in_proj_s5")
        s5_w = (s5_a_re[l], s5_a_im[l], s5_log_dt[l], s5_b_re[l], s5_b_im[l], s5_c_re[l], s5_c_im[l], s5_d[l],
                s5_glu_w[l], s5_glu_b[l], s5_norm[l])
        yd_p, s5r_p, s5i_p = _s5_mixer(pd[:mp], zeros_b(state_s5_re), zeros_b(state_s5_im), *s5_w,
                                       nseq=bp, nch=nch_p, T=CHUNK, npad=pad, nb=bp, TC=CHUNK // 2)
        yd_s, s5r_s, s5i_s = _s5_mixer(sample_rows(pd[mp:]), state_s5_re[l], state_s5_im[l], *s5_w,
                                       nseq=bs, nch=1, T=ts, npad=spad, nb=min(bs, 32), TC=ts)
        y_d = jnp.concatenate([yd_p, unsample(yd_s)], axis=0)
        outs_p[5].append(s5r_p)
        outs_p[6].append(s5i_p)
        outs_s[5].append(s5r_s)
        outs_s[6].append(s5i_s)

        ycat = jnp.concatenate([y_a, y_b, y_c, y_d], axis=-1)
        h = _mm(ycat, w_out_b[l], res=h, name="out_proj")
        hn = _rmsnorm(h, ln_ffn[l], _BF16)
        ff = _swiglu(hn, w_gate_b[l], w_up_b[l])
        h = _mm(ff, w_down_b[l], res=h, kb=0, tk=half, name="ffn_down0")
        h = _mm(ff, w_down_b[l], res=h, kb=1, tk=half, name="ffn_down1")

    y = _rmsnorm(h, ln_f, _F32)
    y_prompt = y[:mp].reshape(bp, lp, d)[:, pad + N_META:]
    y_sample = y[mp:].reshape(bs, ls, d)
    return ((y_prompt, y_sample) + tuple(jnp.stack(a) for a in outs_p) + tuple(jnp.stack(a) for a in outs_s))
```

```python
import functools
import math

import jax
import jax.numpy as jnp
from jax import lax
from jax.experimental import pallas as pl
from jax.experimental.pallas import tpu as pltpu

N_META = 16
EPS = 1e-6
SSD_GROUPS = 2
RWKV_HEAD_DIM = 64
RWKV_GN_EPS = 64e-5
RET_GN_EPS = 1e-5
ROPE_BASE = 10000.0
S5_GROUP = 16
PAST_LEN = 16384
CHUNK = 128
LANE = 128
V7X_VMEM_CAP = 60 * 1024 * 1024

_F32 = jnp.float32
_BF16 = jnp.bfloat16


def _pick(n, cands):
    for c in cands:
        if n % c == 0:
            return c
    raise ValueError(f"no tile in {cands} divides {n}")


def _round_up(n, m):
    return -(-n // m) * m


def _cparams(sem, *block_bytes):
    need = 2 * sum(block_bytes) + (6 << 20)
    return pltpu.CompilerParams(dimension_semantics=sem, vmem_limit_bytes=int(min(max(need, 16 << 20), V7X_VMEM_CAP)))


def _rmsnorm_kernel(x_ref, g_ref, o_ref, *, tr, lp, pad, mp):
    x = x_ref[...]
    y = x * lax.rsqrt(jnp.mean(x * x, axis=-1, keepdims=True) + EPS) * g_ref[...]
    if pad:
        row0 = pl.program_id(0) * tr
        pos0 = lax.rem(row0, lp)
        rows = lax.broadcasted_iota(jnp.int32, (tr, 1), 0)
        is_pad = jnp.logical_and(row0 < mp, pos0 + rows < pad)
        y = jnp.where(is_pad, 0.0, y)
    o_ref[...] = y.astype(o_ref.dtype)


def _rmsnorm(x, g, out_dtype, lp=0, pad=0, mp=0):
    m, d = x.shape
    tr = _pick(math.gcd(m, lp) if pad else m, (256, 128, 64, 32, 16, 8))
    kern = functools.partial(_rmsnorm_kernel, tr=tr, lp=lp, pad=pad, mp=mp)
    return pl.pallas_call(
        kern, grid=(m // tr,),
        in_specs=[pl.BlockSpec((tr, d), lambda i: (i, 0)), pl.BlockSpec((1, d), lambda i: (0, 0))],
        out_specs=pl.BlockSpec((tr, d), lambda i: (i, 0)),
        out_shape=jax.ShapeDtypeStruct((m, d), out_dtype),
        compiler_params=_cparams(("parallel",), tr * d * 4, tr * d * 4),
        name="rmsnorm",
    )(x, g.reshape(1, d).astype(_F32))


def _final_norm(h, g, *, bp, sp, lp, ms):
    d = h.shape[1]
    assert lp - sp == CHUNK and sp % CHUNK == 0
    nch, nout = lp // CHUNK, sp // CHUNK
    kern = functools.partial(_rmsnorm_kernel, tr=CHUNK, lp=0, pad=0, mp=0)
    g2 = g.reshape(1, d).astype(_F32)
    yp = pl.pallas_call(
        kern, grid=(bp, nout),
        in_specs=[pl.BlockSpec((CHUNK, d), lambda b, j: (b * nch + 1 + j, 0)), pl.BlockSpec((1, d), lambda b, j: (0, 0))],
        out_specs=pl.BlockSpec((CHUNK, d), lambda b, j: (b * nout + j, 0)),
        out_shape=jax.ShapeDtypeStruct((bp * sp, d), _F32),
        compiler_params=_cparams(("parallel", "parallel"), CHUNK * d * 4, CHUNK * d * 4),
        name="final_norm_prompt",
    )(h, g2)
    tr = _pick(math.gcd(ms, bp * lp), (256, 128, 64, 32, 16, 8))
    off = bp * lp // tr
    ys = pl.pallas_call(
        functools.partial(_rmsnorm_kernel, tr=tr, lp=0, pad=0, mp=0), grid=(ms // tr,),
        in_specs=[pl.BlockSpec((tr, d), lambda i: (off + i, 0)), pl.BlockSpec((1, d), lambda i: (0, 0))],
        out_specs=pl.BlockSpec((tr, d), lambda i: (i, 0)),
        out_shape=jax.ShapeDtypeStruct((ms, d), _F32),
        compiler_params=_cparams(("parallel",), tr * d * 4, tr * d * 4),
        name="final_norm_sample",
    )(h, g2)
    return yp, ys


def _mm_kernel(*refs, has_res):
    if has_res:
        x_ref, w_ref, r_ref, o_ref = refs
    else:
        x_ref, w_ref, o_ref = refs
    acc = jnp.dot(x_ref[...], w_ref[...], preferred_element_type=_F32)
    if has_res:
        acc = acc + r_ref[...]
    o_ref[...] = acc.astype(o_ref.dtype)


def _mm(x, w, res=None, out_dtype=_F32, kb=0, tk=None, name="matmul"):
    m = x.shape[0]
    n = w.shape[1]
    tk = tk or w.shape[0]
    tm = _pick(m, (1024, 512, 256, 128))
    if tk > 4096:
        tm = _pick(m, (512, 256, 128))
    tn = _pick(n, (512, 256, 128))
    in_specs = [pl.BlockSpec((tm, tk), lambda i, j: (i, kb)), pl.BlockSpec((tk, tn), lambda i, j: (kb, j))]
    args = [x, w]
    blocks = [tm * tk * 2, tk * tn * 2, tm * tn * 4]
    if res is not None:
        in_specs.append(pl.BlockSpec((tm, tn), lambda i, j: (i, j)))
        args.append(res)
        blocks.append(tm * tn * 4)
    return pl.pallas_call(
        functools.partial(_mm_kernel, has_res=res is not None), grid=(m // tm, n // tn),
        in_specs=in_specs, out_specs=pl.BlockSpec((tm, tn), lambda i, j: (i, j)),
        out_shape=jax.ShapeDtypeStruct((m, n), out_dtype),
        compiler_params=_cparams(("parallel", "arbitrary"), *blocks),
        name=name,
    )(*args)


def _outproj_kernel(*refs, nprompt, nmix, gw):
    xp, xs = refs[:nmix], refs[nmix:2 * nmix]
    w_ref, r_ref, o_ref = refs[2 * nmix:]
    i = pl.program_id(0)

    def run(xrefs):
        acc = r_ref[...]
        for j, x in enumerate(xrefs):
            acc = acc + jnp.dot(x[...], w_ref[j * gw:(j + 1) * gw, :], preferred_element_type=_F32)
        o_ref[...] = acc

    @pl.when(i < nprompt)
    def _prompt():
        run(xp)

    @pl.when(i >= nprompt)
    def _sample():
        run(xs)


def _outproj(ys_p, ys_s, w, res):
    mp, gw = ys_p[0].shape
    ms = ys_s[0].shape[0]
    nmix = len(ys_p)
    d, n = w.shape
    tm = _pick(math.gcd(mp, ms), (512, 256, 128))
    tn = _pick(n, (512, 256, 128))
    nprompt = mp // tm
    pspec = pl.BlockSpec((tm, gw), lambda i, j: (jnp.minimum(i, nprompt - 1), 0))
    sspec = pl.BlockSpec((tm, gw), lambda i, j: (jnp.maximum(i - nprompt, 0), 0))
    return pl.pallas_call(
        functools.partial(_outproj_kernel, nprompt=nprompt, nmix=nmix, gw=gw), grid=((mp + ms) // tm, n // tn),
        in_specs=[pspec] * nmix + [sspec] * nmix + [pl.BlockSpec((d, tn), lambda i, j: (0, j)),
                                                    pl.BlockSpec((tm, tn), lambda i, j: (i, j))],
        out_specs=pl.BlockSpec((tm, tn), lambda i, j: (i, j)),
        out_shape=jax.ShapeDtypeStruct((mp + ms, n), _F32),
        compiler_params=_cparams(("parallel", "arbitrary"), 2 * nmix * tm * gw * 2, d * tn * 2, 2 * tm * tn * 4),
        name="out_proj",
    )(*ys_p, *ys_s, w, res)


def _swiglu_kernel(x_ref, wg_ref, wu_ref, o_ref):
    x = x_ref[...]
    g = jnp.dot(x, wg_ref[...], preferred_element_type=_F32)
    u = jnp.dot(x, wu_ref[...], preferred_element_type=_F32)
    o_ref[...] = (g * jax.nn.sigmoid(g) * u).astype(o_ref.dtype)


def _swiglu(x, wg, wu):
    m, k = x.shape
    n = wg.shape[1]
    tm = _pick(m, (1024, 512, 256, 128))
    tn = _pick(n, (256, 128))
    return pl.pallas_call(
        _swiglu_kernel, grid=(m // tm, n // tn),
        in_specs=[pl.BlockSpec((tm, k), lambda i, j: (i, 0)), pl.BlockSpec((k, tn), lambda i, j: (0, j)),
                  pl.BlockSpec((k, tn), lambda i, j: (0, j))],
        out_specs=pl.BlockSpec((tm, tn), lambda i, j: (i, j)),
        out_shape=jax.ShapeDtypeStruct((m, n), _BF16),
        compiler_params=_cparams(("parallel", "arbitrary"), tm * k * 2, 2 * k * tn * 2, tm * tn * 2),
        name="swiglu",
    )(x, wg, wu)


_NT = (((1,), (1,)), ((), ()))
_TN = (((0,), (0,)), ((), ()))


def _silu(x):
    return x * jax.nn.sigmoid(x)


def _softplus(x):
    return jnp.maximum(x, 0.0) + jnp.log(1.0 + jnp.exp(-jnp.abs(x)))


def _split3(x):
    hi = x.astype(_BF16)
    r = x - hi.astype(_F32)
    mid = r.astype(_BF16)
    lo = (r - mid.astype(_F32)).astype(_BF16)
    return hi, mid, lo


def _dot_sel(sel, x, dims):
    parts = _split3(x) if _BF16 == jnp.bfloat16 else (x,)
    out = None
    for p in parts:
        t = lax.dot_general(sel.astype(p.dtype), p, dims, preferred_element_type=_F32)
        out = t if out is None else out + t
    return out


def _dot_sel_r(x, sel, dims):
    parts = _split3(x) if _BF16 == jnp.bfloat16 else (x,)
    out = None
    for p in parts:
        t = lax.dot_general(p, sel.astype(p.dtype), dims, preferred_element_type=_F32)
        out = t if out is None else out + t
    return out


def _full(shape):
    nd = len(shape)
    return pl.BlockSpec(shape, lambda *_: (0,) * nd)


def _ssd_kernel(p_ref, h0_ref, cw_ref, cb_ref, dtb_ref, alog_ref, dsk_ref, nw_ref, y_ref, ho_ref, hst, xbuf, *,
                T, npad, nch, gw, xbc_w, heads, P, N, K):
    c = pl.program_id(1)
    G = SSD_GROUPS
    hpg = heads // G

    @pl.when(c == 0)
    def _init():
        hst[...] = h0_ref[0]
        xbuf[0:8, :] = jnp.zeros((8, xbc_w), _F32)

    xbuf[8:8 + T, :] = p_ref[:, gw:gw + xbc_w]
    conv = cb_ref[...]
    for j in range(K):
        conv = conv + cw_ref[j:j + 1, :] * xbuf[8 - (K - 1) + j:8 - (K - 1) + j + T, :]
    hist = xbuf[8 + T - (K - 1):8 + T, :]
    xbuf[8 - (K - 1):8, :] = hist
    xbc = _silu(conv)
    xs = xbc[:, :gw]
    bm = xbc[:, gw:gw + G * N]
    cm = xbc[:, gw + G * N:gw + 2 * G * N]
    z = p_ref[:, 0:gw]

    dt = _softplus(p_ref[:, gw + xbc_w:gw + xbc_w + LANE] + dtb_ref[...])
    if npad:
        rows = lax.broadcasted_iota(jnp.int32, (T, 1), 0)
        dt = jnp.where(jnp.logical_and(c == 0, rows < npad), 0.0, dt)
    la = dt * (-jnp.exp(alog_ref[...]))
    ri = lax.broadcasted_iota(jnp.int32, (T, T), 0)
    ci = lax.broadcasted_iota(jnp.int32, (T, T), 1)
    causal = ri >= ci
    cum = _dot_sel(causal.astype(_F32), la, (((1,), (0,)), ((), ())))
    hq = lax.broadcasted_iota(jnp.int32, (heads * T, LANE), 0) // T
    ln = lax.broadcasted_iota(jnp.int32, (heads * T, LANE), 1)
    rowb = _dot_sel((ln == hq).astype(_F32), cum, _NT)
    ecum = jnp.exp(cum)
    clast = cum[T - 1:T, :]
    cdec = jnp.exp(clast)
    dte = jnp.exp(clast - cum) * dt

    ys = []
    for g in range(G):
        bm_g = bm[:, g * N:(g + 1) * N].astype(_BF16)
        cm_g = cm[:, g * N:(g + 1) * N].astype(_BF16)
        cb = lax.dot_general(cm_g, bm_g, _NT, preferred_element_type=_F32)
        for hh in range(hpg):
            h = g * hpg + hh
            seg = cum[:, h:h + 1] - rowb[h * T:(h + 1) * T, :]
            lm = jnp.exp(jnp.where(causal, seg, -jnp.inf))
            xh = xs[:, h * P:(h + 1) * P]
            y_diag = jnp.dot((cb * lm).astype(_BF16), (xh * dt[:, h:h + 1]).astype(_BF16),
                             preferred_element_type=_F32)
            hprev = hst[h]
            y_off = lax.dot_general(cm_g, hprev.astype(_BF16), _NT, preferred_element_type=_F32) * ecum[:, h:h + 1]
            xw = (xh * dte[:, h:h + 1]).astype(_BF16)
            hst[h] = hprev * cdec[:, h:h + 1] + lax.dot_general(xw, bm_g, _TN, preferred_element_type=_F32)
            ys.append(y_diag + y_off + xh * dsk_ref[:, h:h + 1])
    y = jnp.concatenate(ys, axis=-1) * _silu(z)
    y = y * lax.rsqrt(jnp.mean(y * y, axis=-1, keepdims=True) + EPS) * nw_ref[...]
    y_ref[...] = y.astype(y_ref.dtype)

    @pl.when(c == nch - 1)
    def _fin():
        ho_ref[0] = hst[...]


def _ssd_mixer(p, h0, conv_w, conv_b, dt_bias, a_log, d_skip, norm_w, *, nseq, nch, T, npad, gw):
    heads, P, N = h0.shape[1:]
    K, xbc_w = conv_w.shape
    wp = p.shape[1]
    padl = lambda v: jnp.pad(v.astype(_F32), (0, LANE - v.shape[0])).reshape(1, LANE)
    kern = functools.partial(_ssd_kernel, T=T, npad=npad, nch=nch, gw=gw, xbc_w=xbc_w, heads=heads, P=P, N=N, K=K)
    return pl.pallas_call(
        kern, grid=(nseq, nch),
        in_specs=[pl.BlockSpec((T, wp), lambda b, c: (b * nch + c, 0)),
                  pl.BlockSpec((1, heads, P, N), lambda b, c: (b, 0, 0, 0)),
                  _full((K, xbc_w)), _full((1, xbc_w)), _full((1, LANE)), _full((1, LANE)), _full((1, LANE)),
                  _full((1, gw))],
        out_specs=[pl.BlockSpec((T, gw), lambda b, c: (b * nch + c, 0)),
                   pl.BlockSpec((1, heads, P, N), lambda b, c: (b, 0, 0, 0))],
        out_shape=[jax.ShapeDtypeStruct((nseq * nch * T, gw), _BF16),
                   jax.ShapeDtypeStruct((nseq, heads, P, N), _F32)],
        scratch_shapes=[pltpu.VMEM((heads, P, N), _F32), pltpu.VMEM((T + 8, xbc_w), _F32)],
        compiler_params=_cparams(("parallel", "arbitrary"), T * wp * 4, 3 * heads * P * N * 4, T * gw * 2,
                                 (T + 8) * xbc_w * 4, 24 * T * max(T, LANE) * 4),
        name="ssd_mixer",
    )(p, h0, conv_w.astype(_F32), conv_b.reshape(1, xbc_w).astype(_F32), padl(dt_bias), padl(a_log), padl(d_skip),
      norm_w.reshape(1, gw).astype(_F32))


def _ret_kernel(p_ref, cc_ref, ss_ref, r0_ref, gw_ref, gb_ref, y_ref, ro_ref, rst, *, T, npad, nch, gw, heads, dk, dv):
    c = pl.program_id(1)
    qkw = heads * dk

    @pl.when(c == 0)
    def _init():
        rst[...] = r0_ref[0]

    npc = jnp.where(c == 0, npad, 0).astype(_F32)
    ri = lax.broadcasted_iota(jnp.int32, (T, T), 0)
    ci = lax.broadcasted_iota(jnp.int32, (T, T), 1)
    causal = ri >= ci
    dlt = (ri - ci).astype(_F32)
    idx = lax.broadcasted_iota(jnp.int32, (T, 1), 0).astype(_F32)
    cc = cc_ref[...]
    ss = ss_ref[...]
    ys = []
    for h in range(heads):
        lg = math.log(1.0 - 2.0 ** (-5.0 - h))
        qh = p_ref[:, h * dk:(h + 1) * dk]
        kh = p_ref[:, qkw + h * dk:qkw + (h + 1) * dk]
        vh = p_ref[:, 2 * qkw + h * dv:2 * qkw + (h + 1) * dv].astype(_BF16)
        qh = (qh * cc + pltpu.roll(qh, dk // 2, 1) * ss)
        kh = (kh * cc + pltpu.roll(kh, dk // 2, 1) * ss) * (dk ** -0.5)
        qb = qh.astype(_BF16)
        dmat = jnp.exp(jnp.where(causal, dlt * lg, -jnp.inf))
        inner = lax.dot_general(qb, kh.astype(_BF16), _NT, preferred_element_type=_F32) * dmat
        y_in = jnp.dot(inner.astype(_BF16), vh, preferred_element_type=_F32)
        rprev = rst[h]
        y_x = jnp.dot(qb, rprev.astype(_BF16), preferred_element_type=_F32) * jnp.exp((idx + 1.0 - npc) * lg)
        kw = (kh * jnp.exp((T - 1.0 - idx) * lg)).astype(_BF16)
        rst[h] = rprev * jnp.exp((T - npc) * lg) + lax.dot_general(kw, vh, _TN, preferred_element_type=_F32)
        o = y_in + y_x
        oc = o - jnp.mean(o, axis=-1, keepdims=True)
        ys.append(oc * lax.rsqrt(jnp.mean(oc * oc, axis=-1, keepdims=True) + RET_GN_EPS))
    gate = p_ref[:, 2 * qkw + gw:2 * qkw + 2 * gw]
    y = _silu(gate) * (jnp.concatenate(ys, axis=-1) * gw_ref[...] + gb_ref[...])
    y_ref[...] = y.astype(y_ref.dtype)

    @pl.when(c == nch - 1)
    def _fin():
        ro_ref[0] = rst[...]


def _ret_mixer(p, r0, gn_w, gn_b, pos0, *, nseq, nch, T, npad, gw):
    heads, dk, dv = r0.shape[1:]
    wp = p.shape[1]
    theta = 1.0 / (ROPE_BASE ** jnp.linspace(0.0, 1.0, dk // 2, dtype=_F32))
    ang = (pos0 + jnp.arange(nch * T) - npad).astype(_F32)[:, None] * theta
    cos, sin = jnp.cos(ang), jnp.sin(ang)
    cc = jnp.concatenate([cos, cos], axis=-1)
    ss = jnp.concatenate([-sin, sin], axis=-1)
    kern = functools.partial(_ret_kernel, T=T, npad=npad, nch=nch, gw=gw, heads=heads, dk=dk, dv=dv)
    return pl.pallas_call(
        kern, grid=(nseq, nch),
        in_specs=[pl.BlockSpec((T, wp), lambda b, c: (b * nch + c, 0)),
                  pl.BlockSpec((T, dk), lambda b, c: (c, 0)), pl.BlockSpec((T, dk), lambda b, c: (c, 0)),
                  pl.BlockSpec((1, heads, dk, dv), lambda b, c: (b, 0, 0, 0)),
                  _full((1, gw)), _full((1, gw))],
        out_specs=[pl.BlockSpec((T, gw), lambda b, c: (b * nch + c, 0)),
                   pl.BlockSpec((1, heads, dk, dv), lambda b, c: (b, 0, 0, 0))],
        out_shape=[jax.ShapeDtypeStruct((nseq * nch * T, gw), _BF16),
                   jax.ShapeDtypeStruct((nseq, heads, dk, dv), _F32)],
        scratch_shapes=[pltpu.VMEM((heads, dk, dv), _F32)],
        compiler_params=_cparams(("parallel", "arbitrary"), T * wp * 4, 3 * heads * dk * dv * 4, T * gw * 2,
                                 16 * T * max(T, dv) * 4),
        name="ret_mixer",
    )(p, cc, ss, r0, gn_w.reshape(1, gw).astype(_F32), gn_b.reshape(1, gw).astype(_F32))


S5_CB = 128


def _gelu_tanh(x):
    return 0.5 * x * (1.0 + jnp.tanh(math.sqrt(2.0 / math.pi) * (x + 0.044715 * (x * x * x))))


def _s5_kernel(*refs, TC, nb, npad, nch, gw, sb, perm):
    nu = nb if perm else 1
    u_refs = refs[:nu]
    (s0r_ref, s0i_ref, are_ref, aim_ref, ldt_ref, wbr_ref, wbi_ref, wcr_ref, wci_ref, d_ref, gluw_ref, glub_ref,
     nw_ref, y_ref, sor_ref, soi_ref, xr, xi, str_, sti) = refs[nu:]
    c = pl.program_id(1)
    nblk = gw // S5_CB
    R = TC * nb

    @pl.when(c == 0)
    def _init():
        str_[...] = s0r_ref[...]
        sti[...] = s0i_ref[...]

    dt = jnp.exp(ldt_ref[...])
    lr, li = are_ref[...], aim_ref[...]
    mag = jnp.exp(lr * dt)
    abr, abi = mag * jnp.cos(li * dt), mag * jnp.sin(li * dt)
    den = lr * lr + li * li
    nr, ni = abr - 1.0, abi
    er, ei = (nr * lr + ni * li) / den, (ni * lr - nr * li) / den

    if perm:
        ti = lax.broadcasted_iota(jnp.int32, (R, R), 0)
        bi_ = lax.broadcasted_iota(jnp.int32, (R, R), 1)
        pmat = jnp.logical_and(ti // nb == bi_ % TC, ti % nb == bi_ // TC).astype(_F32)
        u = _dot_sel(pmat, jnp.concatenate([r[...] for r in u_refs], axis=0), (((1,), (0,)), ((), ())))
    else:
        u = u_refs[0][...]
    ub = u.astype(_BF16)
    for g in range(nblk):
        us = ub[:, g * S5_CB:(g + 1) * S5_CB]
        br = jnp.dot(us, wbr_ref[g], preferred_element_type=_F32)
        bi = jnp.dot(us, wbi_ref[g], preferred_element_type=_F32)
        e_r, e_i = er[:, g * sb:(g + 1) * sb], ei[:, g * sb:(g + 1) * sb]
        xr[:, g * sb:(g + 1) * sb] = e_r * br - e_i * bi
        xi[:, g * sb:(g + 1) * sb] = e_r * bi + e_i * br

    rows = max(nb, 8)
    spi = rows // nb

    def body(i, carry):
        r0 = pl.multiple_of(i * rows, 8)
        b_r, b_i = xr[pl.ds(r0, rows), :], xi[pl.ds(r0, rows), :]
        s_r, s_i = str_[...], sti[...]
        outs_r, outs_i = [], []
        for j in range(spi):
            n_r = abr * s_r - abi * s_i + b_r[j * nb:(j + 1) * nb]
            n_i = abr * s_i + abi * s_r + b_i[j * nb:(j + 1) * nb]
            s_r, s_i = n_r, n_i
            outs_r.append(n_r)
            outs_i.append(n_i)
        xr[pl.ds(r0, rows), :] = outs_r[0] if spi == 1 else jnp.concatenate(outs_r, axis=0)
        xi[pl.ds(r0, rows), :] = outs_i[0] if spi == 1 else jnp.concatenate(outs_i, axis=0)
        str_[...] = s_r
        sti[...] = s_i
        return carry

    start = jnp.clip(npad - c * TC, 0, TC) // spi
    lax.fori_loop(start, TC // spi, body, 0)

    ys = []
    for g in range(nblk):
        xrb = xr[:, g * sb:(g + 1) * sb].astype(_BF16)
        xib = xi[:, g * sb:(g + 1) * sb].astype(_BF16)
        ys.append(jnp.dot(xrb, wcr_ref[g], preferred_element_type=_F32)
                  - jnp.dot(xib, wci_ref[g], preferred_element_type=_F32))
    y = jnp.concatenate(ys, axis=-1) + d_ref[...] * u
    gy = _gelu_tanh(y)
    out = gy * jax.nn.sigmoid(jnp.dot(gy.astype(_BF16), gluw_ref[...], preferred_element_type=_F32) + glub_ref[...])
    out = (out * lax.rsqrt(jnp.mean(out * out, axis=-1, keepdims=True) + EPS) * nw_ref[...]).astype(y_ref.dtype)
    if perm:
        out = lax.dot_general(pmat.astype(out.dtype), out, _TN, preferred_element_type=_F32).astype(y_ref.dtype)
        for b in range(nb):
            y_ref[b] = out[b * TC:(b + 1) * TC]
    else:
        y_ref[...] = out

    @pl.when(c == nch - 1)
    def _fin():
        sor_ref[...] = str_[...]
        soi_ref[...] = sti[...]


def _s5_mixer(u, s0_re, s0_im, a_re, a_im, log_dt, b_re, b_im, c_re, c_im, d, glu_w, glu_b, norm_w, *,
              nseq, nch, T, npad, nb, TC, perm):
    groups, ns = a_re.shape
    gw = groups * S5_GROUP
    gpb = S5_CB // S5_GROUP
    nblk = gw // S5_CB
    sb = gpb * ns
    nst = groups * ns
    nsb = nseq // nb
    ncc = nch * T // TC
    assert npad % max(1, 8 // nb) == 0
    R = TC * nb
    if perm:
        assert nsb == 1
        u_args = [u] * nb
        u_specs = [pl.BlockSpec((TC, gw), functools.partial(lambda s, c, b: (b * ncc + c, 0), b=b)) for b in range(nb)]
        y_spec = pl.BlockSpec((nb, TC, gw), lambda s, c: (0, c, 0))
        y_shape = jax.ShapeDtypeStruct((nb, ncc * TC, gw), _BF16)
    else:
        u_args = [u[:, :gw].reshape(nsb, nb, ncc, TC, gw).transpose(0, 2, 3, 1, 4).reshape(nseq * nch * T, gw)]
        u_specs = [pl.BlockSpec((R, gw), lambda s, c: (s * ncc + c, 0))]
        y_spec = pl.BlockSpec((R, gw), lambda s, c: (s * ncc + c, 0))
        y_shape = jax.ShapeDtypeStruct((nseq * nch * T, gw), _BF16)
    eye = jnp.eye(gpb, dtype=_F32)

    def bd_in(w):
        w4 = w.reshape(nblk, gpb, ns, S5_GROUP)
        return jnp.einsum('bgnc,gh->bgchn', w4, eye).reshape(nblk, S5_CB, sb).astype(_BF16)

    def bd_out(w):
        w4 = w.reshape(nblk, gpb, S5_GROUP, ns)
        return jnp.einsum('bgcn,gh->bgnhc', w4, eye).reshape(nblk, sb, S5_CB).astype(_BF16)

    row = lambda v: v.reshape(1, -1).astype(_F32)
    kern = functools.partial(_s5_kernel, TC=TC, nb=nb, npad=npad, nch=ncc, gw=gw, sb=sb, perm=perm)
    y, so_r, so_i = pl.pallas_call(
        kern, grid=(nsb, ncc),
        in_specs=u_specs + [
            pl.BlockSpec((nb, nst), lambda s, c: (s, 0)), pl.BlockSpec((nb, nst), lambda s, c: (s, 0)),
            _full((1, nst)), _full((1, nst)), _full((1, nst)),
            _full((nblk, S5_CB, sb)), _full((nblk, S5_CB, sb)), _full((nblk, sb, S5_CB)),
            _full((nblk, sb, S5_CB)), _full((1, gw)), _full((gw, gw)), _full((1, gw)), _full((1, gw))],
        out_specs=[y_spec, pl.BlockSpec((nb, nst), lambda s, c: (s, 0)), pl.BlockSpec((nb, nst), lambda s, c: (s, 0))],
        out_shape=[y_shape, jax.ShapeDtypeStruct((nseq, nst), _F32), jax.ShapeDtypeStruct((nseq, nst), _F32)],
        scratch_shapes=[pltpu.VMEM((R, nst), _F32), pltpu.VMEM((R, nst), _F32),
                        pltpu.VMEM((nb, nst), _F32), pltpu.VMEM((nb, nst), _F32)],
        compiler_params=_cparams(("parallel", "arbitrary"), 2 * R * gw * 4, R * gw * 2, 4 * nblk * S5_CB * sb * 2,
                                 gw * gw * 2, R * nst * 4, 6 * max(nb, 8) * nst * 4),
        name="s5_mixer",
    )(*u_args, s0_re.reshape(nseq, nst), s0_im.reshape(nseq, nst), row(a_re), row(a_im),
      row(jnp.broadcast_to(log_dt[:, None], (groups, ns))), bd_in(b_re), bd_in(b_im), bd_out(c_re), bd_out(c_im),
      row(d), glu_w.astype(_BF16), row(glu_b), row(norm_w))
    if perm:
        y = y.reshape(nseq * nch * T, gw)
    else:
        y = y.reshape(nsb, ncc, TC, nb, gw).transpose(0, 3, 1, 2, 4).reshape(nseq * nch * T, gw)
    return y, so_r.reshape(nseq, groups, ns), so_i.reshape(nseq, groups, ns)


def _head_sums(x):
    ri = lax.broadcasted_iota(jnp.int32, (LANE, LANE), 0) // RWKV_HEAD_DIM
    ci = lax.broadcasted_iota(jnp.int32, (LANE, LANE), 1) // RWKV_HEAD_DIM
    e = (ri == ci).astype(_F32)
    nn = (((1,), (0,)), ((), ()))
    return jnp.concatenate([_dot_sel_r(x[:, j:j + LANE], e, nn) for j in range(0, x.shape[1], LANE)], axis=-1)


def _rwkv_pre_kernel(p_ref, mu_ref, w0_ref, w2_ref, a0_ref, a2_ref, g2_ref, kk_ref, ka_ref, rk_ref,
                     r_o, k_o, v_o, w_o, kk_o, b_o, g_o, bon_o, xbuf, *, T, gw, dl, da, dg):
    c = pl.program_id(1)

    @pl.when(c == 0)
    def _init():
        xbuf[0:8, :] = jnp.zeros((8, xbuf.shape[1]), _F32)

    p = p_ref[...]
    xbuf[8:8 + T, :] = p
    prev = xbuf[7:7 + T, :]
    last = xbuf[7 + T:8 + T, :]
    xbuf[7:8, :] = last
    pm = p + (prev - p) * mu_ref[...]
    r, k, v = pm[:, :gw], pm[:, gw:2 * gw], pm[:, 2 * gw:3 * gw]
    o1 = 3 * gw
    wl, al, gl = pm[:, o1:o1 + dl], pm[:, o1 + dl:o1 + dl + da], pm[:, o1 + dl + da:o1 + dl + da + dg]
    wx = w0_ref[...] + jnp.dot(jnp.tanh(wl).astype(_BF16), w2_ref[...], preferred_element_type=_F32)
    decay = jnp.exp(-jnp.exp(-_softplus(-wx) - 0.5))
    a = jax.nn.sigmoid(a0_ref[...] + jnp.dot(al.astype(_BF16), a2_ref[...], preferred_element_type=_F32))
    g = jnp.dot(jax.nn.sigmoid(gl).astype(_BF16), g2_ref[...], preferred_element_type=_F32)
    kkf = k * kk_ref[...]
    kk = kkf / jnp.maximum(jnp.sqrt(_head_sums(kkf * kkf)), 1e-12)
    k2 = k * (1.0 + (a - 1.0) * ka_ref[...])
    r_o[...] = r
    k_o[...] = k2
    v_o[...] = v
    w_o[...] = decay
    kk_o[...] = kk
    b_o[...] = kk * a
    g_o[...] = g
    bon_o[...] = _head_sums(r * k2 * rk_ref[...]) * v


def _rwkv_scan_kernel(r_ref, k_ref, v_ref, w_ref, kk_ref, b_ref, s0_ref, o_ref, so_ref, st, *, TC, K, npad, nch):
    c = pl.program_id(1)

    @pl.when(c == 0)
    def _init():
        st[...] = s0_ref[...]

    start = jnp.clip(npad - c * TC, 0, TC)

    @pl.when(start > 0)
    def _zero():
        o_ref[...] = jnp.zeros(o_ref.shape, _F32)

    vecs = (kk_ref, w_ref, b_ref, k_ref, r_ref)
    NACC = 4

    def body(t, carry):
        row = lambda i, k: vecs[i][t, pl.ds(k, 1), :]
        acc = [st[k] * row(0, k) for k in range(NACC)]
        for k in range(NACC, K):
            acc[k % NACC] = acc[k % NACC] + st[k] * row(0, k)
        sa = (acc[0] + acc[1]) + (acc[2] + acc[3])
        v_t = v_ref[t]
        acc = [None] * NACC
        for k in range(K):
            s = st[k] * row(1, k) - sa * row(2, k) + v_t * row(3, k)
            st[k] = s
            acc[k % NACC] = s * row(4, k) if acc[k % NACC] is None else acc[k % NACC] + s * row(4, k)
        o_ref[t] = (acc[0] + acc[1]) + (acc[2] + acc[3])
        return carry

    lax.fori_loop(start, TC, body, 0)

    @pl.when(c == nch - 1)
    def _fin():
        so_ref[...] = st[...]


def _rwkv_post_kernel(o_ref, bon_ref, g_ref, lw_ref, lb_ref, y_ref):
    o = o_ref[...]
    oc = o - _head_sums(o) * (1.0 / RWKV_HEAD_DIM)
    var = _head_sums(oc * oc) * (1.0 / RWKV_HEAD_DIM)
    y = (oc * lax.rsqrt(var + RWKV_GN_EPS) * lw_ref[...] + lb_ref[...] + bon_ref[...]) * g_ref[...]
    y_ref[...] = y.astype(y_ref.dtype)


def _rwkv_mixer(p, s0, mu_p, w0, w2_p, a0, a2_p, g2_p, k_k, k_a, r_k, ln_w, ln_b, *, nseq, nch, T, npad, gw,
                dl, da, dg, J, TC):
    H, V, K = s0.shape[1:]
    wp = p.shape[1]
    rows = nseq * nch * T
    L = nch * T
    row = lambda v: v.reshape(1, -1).astype(_F32)
    f32rows = jax.ShapeDtypeStruct((rows, gw), _F32)
    blk = pl.BlockSpec((T, gw), lambda b, c: (b * nch + c, 0))
    pre = pl.pallas_call(
        functools.partial(_rwkv_pre_kernel, T=T, gw=gw, dl=dl, da=da, dg=dg), grid=(nseq, nch),
        in_specs=[pl.BlockSpec((T, wp), lambda b, c: (b * nch + c, 0)), _full((1, wp)), _full((1, gw)),
                  _full((dl, gw)), _full((1, gw)), _full((da, gw)), _full((dg, gw)), _full((1, gw)), _full((1, gw)),
                  _full((1, gw))],
        out_specs=[blk] * 8, out_shape=[f32rows] * 8,
        scratch_shapes=[pltpu.VMEM((T + 8, wp), _F32)],
        compiler_params=_cparams(("parallel", "arbitrary"), 2 * T * wp * 4, 8 * T * gw * 4, 12 * T * gw * 4),
        name="rwkv_pre",
    )(p, row(mu_p), row(w0), w2_p.astype(_BF16), row(a0), a2_p.astype(_BF16), g2_p.astype(_BF16), row(k_k),
      row(k_a), row(r_k))
    r, k2, v, decay, kk, bvec, g, bonus = pre

    VI = V // J
    NL = J * nseq * H
    assert NL % LANE == 0 or NL < LANE

    def kvec(x):
        y = x.reshape(nseq, L, H, K).transpose(1, 3, 0, 2).reshape(L, K, 1, nseq * H)
        return jnp.broadcast_to(y, (L, K, J, nseq * H)).reshape(L, K, NL)

    vv = v.reshape(nseq, L, H, J, VI).transpose(1, 4, 3, 0, 2).reshape(L, VI, NL)
    st0 = s0.reshape(nseq, H, J, VI, K).transpose(4, 3, 2, 0, 1).reshape(K, VI, NL)
    lb = min(NL, LANE)
    ncc = L // TC
    kspec = pl.BlockSpec((TC, K, lb), lambda n, c: (c, 0, n))
    vspec = pl.BlockSpec((TC, VI, lb), lambda n, c: (c, 0, n))
    sspec = pl.BlockSpec((K, VI, lb), lambda n, c: (0, 0, n))
    o, st1 = pl.pallas_call(
        functools.partial(_rwkv_scan_kernel, TC=TC, K=K, npad=npad, nch=ncc), grid=(NL // lb, ncc),
        in_specs=[kspec, kspec, vspec, kspec, kspec, kspec, sspec],
        out_specs=[vspec, sspec],
        out_shape=[jax.ShapeDtypeStruct((L, VI, NL), _F32), jax.ShapeDtypeStruct((K, VI, NL), _F32)],
        scratch_shapes=[pltpu.VMEM((K, VI, lb), _F32)],
        compiler_params=_cparams(("parallel", "arbitrary"), 5 * TC * K * lb * 4, 2 * TC * VI * lb * 4,
                                 3 * VI * K * lb * 4),
        name="rwkv_scan",
    )(kvec(r), kvec(k2), vv, kvec(decay), kvec(kk), kvec(bvec), st0)
    o = o.reshape(L, VI, J, nseq, H).transpose(3, 0, 4, 2, 1).reshape(rows, gw)
    s_new = st1.reshape(K, VI, J, nseq, H).transpose(3, 4, 2, 1, 0).reshape(nseq, H, V, K)

    tr = _pick(rows, (256, 128, 64, 32, 16, 8))
    rblk = pl.BlockSpec((tr, gw), lambda i: (i, 0))
    y = pl.pallas_call(
        _rwkv_post_kernel, grid=(rows // tr,),
        in_specs=[rblk, rblk, rblk, _full((1, gw)), _full((1, gw))],
        out_specs=rblk, out_shape=jax.ShapeDtypeStruct((rows, gw), _BF16),
        compiler_params=_cparams(("parallel",), 4 * tr * gw * 4, 8 * tr * gw * 4),
        name="rwkv_post",
    )(o, bonus, g, row(ln_w), row(ln_b))
    return y, s_new


def kernel(x_prompt, x_sample, state_ssd, state_ssd_conv, state_rwkv, state_rwkv_shift, state_ret, state_s5_re,
           state_s5_im, meta, ln_mix, w_in, ssd_conv_w, ssd_conv_b, ssd_dt_bias, ssd_a_log, ssd_d, ssd_norm, rwkv_mu,
           rwkv_w0, rwkv_w2, rwkv_a0, rwkv_a2, rwkv_g2, rwkv_k_k, rwkv_k_a, rwkv_r_k, rwkv_ln_w, rwkv_ln_b, ret_gn_w,
           ret_gn_b, s5_a_re, s5_a_im, s5_log_dt, s5_b_re, s5_b_im, s5_c_re, s5_c_im, s5_d, s5_glu_w, s5_glu_b,
           s5_norm, w_out, ln_ffn, w_gate, w_up, w_down, ln_f):
    bp, sp, d = x_prompt.shape
    bs, ls, _ = x_sample.shape
    depth = w_in.shape[0]
    gw = d // 4
    ssd_heads, ssd_p, ssd_n = state_ssd.shape[2:]
    xbc_w = state_ssd_conv.shape[-1]
    rwkv_proj = state_rwkv_shift.shape[-1]
    ret_heads, ret_dk, ret_dv = state_ret.shape[2:]
    qkw = ret_heads * ret_dk
    dff = w_gate.shape[-1]

    lreal = N_META + sp
    lp = _round_up(lreal, CHUNK)
    pad = lp - lreal
    mp, ms = bp * lp, bs * ls
    m = mp + ms

    meta_b = jnp.broadcast_to(meta[None], (bp, N_META, d))
    hp = jnp.concatenate([jnp.zeros((bp, pad, d), _F32), meta_b, x_prompt], axis=1).reshape(mp, d)
    h = jnp.concatenate([hp, x_sample.reshape(ms, d)], axis=0)

    in_splits = (gw, xbc_w, ssd_heads, rwkv_proj, qkw, qkw, gw, gw, gw)
    offs = [0]
    for s in in_splits:
        offs.append(offs[-1] + s)
    ntile = 512 if gw % 512 == 0 else LANE

    def pack(l, c0, c1):
        w = w_in[l, :, c0:c1].astype(_BF16)
        return jnp.pad(w, ((0, 0), (0, _round_up(c1 - c0, ntile) - (c1 - c0))))

    layers = range(depth)
    w_ssd_b = [pack(l, offs[0], offs[3]) for l in layers]
    w_ret_b = [pack(l, offs[4], offs[8]) for l in layers]
    w_s5_b = [pack(l, offs[8], offs[9]) for l in layers]

    lora_c = (rwkv_w2.shape[1], rwkv_a2.shape[1], rwkv_g2.shape[1])
    dl, da, dg = (_round_up(n, LANE) for n in lora_c)
    rw_w = _round_up(3 * gw + dl + da + dg, ntile)

    def rw_pad(x):
        parts, o = [x[..., :3 * gw]], 3 * gw
        for n, npd in zip(lora_c, (dl, da, dg)):
            parts.append(jnp.pad(x[..., o:o + n], [(0, 0)] * (x.ndim - 1) + [(0, npd - n)]))
            o += n
        y = jnp.concatenate(parts, axis=-1)
        return jnp.pad(y, [(0, 0)] * (x.ndim - 1) + [(0, rw_w - y.shape[-1])])

    def rw_compact(x):
        parts, o = [x[..., :3 * gw]], 3 * gw
        for n, npd in zip(lora_c, (dl, da, dg)):
            parts.append(x[..., o:o + n])
            o += npd
        return jnp.concatenate(parts, axis=-1)

    w_rw_b = [rw_pad(w_in[l, :, offs[3]:offs[4]].astype(_BF16)) for l in layers]
    mu_p = rw_pad(rwkv_mu)
    padrows = lambda w, n: jnp.pad(w, ((0, 0), (0, n - w.shape[1]), (0, 0)))
    w2_p, a2_p, g2_p = padrows(rwkv_w2, dl), padrows(rwkv_a2, da), padrows(rwkv_g2, dg)
    w_out_b = [w_out[l].astype(_BF16) for l in layers]
    w_gate_b = [w_gate[l].astype(_BF16) for l in layers]
    w_up_b = [w_up[l].astype(_BF16) for l in layers]
    w_down_b = [w_down[l].astype(_BF16) for l in layers]
    half = dff // 2

    zeros_b = lambda s: jnp.zeros((bp,) + s.shape[2:], _F32)
    nch_p = lp // CHUNK
    ts = _round_up(ls + 3, 8)
    spad = ts - ls

    def sample_rows(rows, hist=None, c0=0):
        r3 = jnp.pad(rows.reshape(bs, ls, -1), ((0, 0), (spad, 0), (0, 0)))
        if hist is not None:
            k, w = hist.shape[1:]
            r3 = r3.at[:, spad - k:spad, c0:c0 + w].set(hist)
        return r3.reshape(bs * ts, -1)

    def unsample(y):
        return y.reshape(bs, ts, -1)[:, spad:].reshape(ms, -1)

    def last_rows(rows, k):
        return jnp.stack([rows[(b + 1) * lp - k:(b + 1) * lp] for b in range(bp)])

    outs_p = [[] for _ in range(7)]
    outs_s = [[] for _ in range(7)]
    for l in range(depth):
        hn = _rmsnorm(h, ln_mix[l], _BF16, lp=lp, pad=pad, mp=mp)

        pa = _mm(hn, w_ssd_b[l], name="in_proj_ssd")
        ssd_w = (ssd_conv_w[l], ssd_conv_b[l], ssd_dt_bias[l], ssd_a_log[l], ssd_d[l], ssd_norm[l])
        ya_p, hp_new = _ssd_mixer(pa, zeros_b(state_ssd), *ssd_w, nseq=bp, nch=nch_p, T=CHUNK, npad=pad, gw=gw)
        pa_s = sample_rows(pa[mp:], state_ssd_conv[l], gw)
        ya_s, hs_new = _ssd_mixer(pa_s, state_ssd[l], *ssd_w, nseq=bs, nch=1, T=ts, npad=spad, gw=gw)
        kc = ssd_conv_w.shape[1] - 1
        outs_p[0].append(hp_new)
        outs_p[1].append(last_rows(pa, kc)[..., gw:gw + xbc_w])
        outs_s[0].append(hs_new)
        outs_s[1].append(pa_s.reshape(bs, ts, -1)[:, ts - kc:, gw:gw + xbc_w])

        pb = _mm(hn, w_rw_b[l], name="in_proj_rwkv")
        rw_args = (mu_p[l], rwkv_w0[l], w2_p[l], rwkv_a0[l], a2_p[l], g2_p[l], rwkv_k_k[l], rwkv_k_a[l],
                   rwkv_r_k[l], rwkv_ln_w[l], rwkv_ln_b[l])
        rw_kw = dict(gw=gw, dl=dl, da=da, dg=dg)
        rw_heads = state_rwkv.shape[2]
        yb_p, sp_new = _rwkv_mixer(pb, zeros_b(state_rwkv), *rw_args, nseq=bp, nch=nch_p, T=CHUNK, npad=pad,
                                   J=max(1, LANE // (bp * rw_heads)), TC=CHUNK // 4, **rw_kw)
        pb_s = sample_rows(pb[mp:], rw_pad(state_rwkv_shift[l])[:, None, :], 0)
        yb_s, ss_new = _rwkv_mixer(pb_s, state_rwkv[l], *rw_args, nseq=bs, nch=1, T=ts, npad=spad,
                                   J=max(1, LANE // (bs * rw_heads)), TC=ts, **rw_kw)
        outs_p[2].append(sp_new)
        outs_p[3].append(rw_compact(last_rows(pb, 1)[:, 0]))
        outs_s[2].append(ss_new)
        outs_s[3].append(rw_compact(pb_s.reshape(bs, ts, -1)[:, -1]))

        pc = _mm(hn, w_ret_b[l], name="in_proj_ret")
        yc_p, rp_new = _ret_mixer(pc, zeros_b(state_ret), ret_gn_w[l], ret_gn_b[l], 0,
                                  nseq=bp, nch=nch_p, T=CHUNK, npad=pad, gw=gw)
        yc_s, rs_new = _ret_mixer(sample_rows(pc[mp:]), state_ret[l], ret_gn_w[l], ret_gn_b[l], PAST_LEN,
                                  nseq=bs, nch=1, T=ts, npad=spad, gw=gw)
        outs_p[4].append(rp_new)
        outs_s[4].append(rs_new)

        pd = _mm(hn, w_s5_b[l], name="in_proj_s5")
        s5_w = (s5_a_re[l], s5_a_im[l], s5_log_dt[l], s5_b_re[l], s5_b_im[l], s5_c_re[l], s5_c_im[l], s5_d[l],
                s5_glu_w[l], s5_glu_b[l], s5_norm[l])
        yd_p, s5r_p, s5i_p = _s5_mixer(pd, zeros_b(state_s5_re), zeros_b(state_s5_im), *s5_w, nseq=bp, nch=nch_p,
                                       T=CHUNK, npad=pad, nb=bp, TC=CHUNK // 2, perm=True)
        yd_s, s5r_s, s5i_s = _s5_mixer(sample_rows(pd[mp:]), state_s5_re[l], state_s5_im[l], *s5_w, nseq=bs, nch=1,
                                       T=ts, npad=spad, nb=min(bs, 32), TC=ts, perm=False)
        outs_p[5].append(s5r_p)
        outs_p[6].append(s5i_p)
        outs_s[5].append(s5r_s)
        outs_s[6].append(s5i_s)

        h = _outproj((ya_p, yb_p, yc_p, yd_p), tuple(unsample(y) for y in (ya_s, yb_s, yc_s, yd_s)), w_out_b[l], h)
        hn = _rmsnorm(h, ln_ffn[l], _BF16)
        ff = _swiglu(hn, w_gate_b[l], w_up_b[l])
        h = _mm(ff, w_down_b[l], res=h, kb=0, tk=half, name="ffn_down0")
        h = _mm(ff, w_down_b[l], res=h, kb=1, tk=half, name="ffn_down1")

    y_prompt, y_sample = _final_norm(h, ln_f, bp=bp, sp=sp, lp=lp, ms=ms)
    return ((y_prompt.reshape(bp, sp, d), y_sample.reshape(bs, ls, d))
            + tuple(jnp.stack(a) for a in outs_p) + tuple(jnp.stack(a) for a in outs_s))
```

```python
import functools
import math

import jax
import jax.numpy as jnp
from jax import lax
from jax.experimental import pallas as pl
from jax.experimental.pallas import tpu as pltpu

N_META = 16
EPS = 1e-6
SSD_GROUPS = 2
RWKV_HEAD_DIM = 64
RWKV_GN_EPS = 64e-5
RET_GN_EPS = 1e-5
ROPE_BASE = 10000.0
S5_GROUP = 16
PAST_LEN = 16384
CHUNK = 128
LANE = 128
V7X_VMEM_CAP = 60 * 1024 * 1024

_F32 = jnp.float32
_BF16 = jnp.bfloat16


def _pick(n, cands):
    for c in cands:
        if n % c == 0:
            return c
    raise ValueError(f"no tile in {cands} divides {n}")


def _round_up(n, m):
    return -(-n // m) * m


def _cparams(sem, *block_bytes):
    need = 2 * sum(block_bytes) + (6 << 20)
    return pltpu.CompilerParams(dimension_semantics=sem, vmem_limit_bytes=int(min(max(need, 16 << 20), V7X_VMEM_CAP)))


def _rmsnorm_kernel(x_ref, g_ref, o_ref, *, tr, lp, pad, mp):
    x = x_ref[...]
    y = x * lax.rsqrt(jnp.mean(x * x, axis=-1, keepdims=True) + EPS) * g_ref[...]
    if pad:
        row0 = pl.program_id(0) * tr
        pos0 = lax.rem(row0, lp)
        rows = lax.broadcasted_iota(jnp.int32, (tr, 1), 0)
        is_pad = jnp.logical_and(row0 < mp, pos0 + rows < pad)
        y = jnp.where(is_pad, 0.0, y)
    o_ref[...] = y.astype(o_ref.dtype)


def _rmsnorm(x, g, out_dtype, lp=0, pad=0, mp=0):
    m, d = x.shape
    tr = _pick(math.gcd(m, lp) if pad else m, (256, 128, 64, 32, 16, 8))
    kern = functools.partial(_rmsnorm_kernel, tr=tr, lp=lp, pad=pad, mp=mp)
    return pl.pallas_call(
        kern, grid=(m // tr,),
        in_specs=[pl.BlockSpec((tr, d), lambda i: (i, 0)), pl.BlockSpec((1, d), lambda i: (0, 0))],
        out_specs=pl.BlockSpec((tr, d), lambda i: (i, 0)),
        out_shape=jax.ShapeDtypeStruct((m, d), out_dtype),
        compiler_params=_cparams(("parallel",), tr * d * 4, tr * d * 4),
        name="rmsnorm",
    )(x, g.reshape(1, d).astype(_F32))


def _final_norm(h, g, *, bp, sp, lp, ms):
    d = h.shape[1]
    assert lp - sp == CHUNK and sp % CHUNK == 0
    nch, nout = lp // CHUNK, sp // CHUNK
    kern = functools.partial(_rmsnorm_kernel, tr=CHUNK, lp=0, pad=0, mp=0)
    g2 = g.reshape(1, d).astype(_F32)
    yp = pl.pallas_call(
        kern, grid=(bp, nout),
        in_specs=[pl.BlockSpec((CHUNK, d), lambda b, j: (b * nch + 1 + j, 0)), pl.BlockSpec((1, d), lambda b, j: (0, 0))],
        out_specs=pl.BlockSpec((CHUNK, d), lambda b, j: (b * nout + j, 0)),
        out_shape=jax.ShapeDtypeStruct((bp * sp, d), _F32),
        compiler_params=_cparams(("parallel", "parallel"), CHUNK * d * 4, CHUNK * d * 4),
        name="final_norm_prompt",
    )(h, g2)
    tr = _pick(math.gcd(ms, bp * lp), (256, 128, 64, 32, 16, 8))
    off = bp * lp // tr
    ys = pl.pallas_call(
        functools.partial(_rmsnorm_kernel, tr=tr, lp=0, pad=0, mp=0), grid=(ms // tr,),
        in_specs=[pl.BlockSpec((tr, d), lambda i: (off + i, 0)), pl.BlockSpec((1, d), lambda i: (0, 0))],
        out_specs=pl.BlockSpec((tr, d), lambda i: (i, 0)),
        out_shape=jax.ShapeDtypeStruct((ms, d), _F32),
        compiler_params=_cparams(("parallel",), tr * d * 4, tr * d * 4),
        name="final_norm_sample",
    )(h, g2)
    return yp, ys


def _mm_kernel(*refs, has_res):
    if has_res:
        x_ref, w_ref, r_ref, o_ref = refs
    else:
        x_ref, w_ref, o_ref = refs
    acc = jnp.dot(x_ref[...], w_ref[...], preferred_element_type=_F32)
    if has_res:
        acc = acc + r_ref[...]
    o_ref[...] = acc.astype(o_ref.dtype)


def _mm(x, w, res=None, out_dtype=_F32, kb=0, tk=None, name="matmul"):
    m = x.shape[0]
    n = w.shape[1]
    tk = tk or w.shape[0]
    tm = _pick(m, (1024, 512, 256, 128))
    if tk > 4096:
        tm = _pick(m, (512, 256, 128))
    tn = _pick(n, (512, 256, 128))
    in_specs = [pl.BlockSpec((tm, tk), lambda i, j: (i, kb)), pl.BlockSpec((tk, tn), lambda i, j: (kb, j))]
    args = [x, w]
    blocks = [tm * tk * 2, tk * tn * 2, tm * tn * 4]
    if res is not None:
        in_specs.append(pl.BlockSpec((tm, tn), lambda i, j: (i, j)))
        args.append(res)
        blocks.append(tm * tn * 4)
    return pl.pallas_call(
        functools.partial(_mm_kernel, has_res=res is not None), grid=(m // tm, n // tn),
        in_specs=in_specs, out_specs=pl.BlockSpec((tm, tn), lambda i, j: (i, j)),
        out_shape=jax.ShapeDtypeStruct((m, n), out_dtype),
        compiler_params=_cparams(("parallel", "arbitrary"), *blocks),
        name=name,
    )(*args)


def _outproj_kernel(*refs, nprompt, nmix, gw):
    xp, xs = refs[:nmix], refs[nmix:2 * nmix]
    w_ref, r_ref, o_ref = refs[2 * nmix:]
    i = pl.program_id(0)

    def run(xrefs):
        acc = r_ref[...]
        for j, x in enumerate(xrefs):
            acc = acc + jnp.dot(x[...], w_ref[j * gw:(j + 1) * gw, :], preferred_element_type=_F32)
        o_ref[...] = acc

    @pl.when(i < nprompt)
    def _prompt():
        run(xp)

    @pl.when(i >= nprompt)
    def _sample():
        run(xs)


def _outproj(ys_p, ys_s, w, res):
    mp, gw = ys_p[0].shape
    ms = ys_s[0].shape[0]
    nmix = len(ys_p)
    d, n = w.shape
    tm = _pick(math.gcd(mp, ms), (512, 256, 128))
    tn = _pick(n, (512, 256, 128))
    nprompt = mp // tm
    pspec = pl.BlockSpec((tm, gw), lambda i, j: (jnp.minimum(i, nprompt - 1), 0))
    sspec = pl.BlockSpec((tm, gw), lambda i, j: (jnp.maximum(i - nprompt, 0), 0))
    return pl.pallas_call(
        functools.partial(_outproj_kernel, nprompt=nprompt, nmix=nmix, gw=gw), grid=((mp + ms) // tm, n // tn),
        in_specs=[pspec] * nmix + [sspec] * nmix + [pl.BlockSpec((d, tn), lambda i, j: (0, j)),
                                                    pl.BlockSpec((tm, tn), lambda i, j: (i, j))],
        out_specs=pl.BlockSpec((tm, tn), lambda i, j: (i, j)),
        out_shape=jax.ShapeDtypeStruct((mp + ms, n), _F32),
        compiler_params=_cparams(("parallel", "arbitrary"), 2 * nmix * tm * gw * 2, d * tn * 2, 2 * tm * tn * 4),
        name="out_proj",
    )(*ys_p, *ys_s, w, res)


def _swiglu_kernel(x_ref, wg_ref, wu_ref, o_ref):
    x = x_ref[...]
    g = jnp.dot(x, wg_ref[...], preferred_element_type=_F32)
    u = jnp.dot(x, wu_ref[...], preferred_element_type=_F32)
    o_ref[...] = (g * jax.nn.sigmoid(g) * u).astype(o_ref.dtype)


def _swiglu(x, wg, wu):
    m, k = x.shape
    n = wg.shape[1]
    tm = _pick(m, (1024, 512, 256, 128))
    tn = _pick(n, (256, 128))
    return pl.pallas_call(
        _swiglu_kernel, grid=(m // tm, n // tn),
        in_specs=[pl.BlockSpec((tm, k), lambda i, j: (i, 0)), pl.BlockSpec((k, tn), lambda i, j: (0, j)),
                  pl.BlockSpec((k, tn), lambda i, j: (0, j))],
        out_specs=pl.BlockSpec((tm, tn), lambda i, j: (i, j)),
        out_shape=jax.ShapeDtypeStruct((m, n), _BF16),
        compiler_params=_cparams(("parallel", "arbitrary"), tm * k * 2, 2 * k * tn * 2, tm * tn * 2),
        name="swiglu",
    )(x, wg, wu)


_NT = (((1,), (1,)), ((), ()))
_TN = (((0,), (0,)), ((), ()))


def _silu(x):
    return x * jax.nn.sigmoid(x)


def _softplus(x):
    return jnp.maximum(x, 0.0) + jnp.log(1.0 + jnp.exp(-jnp.abs(x)))


def _split3(x):
    hi = x.astype(_BF16)
    r = x - hi.astype(_F32)
    mid = r.astype(_BF16)
    lo = (r - mid.astype(_F32)).astype(_BF16)
    return hi, mid, lo


def _dot_sel(sel, x, dims):
    parts = _split3(x) if _BF16 == jnp.bfloat16 else (x,)
    out = None
    for p in parts:
        t = lax.dot_general(sel.astype(p.dtype), p, dims, preferred_element_type=_F32)
        out = t if out is None else out + t
    return out


def _dot_sel_r(x, sel, dims):
    parts = _split3(x) if _BF16 == jnp.bfloat16 else (x,)
    out = None
    for p in parts:
        t = lax.dot_general(p, sel.astype(p.dtype), dims, preferred_element_type=_F32)
        out = t if out is None else out + t
    return out


def _full(shape):
    nd = len(shape)
    return pl.BlockSpec(shape, lambda *_: (0,) * nd)


def _ssd_kernel(p_ref, h0_ref, cw_ref, cb_ref, dtb_ref, alog_ref, dsk_ref, nw_ref, y_ref, ho_ref, hst, xbuf, *,
                T, npad, nch, gw, xbc_w, heads, P, N, K):
    c = pl.program_id(1)
    G = SSD_GROUPS
    hpg = heads // G

    @pl.when(c == 0)
    def _init():
        hst[...] = h0_ref[0]
        xbuf[0:8, :] = jnp.zeros((8, xbc_w), _F32)

    xbuf[8:8 + T, :] = p_ref[:, gw:gw + xbc_w]
    conv = cb_ref[...]
    for j in range(K):
        conv = conv + cw_ref[j:j + 1, :] * xbuf[8 - (K - 1) + j:8 - (K - 1) + j + T, :]
    hist = xbuf[8 + T - (K - 1):8 + T, :]
    xbuf[8 - (K - 1):8, :] = hist
    xbc = _silu(conv)
    xs = xbc[:, :gw]
    bm = xbc[:, gw:gw + G * N]
    cm = xbc[:, gw + G * N:gw + 2 * G * N]
    z = p_ref[:, 0:gw]

    dt = _softplus(p_ref[:, gw + xbc_w:gw + xbc_w + LANE] + dtb_ref[...])
    if npad:
        rows = lax.broadcasted_iota(jnp.int32, (T, 1), 0)
        dt = jnp.where(jnp.logical_and(c == 0, rows < npad), 0.0, dt)
    la = dt * (-jnp.exp(alog_ref[...]))
    ri = lax.broadcasted_iota(jnp.int32, (T, T), 0)
    ci = lax.broadcasted_iota(jnp.int32, (T, T), 1)
    causal = ri >= ci
    cum = _dot_sel(causal.astype(_F32), la, (((1,), (0,)), ((), ())))
    hq = lax.broadcasted_iota(jnp.int32, (heads * T, LANE), 0) // T
    ln = lax.broadcasted_iota(jnp.int32, (heads * T, LANE), 1)
    rowb = _dot_sel((ln == hq).astype(_F32), cum, _NT)
    ecum = jnp.exp(cum)
    clast = cum[T - 1:T, :]
    cdec = jnp.exp(clast)
    dte = jnp.exp(clast - cum) * dt

    ys = []
    for g in range(G):
        bm_g = bm[:, g * N:(g + 1) * N].astype(_BF16)
        cm_g = cm[:, g * N:(g + 1) * N].astype(_BF16)
        cb = lax.dot_general(cm_g, bm_g, _NT, preferred_element_type=_F32)
        for hh in range(hpg):
            h = g * hpg + hh
            seg = cum[:, h:h + 1] - rowb[h * T:(h + 1) * T, :]
            lm = jnp.exp(jnp.where(causal, seg, -jnp.inf))
            xh = xs[:, h * P:(h + 1) * P]
            y_diag = jnp.dot((cb * lm).astype(_BF16), (xh * dt[:, h:h + 1]).astype(_BF16),
                             preferred_element_type=_F32)
            hprev = hst[h]
            y_off = lax.dot_general(cm_g, hprev.astype(_BF16), _NT, preferred_element_type=_F32) * ecum[:, h:h + 1]
            xw = (xh * dte[:, h:h + 1]).astype(_BF16)
            hst[h] = hprev * cdec[:, h:h + 1] + lax.dot_general(xw, bm_g, _TN, preferred_element_type=_F32)
            ys.append(y_diag + y_off + xh * dsk_ref[:, h:h + 1])
    y = jnp.concatenate(ys, axis=-1) * _silu(z)
    y = y * lax.rsqrt(jnp.mean(y * y, axis=-1, keepdims=True) + EPS) * nw_ref[...]
    y_ref[...] = y.astype(y_ref.dtype)

    @pl.when(c == nch - 1)
    def _fin():
        ho_ref[0] = hst[...]


def _ssd_mixer(p, h0, conv_w, conv_b, dt_bias, a_log, d_skip, norm_w, *, nseq, nch, T, npad, gw):
    heads, P, N = h0.shape[1:]
    K, xbc_w = conv_w.shape
    wp = p.shape[1]
    padl = lambda v: jnp.pad(v.astype(_F32), (0, LANE - v.shape[0])).reshape(1, LANE)
    kern = functools.partial(_ssd_kernel, T=T, npad=npad, nch=nch, gw=gw, xbc_w=xbc_w, heads=heads, P=P, N=N, K=K)
    return pl.pallas_call(
        kern, grid=(nseq, nch),
        in_specs=[pl.BlockSpec((T, wp), lambda b, c: (b * nch + c, 0)),
                  pl.BlockSpec((1, heads, P, N), lambda b, c: (b, 0, 0, 0)),
                  _full((K, xbc_w)), _full((1, xbc_w)), _full((1, LANE)), _full((1, LANE)), _full((1, LANE)),
                  _full((1, gw))],
        out_specs=[pl.BlockSpec((T, gw), lambda b, c: (b * nch + c, 0)),
                   pl.BlockSpec((1, heads, P, N), lambda b, c: (b, 0, 0, 0))],
        out_shape=[jax.ShapeDtypeStruct((nseq * nch * T, gw), _BF16),
                   jax.ShapeDtypeStruct((nseq, heads, P, N), _F32)],
        scratch_shapes=[pltpu.VMEM((heads, P, N), _F32), pltpu.VMEM((T + 8, xbc_w), _F32)],
        compiler_params=_cparams(("parallel", "arbitrary"), T * wp * 4, 3 * heads * P * N * 4, T * gw * 2,
                                 (T + 8) * xbc_w * 4, 24 * T * max(T, LANE) * 4),
        name="ssd_mixer",
    )(p, h0, conv_w.astype(_F32), conv_b.reshape(1, xbc_w).astype(_F32), padl(dt_bias), padl(a_log), padl(d_skip),
      norm_w.reshape(1, gw).astype(_F32))


def _ret_kernel(p_ref, cc_ref, ss_ref, r0_ref, gw_ref, gb_ref, y_ref, ro_ref, rst, *, T, npad, nch, gw, heads, dk, dv):
    c = pl.program_id(1)
    qkw = heads * dk

    @pl.when(c == 0)
    def _init():
        rst[...] = r0_ref[0]

    npc = jnp.where(c == 0, npad, 0).astype(_F32)
    ri = lax.broadcasted_iota(jnp.int32, (T, T), 0)
    ci = lax.broadcasted_iota(jnp.int32, (T, T), 1)
    causal = ri >= ci
    dlt = (ri - ci).astype(_F32)
    idx = lax.broadcasted_iota(jnp.int32, (T, 1), 0).astype(_F32)
    cc = cc_ref[...]
    ss = ss_ref[...]
    ys = []
    for h in range(heads):
        lg = math.log(1.0 - 2.0 ** (-5.0 - h))
        qh = p_ref[:, h * dk:(h + 1) * dk]
        kh = p_ref[:, qkw + h * dk:qkw + (h + 1) * dk]
        vh = p_ref[:, 2 * qkw + h * dv:2 * qkw + (h + 1) * dv].astype(_BF16)
        qh = (qh * cc + pltpu.roll(qh, dk // 2, 1) * ss)
        kh = (kh * cc + pltpu.roll(kh, dk // 2, 1) * ss) * (dk ** -0.5)
        qb = qh.astype(_BF16)
        dmat = jnp.exp(jnp.where(causal, dlt * lg, -jnp.inf))
        inner = lax.dot_general(qb, kh.astype(_BF16), _NT, preferred_element_type=_F32) * dmat
        y_in = jnp.dot(inner.astype(_BF16), vh, preferred_element_type=_F32)
        rprev = rst[h]
        y_x = jnp.dot(qb, rprev.astype(_BF16), preferred_element_type=_F32) * jnp.exp((idx + 1.0 - npc) * lg)
        kw = (kh * jnp.exp((T - 1.0 - idx) * lg)).astype(_BF16)
        rst[h] = rprev * jnp.exp((T - npc) * lg) + lax.dot_general(kw, vh, _TN, preferred_element_type=_F32)
        o = y_in + y_x
        oc = o - jnp.mean(o, axis=-1, keepdims=True)
        ys.append(oc * lax.rsqrt(jnp.mean(oc * oc, axis=-1, keepdims=True) + RET_GN_EPS))
    gate = p_ref[:, 2 * qkw + gw:2 * qkw + 2 * gw]
    y = _silu(gate) * (jnp.concatenate(ys, axis=-1) * gw_ref[...] + gb_ref[...])
    y_ref[...] = y.astype(y_ref.dtype)

    @pl.when(c == nch - 1)
    def _fin():
        ro_ref[0] = rst[...]


def _ret_mixer(p, r0, gn_w, gn_b, pos0, *, nseq, nch, T, npad, gw):
    heads, dk, dv = r0.shape[1:]
    wp = p.shape[1]
    theta = 1.0 / (ROPE_BASE ** jnp.linspace(0.0, 1.0, dk // 2, dtype=_F32))
    ang = (pos0 + jnp.arange(nch * T) - npad).astype(_F32)[:, None] * theta
    cos, sin = jnp.cos(ang), jnp.sin(ang)
    cc = jnp.concatenate([cos, cos], axis=-1)
    ss = jnp.concatenate([-sin, sin], axis=-1)
    kern = functools.partial(_ret_kernel, T=T, npad=npad, nch=nch, gw=gw, heads=heads, dk=dk, dv=dv)
    return pl.pallas_call(
        kern, grid=(nseq, nch),
        in_specs=[pl.BlockSpec((T, wp), lambda b, c: (b * nch + c, 0)),
                  pl.BlockSpec((T, dk), lambda b, c: (c, 0)), pl.BlockSpec((T, dk), lambda b, c: (c, 0)),
                  pl.BlockSpec((1, heads, dk, dv), lambda b, c: (b, 0, 0, 0)),
                  _full((1, gw)), _full((1, gw))],
        out_specs=[pl.BlockSpec((T, gw), lambda b, c: (b * nch + c, 0)),
                   pl.BlockSpec((1, heads, dk, dv), lambda b, c: (b, 0, 0, 0))],
        out_shape=[jax.ShapeDtypeStruct((nseq * nch * T, gw), _BF16),
                   jax.ShapeDtypeStruct((nseq, heads, dk, dv), _F32)],
        scratch_shapes=[pltpu.VMEM((heads, dk, dv), _F32)],
        compiler_params=_cparams(("parallel", "arbitrary"), T * wp * 4, 3 * heads * dk * dv * 4, T * gw * 2,
                                 16 * T * max(T, dv) * 4),
        name="ret_mixer",
    )(p, cc, ss, r0, gn_w.reshape(1, gw).astype(_F32), gn_b.reshape(1, gw).astype(_F32))


S5_CB = 128


def _gelu_tanh(x):
    return 0.5 * x * (1.0 + jnp.tanh(math.sqrt(2.0 / math.pi) * (x + 0.044715 * (x * x * x))))


def _s5_kernel(*refs, TC, nb, npad, nch, gw, sb, perm):
    nu = nb if perm else 1
    u_refs = refs[:nu]
    (s0r_ref, s0i_ref, are_ref, aim_ref, ldt_ref, wbr_ref, wbi_ref, wcr_ref, wci_ref, d_ref, gluw_ref, glub_ref,
     nw_ref, y_ref, sor_ref, soi_ref, xr, xi, str_, sti) = refs[nu:]
    c = pl.program_id(1)
    nblk = gw // S5_CB
    R = TC * nb

    @pl.when(c == 0)
    def _init():
        str_[...] = s0r_ref[...]
        sti[...] = s0i_ref[...]

    dt = jnp.exp(ldt_ref[...])
    lr, li = are_ref[...], aim_ref[...]
    mag = jnp.exp(lr * dt)
    abr, abi = mag * jnp.cos(li * dt), mag * jnp.sin(li * dt)
    den = lr * lr + li * li
    nr, ni = abr - 1.0, abi
    er, ei = (nr * lr + ni * li) / den, (ni * lr - nr * li) / den

    if perm:
        ti = lax.broadcasted_iota(jnp.int32, (R, R), 0)
        bi_ = lax.broadcasted_iota(jnp.int32, (R, R), 1)
        pmat = jnp.logical_and(ti // nb == bi_ % TC, ti % nb == bi_ // TC).astype(_F32)
        u = _dot_sel(pmat, jnp.concatenate([r[...] for r in u_refs], axis=0), (((1,), (0,)), ((), ())))
    else:
        u = u_refs[0][...]
    ub = u.astype(_BF16)
    for g in range(nblk):
        us = ub[:, g * S5_CB:(g + 1) * S5_CB]
        br = jnp.dot(us, wbr_ref[g], preferred_element_type=_F32)
        bi = jnp.dot(us, wbi_ref[g], preferred_element_type=_F32)
        e_r, e_i = er[:, g * sb:(g + 1) * sb], ei[:, g * sb:(g + 1) * sb]
        xr[:, g * sb:(g + 1) * sb] = e_r * br - e_i * bi
        xi[:, g * sb:(g + 1) * sb] = e_r * bi + e_i * br

    rows = max(nb, 8)
    spi = rows // nb

    def body(i, carry):
        r0 = pl.multiple_of(i * rows, 8)
        b_r, b_i = xr[pl.ds(r0, rows), :], xi[pl.ds(r0, rows), :]
        s_r, s_i = str_[...], sti[...]
        outs_r, outs_i = [], []
        for j in range(spi):
            n_r = abr * s_r - abi * s_i + b_r[j * nb:(j + 1) * nb]
            n_i = abr * s_i + abi * s_r + b_i[j * nb:(j + 1) * nb]
            s_r, s_i = n_r, n_i
            outs_r.append(n_r)
            outs_i.append(n_i)
        xr[pl.ds(r0, rows), :] = outs_r[0] if spi == 1 else jnp.concatenate(outs_r, axis=0)
        xi[pl.ds(r0, rows), :] = outs_i[0] if spi == 1 else jnp.concatenate(outs_i, axis=0)
        str_[...] = s_r
        sti[...] = s_i
        return carry

    start = jnp.clip(npad - c * TC, 0, TC) // spi
    lax.fori_loop(start, TC // spi, body, 0)

    ys = []
    for g in range(nblk):
        xrb = xr[:, g * sb:(g + 1) * sb].astype(_BF16)
        xib = xi[:, g * sb:(g + 1) * sb].astype(_BF16)
        ys.append(jnp.dot(xrb, wcr_ref[g], preferred_element_type=_F32)
                  - jnp.dot(xib, wci_ref[g], preferred_element_type=_F32))
    y = jnp.concatenate(ys, axis=-1) + d_ref[...] * u
    gy = _gelu_tanh(y)
    out = gy * jax.nn.sigmoid(jnp.dot(gy.astype(_BF16), gluw_ref[...], preferred_element_type=_F32) + glub_ref[...])
    out = (out * lax.rsqrt(jnp.mean(out * out, axis=-1, keepdims=True) + EPS) * nw_ref[...]).astype(y_ref.dtype)
    if perm:
        out = lax.dot_general(pmat.astype(out.dtype), out, _TN, preferred_element_type=_F32).astype(y_ref.dtype)
        for b in range(nb):
            y_ref[b] = out[b * TC:(b + 1) * TC]
    else:
        y_ref[...] = out

    @pl.when(c == nch - 1)
    def _fin():
        sor_ref[...] = str_[...]
        soi_ref[...] = sti[...]


def _s5_mixer(u, s0_re, s0_im, a_re, a_im, log_dt, b_re, b_im, c_re, c_im, d, glu_w, glu_b, norm_w, *,
              nseq, nch, T, npad, nb, TC, perm):
    groups, ns = a_re.shape
    gw = groups * S5_GROUP
    gpb = S5_CB // S5_GROUP
    nblk = gw // S5_CB
    sb = gpb * ns
    nst = groups * ns
    nsb = nseq // nb
    ncc = nch * T // TC
    assert npad % max(1, 8 // nb) == 0
    R = TC * nb
    if perm:
        assert nsb == 1
        u_args = [u] * nb
        u_specs = [pl.BlockSpec((TC, gw), functools.partial(lambda s, c, b: (b * ncc + c, 0), b=b)) for b in range(nb)]
        y_spec = pl.BlockSpec((nb, TC, gw), lambda s, c: (0, c, 0))
        y_shape = jax.ShapeDtypeStruct((nb, ncc * TC, gw), _BF16)
    else:
        u_args = [u[:, :gw].reshape(nsb, nb, ncc, TC, gw).transpose(0, 2, 3, 1, 4).reshape(nseq * nch * T, gw)]
        u_specs = [pl.BlockSpec((R, gw), lambda s, c: (s * ncc + c, 0))]
        y_spec = pl.BlockSpec((R, gw), lambda s, c: (s * ncc + c, 0))
        y_shape = jax.ShapeDtypeStruct((nseq * nch * T, gw), _BF16)
    eye = jnp.eye(gpb, dtype=_F32)

    def bd_in(w):
        w4 = w.reshape(nblk, gpb, ns, S5_GROUP)
        return jnp.einsum('bgnc,gh->bgchn', w4, eye).reshape(nblk, S5_CB, sb).astype(_BF16)

    def bd_out(w):
        w4 = w.reshape(nblk, gpb, S5_GROUP, ns)
        return jnp.einsum('bgcn,gh->bgnhc', w4, eye).reshape(nblk, sb, S5_CB).astype(_BF16)

    row = lambda v: v.reshape(1, -1).astype(_F32)
    kern = functools.partial(_s5_kernel, TC=TC, nb=nb, npad=npad, nch=ncc, gw=gw, sb=sb, perm=perm)
    y, so_r, so_i = pl.pallas_call(
        kern, grid=(nsb, ncc),
        in_specs=u_specs + [
            pl.BlockSpec((nb, nst), lambda s, c: (s, 0)), pl.BlockSpec((nb, nst), lambda s, c: (s, 0)),
            _full((1, nst)), _full((1, nst)), _full((1, nst)),
            _full((nblk, S5_CB, sb)), _full((nblk, S5_CB, sb)), _full((nblk, sb, S5_CB)),
            _full((nblk, sb, S5_CB)), _full((1, gw)), _full((gw, gw)), _full((1, gw)), _full((1, gw))],
        out_specs=[y_spec, pl.BlockSpec((nb, nst), lambda s, c: (s, 0)), pl.BlockSpec((nb, nst), lambda s, c: (s, 0))],
        out_shape=[y_shape, jax.ShapeDtypeStruct((nseq, nst), _F32), jax.ShapeDtypeStruct((nseq, nst), _F32)],
        scratch_shapes=[pltpu.VMEM((R, nst), _F32), pltpu.VMEM((R, nst), _F32),
                        pltpu.VMEM((nb, nst), _F32), pltpu.VMEM((nb, nst), _F32)],
        compiler_params=_cparams(("parallel", "arbitrary"), 2 * R * gw * 4, R * gw * 2, 4 * nblk * S5_CB * sb * 2,
                                 gw * gw * 2, R * nst * 4, 6 * max(nb, 8) * nst * 4),
        name="s5_mixer",
    )(*u_args, s0_re.reshape(nseq, nst), s0_im.reshape(nseq, nst), row(a_re), row(a_im),
      row(jnp.broadcast_to(log_dt[:, None], (groups, ns))), bd_in(b_re), bd_in(b_im), bd_out(c_re), bd_out(c_im),
      row(d), glu_w.astype(_BF16), row(glu_b), row(norm_w))
    if perm:
        y = y.reshape(nseq * nch * T, gw)
    else:
        y = y.reshape(nsb, ncc, TC, nb, gw).transpose(0, 3, 1, 2, 4).reshape(nseq * nch * T, gw)
    return y, so_r.reshape(nseq, groups, ns), so_i.reshape(nseq, groups, ns)


def _head_sums(x):
    ri = lax.broadcasted_iota(jnp.int32, (LANE, LANE), 0) // RWKV_HEAD_DIM
    ci = lax.broadcasted_iota(jnp.int32, (LANE, LANE), 1) // RWKV_HEAD_DIM
    e = (ri == ci).astype(_F32)
    nn = (((1,), (0,)), ((), ()))
    return jnp.concatenate([_dot_sel_r(x[:, j:j + LANE], e, nn) for j in range(0, x.shape[1], LANE)], axis=-1)


def _rwkv_pre_kernel(p_ref, mu_ref, w0_ref, w2_ref, a0_ref, a2_ref, g2_ref, kk_ref, ka_ref, rk_ref,
                     r_o, k_o, v_o, w_o, kk_o, b_o, g_o, bon_o, xbuf, *, T, gw, dl, da, dg):
    c = pl.program_id(1)

    @pl.when(c == 0)
    def _init():
        xbuf[0:8, :] = jnp.zeros((8, xbuf.shape[1]), _F32)

    p = p_ref[...]
    xbuf[8:8 + T, :] = p
    prev = xbuf[7:7 + T, :]
    last = xbuf[7 + T:8 + T, :]
    xbuf[7:8, :] = last
    pm = p + (prev - p) * mu_ref[...]
    r, k, v = pm[:, :gw], pm[:, gw:2 * gw], pm[:, 2 * gw:3 * gw]
    o1 = 3 * gw
    wl, al, gl = pm[:, o1:o1 + dl], pm[:, o1 + dl:o1 + dl + da], pm[:, o1 + dl + da:o1 + dl + da + dg]
    wx = w0_ref[...] + jnp.dot(jnp.tanh(wl).astype(_BF16), w2_ref[...], preferred_element_type=_F32)
    decay = jnp.exp(-jnp.exp(-_softplus(-wx) - 0.5))
    a = jax.nn.sigmoid(a0_ref[...] + jnp.dot(al.astype(_BF16), a2_ref[...], preferred_element_type=_F32))
    g = jnp.dot(jax.nn.sigmoid(gl).astype(_BF16), g2_ref[...], preferred_element_type=_F32)
    kkf = k * kk_ref[...]
    kk = kkf / jnp.maximum(jnp.sqrt(_head_sums(kkf * kkf)), 1e-12)
    k2 = k * (1.0 + (a - 1.0) * ka_ref[...])
    r_o[...] = r
    k_o[...] = k2
    v_o[...] = v
    w_o[...] = decay
    kk_o[...] = kk
    b_o[...] = kk * a
    g_o[...] = g
    bon_o[...] = _head_sums(r * k2 * rk_ref[...]) * v


def _rwkv_scan_kernel(r_ref, k_ref, v_ref, w_ref, kk_ref, b_ref, s0_ref, o_ref, so_ref, st, *, TC, K, npad, nch):
    c = pl.program_id(1)

    @pl.when(c == 0)
    def _init():
        st[...] = s0_ref[...]

    start = jnp.clip(npad - c * TC, 0, TC)

    @pl.when(start > 0)
    def _zero():
        o_ref[...] = jnp.zeros(o_ref.shape, _F32)

    vecs = (kk_ref, w_ref, b_ref, k_ref, r_ref)
    NACC = 4

    def body(t, carry):
        row = lambda i, k: vecs[i][k, pl.ds(t, 1), :]
        acc = [st[k] * row(0, k) for k in range(NACC)]
        for k in range(NACC, K):
            acc[k % NACC] = acc[k % NACC] + st[k] * row(0, k)
        sa = (acc[0] + acc[1]) + (acc[2] + acc[3])
        v_t = v_ref[t]
        acc = [None] * NACC
        for k in range(K):
            s = st[k] * row(1, k) - sa * row(2, k) + v_t * row(3, k)
            st[k] = s
            acc[k % NACC] = s * row(4, k) if acc[k % NACC] is None else acc[k % NACC] + s * row(4, k)
        o_ref[t] = (acc[0] + acc[1]) + (acc[2] + acc[3])
        return carry

    lax.fori_loop(start, TC, body, 0)

    @pl.when(c == nch - 1)
    def _fin():
        so_ref[...] = st[...]


def _rwkv_relayout_kernel(*refs, nseq, H, K, J):
    x_refs, o_ref, zs = refs[:nseq], refs[nseq], refs[nseq + 1]
    gw = H * K
    for n in range(nseq):
        x = x_refs[n][...]
        for c0 in range(0, gw, LANE):
            zs[n * gw + c0:n * gw + c0 + LANE, :] = x[:, c0:c0 + LANE].T
    for k in range(K):
        q = zs[pl.ds(k, nseq * H, stride=K), :]
        o_ref[k] = jnp.concatenate([q] * J, axis=0).T


def _rwkv_relayout(x, *, nseq, nch, H, K, J):
    gw = H * K
    nl = J * nseq * H
    assert nl == LANE
    specs = [pl.BlockSpec((CHUNK, gw), functools.partial(lambda c, n: (n * nch + c, 0), n=n)) for n in range(nseq)]
    return pl.pallas_call(
        functools.partial(_rwkv_relayout_kernel, nseq=nseq, H=H, K=K, J=J), grid=(nch,),
        in_specs=specs, out_specs=pl.BlockSpec((K, CHUNK, nl), lambda c: (0, c, 0)),
        out_shape=jax.ShapeDtypeStruct((K, nch * CHUNK, nl), _F32),
        scratch_shapes=[pltpu.VMEM((nseq * gw, CHUNK), _F32)],
        compiler_params=_cparams(("parallel",), nseq * CHUNK * gw * 4, K * CHUNK * nl * 4, nseq * gw * CHUNK * 4),
        name="rwkv_relayout",
    )(*([x] * nseq))


def _rwkv_post_kernel(o_ref, bon_ref, g_ref, lw_ref, lb_ref, y_ref):
    o = o_ref[...]
    oc = o - _head_sums(o) * (1.0 / RWKV_HEAD_DIM)
    var = _head_sums(oc * oc) * (1.0 / RWKV_HEAD_DIM)
    y = (oc * lax.rsqrt(var + RWKV_GN_EPS) * lw_ref[...] + lb_ref[...] + bon_ref[...]) * g_ref[...]
    y_ref[...] = y.astype(y_ref.dtype)


def _rwkv_mixer(p, s0, mu_p, w0, w2_p, a0, a2_p, g2_p, k_k, k_a, r_k, ln_w, ln_b, *, nseq, nch, T, npad, gw,
                dl, da, dg, J, TC):
    H, V, K = s0.shape[1:]
    wp = p.shape[1]
    rows = nseq * nch * T
    L = nch * T
    row = lambda v: v.reshape(1, -1).astype(_F32)
    f32rows = jax.ShapeDtypeStruct((rows, gw), _F32)
    blk = pl.BlockSpec((T, gw), lambda b, c: (b * nch + c, 0))
    pre = pl.pallas_call(
        functools.partial(_rwkv_pre_kernel, T=T, gw=gw, dl=dl, da=da, dg=dg), grid=(nseq, nch),
        in_specs=[pl.BlockSpec((T, wp), lambda b, c: (b * nch + c, 0)), _full((1, wp)), _full((1, gw)),
                  _full((dl, gw)), _full((1, gw)), _full((da, gw)), _full((dg, gw)), _full((1, gw)), _full((1, gw)),
                  _full((1, gw))],
        out_specs=[blk] * 8, out_shape=[f32rows] * 8,
        scratch_shapes=[pltpu.VMEM((T + 8, wp), _F32)],
        compiler_params=_cparams(("parallel", "arbitrary"), 2 * T * wp * 4, 8 * T * gw * 4, 12 * T * gw * 4),
        name="rwkv_pre",
    )(p, row(mu_p), row(w0), w2_p.astype(_BF16), row(a0), a2_p.astype(_BF16), g2_p.astype(_BF16), row(k_k),
      row(k_a), row(r_k))
    r, k2, v, decay, kk, bvec, g, bonus = pre

    VI = V // J
    NL = J * nseq * H
    assert NL % LANE == 0 or NL < LANE

    def kvec(x):
        if T == CHUNK and NL == LANE:
            return _rwkv_relayout(x, nseq=nseq, nch=nch, H=H, K=K, J=J)
        y = x.reshape(nseq, L, H, K).transpose(3, 1, 0, 2).reshape(K, L, 1, nseq * H)
        return jnp.broadcast_to(y, (K, L, J, nseq * H)).reshape(K, L, NL)

    vv = v.reshape(nseq, L, H, J, VI).transpose(1, 4, 3, 0, 2).reshape(L, VI, NL)
    st0 = s0.reshape(nseq, H, J, VI, K).transpose(4, 3, 2, 0, 1).reshape(K, VI, NL)
    lb = min(NL, LANE)
    ncc = L // TC
    kspec = pl.BlockSpec((K, TC, lb), lambda n, c: (0, c, n))
    vspec = pl.BlockSpec((TC, VI, lb), lambda n, c: (c, 0, n))
    sspec = pl.BlockSpec((K, VI, lb), lambda n, c: (0, 0, n))
    o, st1 = pl.pallas_call(
        functools.partial(_rwkv_scan_kernel, TC=TC, K=K, npad=npad, nch=ncc), grid=(NL // lb, ncc),
        in_specs=[kspec, kspec, vspec, kspec, kspec, kspec, sspec],
        out_specs=[vspec, sspec],
        out_shape=[jax.ShapeDtypeStruct((L, VI, NL), _F32), jax.ShapeDtypeStruct((K, VI, NL), _F32)],
        scratch_shapes=[pltpu.VMEM((K, VI, lb), _F32)],
        compiler_params=_cparams(("parallel", "arbitrary"), 5 * TC * K * lb * 4, 2 * TC * VI * lb * 4,
                                 3 * VI * K * lb * 4),
        name="rwkv_scan",
    )(kvec(r), kvec(k2), vv, kvec(decay), kvec(kk), kvec(bvec), st0)
    o = o.reshape(L, VI, J, nseq, H).transpose(3, 0, 4, 2, 1).reshape(rows, gw)
    s_new = st1.reshape(K, VI, J, nseq, H).transpose(3, 4, 2, 1, 0).reshape(nseq, H, V, K)

    tr = _pick(rows, (256, 128, 64, 32, 16, 8))
    rblk = pl.BlockSpec((tr, gw), lambda i: (i, 0))
    y = pl.pallas_call(
        _rwkv_post_kernel, grid=(rows // tr,),
        in_specs=[rblk, rblk, rblk, _full((1, gw)), _full((1, gw))],
        out_specs=rblk, out_shape=jax.ShapeDtypeStruct((rows, gw), _BF16),
        compiler_params=_cparams(("parallel",), 4 * tr * gw * 4, 8 * tr * gw * 4),
        name="rwkv_post",
    )(o, bonus, g, row(ln_w), row(ln_b))
    return y, s_new


def kernel(x_prompt, x_sample, state_ssd, state_ssd_conv, state_rwkv, state_rwkv_shift, state_ret, state_s5_re,
           state_s5_im, meta, ln_mix, w_in, ssd_conv_w, ssd_conv_b, ssd_dt_bias, ssd_a_log, ssd_d, ssd_norm, rwkv_mu,
           rwkv_w0, rwkv_w2, rwkv_a0, rwkv_a2, rwkv_g2, rwkv_k_k, rwkv_k_a, rwkv_r_k, rwkv_ln_w, rwkv_ln_b, ret_gn_w,
           ret_gn_b, s5_a_re, s5_a_im, s5_log_dt, s5_b_re, s5_b_im, s5_c_re, s5_c_im, s5_d, s5_glu_w, s5_glu_b,
           s5_norm, w_out, ln_ffn, w_gate, w_up, w_down, ln_f):
    bp, sp, d = x_prompt.shape
    bs, ls, _ = x_sample.shape
    depth = w_in.shape[0]
    gw = d // 4
    ssd_heads, ssd_p, ssd_n = state_ssd.shape[2:]
    xbc_w = state_ssd_conv.shape[-1]
    rwkv_proj = state_rwkv_shift.shape[-1]
    ret_heads, ret_dk, ret_dv = state_ret.shape[2:]
    qkw = ret_heads * ret_dk
    dff = w_gate.shape[-1]

    lreal = N_META + sp
    lp = _round_up(lreal, CHUNK)
    pad = lp - lreal
    mp, ms = bp * lp, bs * ls
    m = mp + ms

    head = jnp.concatenate([jnp.zeros((pad, d), _F32), meta.astype(_F32)], axis=0)
    h = jnp.concatenate([piece for b in range(bp) for piece in (head, x_prompt[b])] + [x_sample.reshape(ms, d)],
                        axis=0)

    in_splits = (gw, xbc_w, ssd_heads, rwkv_proj, qkw, qkw, gw, gw, gw)
    offs = [0]
    for s in in_splits:
        offs.append(offs[-1] + s)
    ntile = 512 if gw % 512 == 0 else LANE

    def pack(l, c0, c1):
        w = w_in[l, :, c0:c1].astype(_BF16)
        return jnp.pad(w, ((0, 0), (0, _round_up(c1 - c0, ntile) - (c1 - c0))))

    layers = range(depth)
    w_ssd_b = [pack(l, offs[0], offs[3]) for l in layers]
    w_ret_b = [pack(l, offs[4], offs[8]) for l in layers]
    w_s5_b = [pack(l, offs[8], offs[9]) for l in layers]

    lora_c = (rwkv_w2.shape[1], rwkv_a2.shape[1], rwkv_g2.shape[1])
    dl, da, dg = (_round_up(n, LANE) for n in lora_c)
    rw_w = _round_up(3 * gw + dl + da + dg, ntile)

    def rw_pad(x):
        parts, o = [x[..., :3 * gw]], 3 * gw
        for n, npd in zip(lora_c, (dl, da, dg)):
            parts.append(jnp.pad(x[..., o:o + n], [(0, 0)] * (x.ndim - 1) + [(0, npd - n)]))
            o += n
        y = jnp.concatenate(parts, axis=-1)
        return jnp.pad(y, [(0, 0)] * (x.ndim - 1) + [(0, rw_w - y.shape[-1])])

    def rw_compact(x):
        parts, o = [x[..., :3 * gw]], 3 * gw
        for n, npd in zip(lora_c, (dl, da, dg)):
            parts.append(x[..., o:o + n])
            o += npd
        return jnp.concatenate(parts, axis=-1)

    w_rw_b = [rw_pad(w_in[l, :, offs[3]:offs[4]].astype(_BF16)) for l in layers]
    mu_p = rw_pad(rwkv_mu)
    padrows = lambda w, n: jnp.pad(w, ((0, 0), (0, n - w.shape[1]), (0, 0)))
    w2_p, a2_p, g2_p = padrows(rwkv_w2, dl), padrows(rwkv_a2, da), padrows(rwkv_g2, dg)
    w_out_b = [w_out[l].astype(_BF16) for l in layers]
    w_gate_b = [w_gate[l].astype(_BF16) for l in layers]
    w_up_b = [w_up[l].astype(_BF16) for l in layers]
    w_down_b = [w_down[l].astype(_BF16) for l in layers]
    half = dff // 2

    zeros_b = lambda s: jnp.zeros((bp,) + s.shape[2:], _F32)
    nch_p = lp // CHUNK
    ts = _round_up(ls + 3, 8)
    spad = ts - ls

    def sample_rows(rows, hist=None, c0=0):
        r3 = jnp.pad(rows.reshape(bs, ls, -1), ((0, 0), (spad, 0), (0, 0)))
        if hist is not None:
            k, w = hist.shape[1:]
            r3 = r3.at[:, spad - k:spad, c0:c0 + w].set(hist)
        return r3.reshape(bs * ts, -1)

    def unsample(y):
        return y.reshape(bs, ts, -1)[:, spad:].reshape(ms, -1)

    def last_rows(rows, k):
        return jnp.stack([rows[(b + 1) * lp - k:(b + 1) * lp] for b in range(bp)])

    outs_p = [[] for _ in range(7)]
    outs_s = [[] for _ in range(7)]
    for l in range(depth):
        hn = _rmsnorm(h, ln_mix[l], _BF16, lp=lp, pad=pad, mp=mp)

        pa = _mm(hn, w_ssd_b[l], name="in_proj_ssd")
        ssd_w = (ssd_conv_w[l], ssd_conv_b[l], ssd_dt_bias[l], ssd_a_log[l], ssd_d[l], ssd_norm[l])
        ya_p, hp_new = _ssd_mixer(pa, zeros_b(state_ssd), *ssd_w, nseq=bp, nch=nch_p, T=CHUNK, npad=pad, gw=gw)
        pa_s = sample_rows(pa[mp:], state_ssd_conv[l], gw)
        ya_s, hs_new = _ssd_mixer(pa_s, state_ssd[l], *ssd_w, nseq=bs, nch=1, T=ts, npad=spad, gw=gw)
        kc = ssd_conv_w.shape[1] - 1
        outs_p[0].append(hp_new)
        outs_p[1].append(last_rows(pa, kc)[..., gw:gw + xbc_w])
        outs_s[0].append(hs_new)
        outs_s[1].append(pa_s.reshape(bs, ts, -1)[:, ts - kc:, gw:gw + xbc_w])

        pb = _mm(hn, w_rw_b[l], name="in_proj_rwkv")
        rw_args = (mu_p[l], rwkv_w0[l], w2_p[l], rwkv_a0[l], a2_p[l], g2_p[l], rwkv_k_k[l], rwkv_k_a[l],
                   rwkv_r_k[l], rwkv_ln_w[l], rwkv_ln_b[l])
        rw_kw = dict(gw=gw, dl=dl, da=da, dg=dg)
        rw_heads = state_rwkv.shape[2]
        yb_p, sp_new = _rwkv_mixer(pb, zeros_b(state_rwkv), *rw_args, nseq=bp, nch=nch_p, T=CHUNK, npad=pad,
                                   J=max(1, LANE // (bp * rw_heads)), TC=CHUNK // 4, **rw_kw)
        pb_s = sample_rows(pb[mp:], rw_pad(state_rwkv_shift[l])[:, None, :], 0)
        yb_s, ss_new = _rwkv_mixer(pb_s, state_rwkv[l], *rw_args, nseq=bs, nch=1, T=ts, npad=spad,
                                   J=max(1, LANE // (bs * rw_heads)), TC=ts, **rw_kw)
        outs_p[2].append(sp_new)
        outs_p[3].append(rw_compact(last_rows(pb, 1)[:, 0]))
        outs_s[2].append(ss_new)
        outs_s[3].append(rw_compact(pb_s.reshape(bs, ts, -1)[:, -1]))

        pc = _mm(hn, w_ret_b[l], name="in_proj_ret")
        yc_p, rp_new = _ret_mixer(pc, zeros_b(state_ret), ret_gn_w[l], ret_gn_b[l], 0,
                                  nseq=bp, nch=nch_p, T=CHUNK, npad=pad, gw=gw)
        yc_s, rs_new = _ret_mixer(sample_rows(pc[mp:]), state_ret[l], ret_gn_w[l], ret_gn_b[l], PAST_LEN,
                                  nseq=bs, nch=1, T=ts, npad=spad, gw=gw)
        outs_p[4].append(rp_new)
        outs_s[4].append(rs_new)

        pd = _mm(hn, w_s5_b[l], name="in_proj_s5")
        s5_w = (s5_a_re[l], s5_a_im[l], s5_log_dt[l], s5_b_re[l], s5_b_im[l], s5_c_re[l], s5_c_im[l], s5_d[l],
                s5_glu_w[l], s5_glu_b[l], s5_norm[l])
        yd_p, s5r_p, s5i_p = _s5_mixer(pd, zeros_b(state_s5_re), zeros_b(state_s5_im), *s5_w, nseq=bp, nch=nch_p,
                                       T=CHUNK, npad=pad, nb=bp, TC=CHUNK // 2, perm=True)
        yd_s, s5r_s, s5i_s = _s5_mixer(sample_rows(pd[mp:]), state_s5_re[l], state_s5_im[l], *s5_w, nseq=bs, nch=1,
                                       T=ts, npad=spad, nb=min(bs, 32), TC=ts, perm=False)
        outs_p[5].append(s5r_p)
        outs_p[6].append(s5i_p)
        outs_s[5].append(s5r_s)
        outs_s[6].append(s5i_s)

        h = _outproj((ya_p, yb_p, yc_p, yd_p), tuple(unsample(y) for y in (ya_s, yb_s, yc_s, yd_s)), w_out_b[l], h)
        hn = _rmsnorm(h, ln_ffn[l], _BF16)
        ff = _swiglu(hn, w_gate_b[l], w_up_b[l])
        h = _mm(ff, w_down_b[l], res=h, kb=0, tk=half, name="ffn_down0")
        h = _mm(ff, w_down_b[l], res=h, kb=1, tk=half, name="ffn_down1")

    y_prompt, y_sample = _final_norm(h, ln_f, bp=bp, sp=sp, lp=lp, ms=ms)
    return ((y_prompt.reshape(bp, sp, d), y_sample.reshape(bs, ls, d))
            + tuple(jnp.stack(a) for a in outs_p) + tuple(jnp.stack(a) for a in outs_s))
```

```python
import functools
import math

import jax
import jax.numpy as jnp
from jax import lax
from jax.experimental import pallas as pl
from jax.experimental.pallas import tpu as pltpu

N_META = 16
EPS = 1e-6
SSD_GROUPS = 2
RWKV_HEAD_DIM = 64
RWKV_GN_EPS = 64e-5
RET_GN_EPS = 1e-5
ROPE_BASE = 10000.0
S5_GROUP = 16
PAST_LEN = 16384
CHUNK = 128
LANE = 128
V7X_VMEM_CAP = 60 * 1024 * 1024

_F32 = jnp.float32
_BF16 = jnp.bfloat16


def _pick(n, cands):
    for c in cands:
        if n % c == 0:
            return c
    raise ValueError(f"no tile in {cands} divides {n}")


def _round_up(n, m):
    return -(-n // m) * m


def _cparams(sem, *block_bytes):
    need = 2 * sum(block_bytes) + (6 << 20)
    return pltpu.CompilerParams(dimension_semantics=sem, vmem_limit_bytes=int(min(max(need, 16 << 20), V7X_VMEM_CAP)))


def _rmsnorm_kernel(x_ref, g_ref, o_ref, *, tr, lp, pad, mp):
    x = x_ref[...]
    y = x * lax.rsqrt(jnp.mean(x * x, axis=-1, keepdims=True) + EPS) * g_ref[...]
    if pad:
        row0 = pl.program_id(0) * tr
        pos0 = lax.rem(row0, lp)
        rows = lax.broadcasted_iota(jnp.int32, (tr, 1), 0)
        is_pad = jnp.logical_and(row0 < mp, pos0 + rows < pad)
        y = jnp.where(is_pad, 0.0, y)
    o_ref[...] = y.astype(o_ref.dtype)


def _rmsnorm(x, g, out_dtype, lp=0, pad=0, mp=0):
    m, d = x.shape
    tr = _pick(math.gcd(m, lp) if pad else m, (256, 128, 64, 32, 16, 8))
    kern = functools.partial(_rmsnorm_kernel, tr=tr, lp=lp, pad=pad, mp=mp)
    return pl.pallas_call(
        kern, grid=(m // tr,),
        in_specs=[pl.BlockSpec((tr, d), lambda i: (i, 0)), pl.BlockSpec((1, d), lambda i: (0, 0))],
        out_specs=pl.BlockSpec((tr, d), lambda i: (i, 0)),
        out_shape=jax.ShapeDtypeStruct((m, d), out_dtype),
        compiler_params=_cparams(("parallel",), tr * d * 4, tr * d * 4),
        name="rmsnorm",
    )(x, g.reshape(1, d).astype(_F32))


def _final_norm(h, g, *, bp, sp, lp, ms):
    d = h.shape[1]
    assert lp - sp == CHUNK and sp % CHUNK == 0
    nch, nout = lp // CHUNK, sp // CHUNK
    kern = functools.partial(_rmsnorm_kernel, tr=CHUNK, lp=0, pad=0, mp=0)
    g2 = g.reshape(1, d).astype(_F32)
    yp = pl.pallas_call(
        kern, grid=(bp, nout),
        in_specs=[pl.BlockSpec((CHUNK, d), lambda b, j: (b * nch + 1 + j, 0)), pl.BlockSpec((1, d), lambda b, j: (0, 0))],
        out_specs=pl.BlockSpec((CHUNK, d), lambda b, j: (b * nout + j, 0)),
        out_shape=jax.ShapeDtypeStruct((bp * sp, d), _F32),
        compiler_params=_cparams(("parallel", "parallel"), CHUNK * d * 4, CHUNK * d * 4),
        name="final_norm_prompt",
    )(h, g2)
    tr = _pick(math.gcd(ms, bp * lp), (256, 128, 64, 32, 16, 8))
    off = bp * lp // tr
    ys = pl.pallas_call(
        functools.partial(_rmsnorm_kernel, tr=tr, lp=0, pad=0, mp=0), grid=(ms // tr,),
        in_specs=[pl.BlockSpec((tr, d), lambda i: (off + i, 0)), pl.BlockSpec((1, d), lambda i: (0, 0))],
        out_specs=pl.BlockSpec((tr, d), lambda i: (i, 0)),
        out_shape=jax.ShapeDtypeStruct((ms, d), _F32),
        compiler_params=_cparams(("parallel",), tr * d * 4, tr * d * 4),
        name="final_norm_sample",
    )(h, g2)
    return yp, ys


def _pack_kernel(w_ref, *o_refs, plans):
    x = w_ref[...]
    tr = x.shape[0]
    for o_ref, plan in zip(o_refs, plans):
        done = 0
        for dst, src, n in plan:
            npd = _round_up(n, LANE)
            assert dst % LANE == 0 and dst >= done
            if dst > done:
                o_ref[:, done:dst] = jnp.zeros((tr, dst - done), o_ref.dtype)
            seg = x[:, src:src + npd]
            if npd > n:
                seg = jnp.where(lax.broadcasted_iota(jnp.int32, (tr, npd), 1) < n, seg, 0.0)
            o_ref[:, dst:dst + npd] = seg.astype(o_ref.dtype)
            done = dst + npd
        if done < o_ref.shape[1]:
            o_ref[:, done:] = jnp.zeros((tr, o_ref.shape[1] - done), o_ref.dtype)


def _pack_w_in(w_in, l, plans, widths):
    _, d, nin = w_in.shape
    assert all(src + _round_up(n, LANE) <= nin for plan in plans for _, src, n in plan)
    tr = _pick(d, (128, 64, 32, 16, 8))
    return pl.pallas_call(
        functools.partial(_pack_kernel, plans=plans), grid=(d // tr,),
        in_specs=[pl.BlockSpec((None, tr, nin), lambda i: (l, i, 0))],
        out_specs=[pl.BlockSpec((tr, w), lambda i: (i, 0)) for w in widths],
        out_shape=[jax.ShapeDtypeStruct((d, w), _BF16) for w in widths],
        compiler_params=_cparams(("parallel",), 2 * tr * nin * 4, tr * sum(widths) * 2),
        name="pack_w_in",
    )(w_in)


def _mm_kernel(*refs, has_res):
    if has_res:
        x_ref, w_ref, r_ref, o_ref = refs
    else:
        x_ref, w_ref, o_ref = refs
    acc = jnp.dot(x_ref[...], w_ref[...], preferred_element_type=_F32)
    if has_res:
        acc = acc + r_ref[...]
    o_ref[...] = acc.astype(o_ref.dtype)


def _mm(x, w, res=None, out_dtype=_F32, kb=0, tk=None, name="matmul"):
    m = x.shape[0]
    n = w.shape[1]
    tk = tk or w.shape[0]
    tm = _pick(m, (1024, 512, 256, 128))
    if tk > 4096:
        tm = _pick(m, (512, 256, 128))
    tn = _pick(n, (512, 256, 128))
    in_specs = [pl.BlockSpec((tm, tk), lambda i, j: (i, kb)), pl.BlockSpec((tk, tn), lambda i, j: (kb, j))]
    args = [x, w]
    blocks = [tm * tk * 2, tk * tn * 2, tm * tn * 4]
    if res is not None:
        in_specs.append(pl.BlockSpec((tm, tn), lambda i, j: (i, j)))
        args.append(res)
        blocks.append(tm * tn * 4)
    return pl.pallas_call(
        functools.partial(_mm_kernel, has_res=res is not None), grid=(m // tm, n // tn),
        in_specs=in_specs, out_specs=pl.BlockSpec((tm, tn), lambda i, j: (i, j)),
        out_shape=jax.ShapeDtypeStruct((m, n), out_dtype),
        compiler_params=_cparams(("parallel", "arbitrary"), *blocks),
        name=name,
    )(*args)


def _outproj_kernel(*refs, nmix, gw):
    xs = refs[:nmix]
    w_ref, r_ref, o_ref = refs[nmix:]
    acc = r_ref[...]
    for j, x in enumerate(xs):
        acc = acc + jnp.dot(x[...], w_ref[j * gw:(j + 1) * gw, :], preferred_element_type=_F32)
    o_ref[...] = acc


def _outproj(ys, w, h, row0):
    nrows, gw = ys[0].shape
    nmix = len(ys)
    d, n = w.shape
    tm = _pick(math.gcd(nrows, row0) if row0 else nrows, (1088, 1024, 512, 256, 128))
    tn = _pick(n, (512, 256, 128))
    blk0 = row0 // tm
    hspec = pl.BlockSpec((tm, tn), lambda i, j: (blk0 + i, j))
    return pl.pallas_call(
        functools.partial(_outproj_kernel, nmix=nmix, gw=gw), grid=(nrows // tm, n // tn),
        in_specs=[pl.BlockSpec((tm, gw), lambda i, j: (i, 0))] * nmix + [pl.BlockSpec((d, tn), lambda i, j: (0, j)),
                                                                          hspec],
        out_specs=hspec, out_shape=jax.ShapeDtypeStruct(h.shape, _F32),
        input_output_aliases={nmix + 1: 0},
        compiler_params=_cparams(("parallel", "arbitrary"), nmix * tm * gw * 2, d * tn * 2, 2 * tm * tn * 4),
        name="out_proj",
    )(*ys, w, h)


def _swiglu_kernel(x_ref, wg_ref, wu_ref, o_ref):
    x = x_ref[...]
    g = jnp.dot(x, wg_ref[...], preferred_element_type=_F32)
    u = jnp.dot(x, wu_ref[...], preferred_element_type=_F32)
    o_ref[...] = (g * jax.nn.sigmoid(g) * u).astype(o_ref.dtype)


def _swiglu(x, wg, wu):
    m, k = x.shape
    n = wg.shape[1]
    tm = _pick(m, (1024, 512, 256, 128))
    tn = _pick(n, (256, 128))
    return pl.pallas_call(
        _swiglu_kernel, grid=(m // tm, n // tn),
        in_specs=[pl.BlockSpec((tm, k), lambda i, j: (i, 0)), pl.BlockSpec((k, tn), lambda i, j: (0, j)),
                  pl.BlockSpec((k, tn), lambda i, j: (0, j))],
        out_specs=pl.BlockSpec((tm, tn), lambda i, j: (i, j)),
        out_shape=jax.ShapeDtypeStruct((m, n), _BF16),
        compiler_params=_cparams(("parallel", "arbitrary"), tm * k * 2, 2 * k * tn * 2, tm * tn * 2),
        name="swiglu",
    )(x, wg, wu)


_NT = (((1,), (1,)), ((), ()))
_TN = (((0,), (0,)), ((), ()))


def _silu(x):
    return x * jax.nn.sigmoid(x)


def _softplus(x):
    return jnp.maximum(x, 0.0) + jnp.log(1.0 + jnp.exp(-jnp.abs(x)))


def _split3(x):
    hi = x.astype(_BF16)
    r = x - hi.astype(_F32)
    mid = r.astype(_BF16)
    lo = (r - mid.astype(_F32)).astype(_BF16)
    return hi, mid, lo


def _dot_sel(sel, x, dims):
    parts = _split3(x) if _BF16 == jnp.bfloat16 else (x,)
    out = None
    for p in parts:
        t = lax.dot_general(sel.astype(p.dtype), p, dims, preferred_element_type=_F32)
        out = t if out is None else out + t
    return out


def _dot_sel_r(x, sel, dims):
    parts = _split3(x) if _BF16 == jnp.bfloat16 else (x,)
    out = None
    for p in parts:
        t = lax.dot_general(p, sel.astype(p.dtype), dims, preferred_element_type=_F32)
        out = t if out is None else out + t
    return out


def _full(shape):
    nd = len(shape)
    return pl.BlockSpec(shape, lambda *_: (0,) * nd)


def _ssd_kernel(p_ref, h0_ref, cw_ref, cb_ref, dtb_ref, alog_ref, dsk_ref, nw_ref, y_ref, ho_ref, hst, xbuf, *,
                T, npad, nch, gw, xbc_w, heads, P, N, K):
    c = pl.program_id(1)
    G = SSD_GROUPS
    hpg = heads // G

    @pl.when(c == 0)
    def _init():
        hst[...] = h0_ref[0]
        xbuf[0:8, :] = jnp.zeros((8, xbc_w), _F32)

    xbuf[8:8 + T, :] = p_ref[:, gw:gw + xbc_w]
    conv = cb_ref[...]
    for j in range(K):
        conv = conv + cw_ref[j:j + 1, :] * xbuf[8 - (K - 1) + j:8 - (K - 1) + j + T, :]
    hist = xbuf[8 + T - (K - 1):8 + T, :]
    xbuf[8 - (K - 1):8, :] = hist
    xbc = _silu(conv)
    xs = xbc[:, :gw]
    bm = xbc[:, gw:gw + G * N]
    cm = xbc[:, gw + G * N:gw + 2 * G * N]
    z = p_ref[:, 0:gw]

    dt = _softplus(p_ref[:, gw + xbc_w:gw + xbc_w + LANE] + dtb_ref[...])
    if npad:
        rows = lax.broadcasted_iota(jnp.int32, (T, 1), 0)
        dt = jnp.where(jnp.logical_and(c == 0, rows < npad), 0.0, dt)
    la = dt * (-jnp.exp(alog_ref[...]))
    ri = lax.broadcasted_iota(jnp.int32, (T, T), 0)
    ci = lax.broadcasted_iota(jnp.int32, (T, T), 1)
    causal = ri >= ci
    cum = _dot_sel(causal.astype(_F32), la, (((1,), (0,)), ((), ())))
    hq = lax.broadcasted_iota(jnp.int32, (heads * T, LANE), 0) // T
    ln = lax.broadcasted_iota(jnp.int32, (heads * T, LANE), 1)
    rowb = _dot_sel((ln == hq).astype(_F32), cum, _NT)
    ecum = jnp.exp(cum)
    clast = cum[T - 1:T, :]
    cdec = jnp.exp(clast)
    dte = jnp.exp(clast - cum) * dt

    ys = []
    for g in range(G):
        bm_g = bm[:, g * N:(g + 1) * N].astype(_BF16)
        cm_g = cm[:, g * N:(g + 1) * N].astype(_BF16)
        cb = lax.dot_general(cm_g, bm_g, _NT, preferred_element_type=_F32)
        for hh in range(hpg):
            h = g * hpg + hh
            seg = cum[:, h:h + 1] - rowb[h * T:(h + 1) * T, :]
            lm = jnp.exp(jnp.where(causal, seg, -jnp.inf))
            xh = xs[:, h * P:(h + 1) * P]
            y_diag = jnp.dot((cb * lm).astype(_BF16), (xh * dt[:, h:h + 1]).astype(_BF16),
                             preferred_element_type=_F32)
            hprev = hst[h]
            y_off = lax.dot_general(cm_g, hprev.astype(_BF16), _NT, preferred_element_type=_F32) * ecum[:, h:h + 1]
            xw = (xh * dte[:, h:h + 1]).astype(_BF16)
            hst[h] = hprev * cdec[:, h:h + 1] + lax.dot_general(xw, bm_g, _TN, preferred_element_type=_F32)
            ys.append(y_diag + y_off + xh * dsk_ref[:, h:h + 1])
    y = jnp.concatenate(ys, axis=-1) * _silu(z)
    y = y * lax.rsqrt(jnp.mean(y * y, axis=-1, keepdims=True) + EPS) * nw_ref[...]
    y_ref[...] = y.astype(y_ref.dtype)

    @pl.when(c == nch - 1)
    def _fin():
        ho_ref[0] = hst[...]


def _ssd_mixer(p, h0, conv_w, conv_b, dt_bias, a_log, d_skip, norm_w, *, nseq, nch, T, npad, gw):
    heads, P, N = h0.shape[1:]
    K, xbc_w = conv_w.shape
    wp = p.shape[1]
    padl = lambda v: jnp.pad(v.astype(_F32), (0, LANE - v.shape[0])).reshape(1, LANE)
    kern = functools.partial(_ssd_kernel, T=T, npad=npad, nch=nch, gw=gw, xbc_w=xbc_w, heads=heads, P=P, N=N, K=K)
    return pl.pallas_call(
        kern, grid=(nseq, nch),
        in_specs=[pl.BlockSpec((T, wp), lambda b, c: (b * nch + c, 0)),
                  pl.BlockSpec((1, heads, P, N), lambda b, c: (b, 0, 0, 0)),
                  _full((K, xbc_w)), _full((1, xbc_w)), _full((1, LANE)), _full((1, LANE)), _full((1, LANE)),
                  _full((1, gw))],
        out_specs=[pl.BlockSpec((T, gw), lambda b, c: (b * nch + c, 0)),
                   pl.BlockSpec((1, heads, P, N), lambda b, c: (b, 0, 0, 0))],
        out_shape=[jax.ShapeDtypeStruct((nseq * nch * T, gw), _BF16),
                   jax.ShapeDtypeStruct((nseq, heads, P, N), _F32)],
        scratch_shapes=[pltpu.VMEM((heads, P, N), _F32), pltpu.VMEM((T + 8, xbc_w), _F32)],
        compiler_params=_cparams(("parallel", "arbitrary"), T * wp * 4, 3 * heads * P * N * 4, T * gw * 2,
                                 (T + 8) * xbc_w * 4, 24 * T * max(T, LANE) * 4),
        name="ssd_mixer",
    )(p, h0, conv_w.astype(_F32), conv_b.reshape(1, xbc_w).astype(_F32), padl(dt_bias), padl(a_log), padl(d_skip),
      norm_w.reshape(1, gw).astype(_F32))


def _ret_kernel(p_ref, cc_ref, ss_ref, r0_ref, gw_ref, gb_ref, y_ref, ro_ref, rst, *, T, npad, nch, gw, heads, dk, dv):
    c = pl.program_id(1)
    qkw = heads * dk

    @pl.when(c == 0)
    def _init():
        rst[...] = r0_ref[0]

    npc = jnp.where(c == 0, npad, 0).astype(_F32)
    ri = lax.broadcasted_iota(jnp.int32, (T, T), 0)
    ci = lax.broadcasted_iota(jnp.int32, (T, T), 1)
    causal = ri >= ci
    dlt = (ri - ci).astype(_F32)
    idx = lax.broadcasted_iota(jnp.int32, (T, 1), 0).astype(_F32)
    cc = cc_ref[...]
    ss = ss_ref[...]
    ys = []
    for h in range(heads):
        lg = math.log(1.0 - 2.0 ** (-5.0 - h))
        qh = p_ref[:, h * dk:(h + 1) * dk]
        kh = p_ref[:, qkw + h * dk:qkw + (h + 1) * dk]
        vh = p_ref[:, 2 * qkw + h * dv:2 * qkw + (h + 1) * dv].astype(_BF16)
        qh = (qh * cc + pltpu.roll(qh, dk // 2, 1) * ss)
        kh = (kh * cc + pltpu.roll(kh, dk // 2, 1) * ss) * (dk ** -0.5)
        qb = qh.astype(_BF16)
        dmat = jnp.exp(jnp.where(causal, dlt * lg, -jnp.inf))
        inner = lax.dot_general(qb, kh.astype(_BF16), _NT, preferred_element_type=_F32) * dmat
        y_in = jnp.dot(inner.astype(_BF16), vh, preferred_element_type=_F32)
        rprev = rst[h]
        y_x = jnp.dot(qb, rprev.astype(_BF16), preferred_element_type=_F32) * jnp.exp((idx + 1.0 - npc) * lg)
        kw = (kh * jnp.exp((T - 1.0 - idx) * lg)).astype(_BF16)
        rst[h] = rprev * jnp.exp((T - npc) * lg) + lax.dot_general(kw, vh, _TN, preferred_element_type=_F32)
        o = y_in + y_x
        oc = o - jnp.mean(o, axis=-1, keepdims=True)
        ys.append(oc * lax.rsqrt(jnp.mean(oc * oc, axis=-1, keepdims=True) + RET_GN_EPS))
    gate = p_ref[:, 2 * qkw + gw:2 * qkw + 2 * gw]
    y = _silu(gate) * (jnp.concatenate(ys, axis=-1) * gw_ref[...] + gb_ref[...])
    y_ref[...] = y.astype(y_ref.dtype)

    @pl.when(c == nch - 1)
    def _fin():
        ro_ref[0] = rst[...]


def _ret_mixer(p, r0, gn_w, gn_b, pos0, *, nseq, nch, T, npad, gw):
    heads, dk, dv = r0.shape[1:]
    wp = p.shape[1]
    theta = 1.0 / (ROPE_BASE ** jnp.linspace(0.0, 1.0, dk // 2, dtype=_F32))
    ang = (pos0 + jnp.arange(nch * T) - npad).astype(_F32)[:, None] * theta
    cos, sin = jnp.cos(ang), jnp.sin(ang)
    cc = jnp.concatenate([cos, cos], axis=-1)
    ss = jnp.concatenate([-sin, sin], axis=-1)
    kern = functools.partial(_ret_kernel, T=T, npad=npad, nch=nch, gw=gw, heads=heads, dk=dk, dv=dv)
    return pl.pallas_call(
        kern, grid=(nseq, nch),
        in_specs=[pl.BlockSpec((T, wp), lambda b, c: (b * nch + c, 0)),
                  pl.BlockSpec((T, dk), lambda b, c: (c, 0)), pl.BlockSpec((T, dk), lambda b, c: (c, 0)),
                  pl.BlockSpec((1, heads, dk, dv), lambda b, c: (b, 0, 0, 0)),
                  _full((1, gw)), _full((1, gw))],
        out_specs=[pl.BlockSpec((T, gw), lambda b, c: (b * nch + c, 0)),
                   pl.BlockSpec((1, heads, dk, dv), lambda b, c: (b, 0, 0, 0))],
        out_shape=[jax.ShapeDtypeStruct((nseq * nch * T, gw), _BF16),
                   jax.ShapeDtypeStruct((nseq, heads, dk, dv), _F32)],
        scratch_shapes=[pltpu.VMEM((heads, dk, dv), _F32)],
        compiler_params=_cparams(("parallel", "arbitrary"), T * wp * 4, 3 * heads * dk * dv * 4, T * gw * 2,
                                 16 * T * max(T, dv) * 4),
        name="ret_mixer",
    )(p, cc, ss, r0, gn_w.reshape(1, gw).astype(_F32), gn_b.reshape(1, gw).astype(_F32))


S5_CB = 128


def _gelu_tanh(x):
    return 0.5 * x * (1.0 + jnp.tanh(math.sqrt(2.0 / math.pi) * (x + 0.044715 * (x * x * x))))


def _s5_kernel(*refs, TC, nb, npad, nch, gw, sb, perm):
    nu = nb if perm else 1
    u_refs = refs[:nu]
    (s0r_ref, s0i_ref, are_ref, aim_ref, ldt_ref, wbr_ref, wbi_ref, wcr_ref, wci_ref, d_ref, gluw_ref, glub_ref,
     nw_ref, y_ref, sor_ref, soi_ref, xr, xi, str_, sti) = refs[nu:]
    c = pl.program_id(1)
    nblk = gw // S5_CB
    R = TC * nb

    @pl.when(c == 0)
    def _init():
        str_[...] = s0r_ref[...]
        sti[...] = s0i_ref[...]

    dt = jnp.exp(ldt_ref[...])
    lr, li = are_ref[...], aim_ref[...]
    mag = jnp.exp(lr * dt)
    abr, abi = mag * jnp.cos(li * dt), mag * jnp.sin(li * dt)
    den = lr * lr + li * li
    nr, ni = abr - 1.0, abi
    er, ei = (nr * lr + ni * li) / den, (ni * lr - nr * li) / den

    if perm:
        ti = lax.broadcasted_iota(jnp.int32, (R, R), 0)
        bi_ = lax.broadcasted_iota(jnp.int32, (R, R), 1)
        pmat = jnp.logical_and(ti // nb == bi_ % TC, ti % nb == bi_ // TC).astype(_F32)
        u = _dot_sel(pmat, jnp.concatenate([r[...] for r in u_refs], axis=0), (((1,), (0,)), ((), ())))
    else:
        u = u_refs[0][...]
    ub = u.astype(_BF16)
    for g in range(nblk):
        us = ub[:, g * S5_CB:(g + 1) * S5_CB]
        br = jnp.dot(us, wbr_ref[g], preferred_element_type=_F32)
        bi = jnp.dot(us, wbi_ref[g], preferred_element_type=_F32)
        e_r, e_i = er[:, g * sb:(g + 1) * sb], ei[:, g * sb:(g + 1) * sb]
        xr[:, g * sb:(g + 1) * sb] = e_r * br - e_i * bi
        xi[:, g * sb:(g + 1) * sb] = e_r * bi + e_i * br

    rows = max(nb, 8)
    spi = rows // nb

    def body(i, carry):
        r0 = pl.multiple_of(i * rows, 8)
        b_r, b_i = xr[pl.ds(r0, rows), :], xi[pl.ds(r0, rows), :]
        s_r, s_i = str_[...], sti[...]
        outs_r, outs_i = [], []
        for j in range(spi):
            n_r = abr * s_r - abi * s_i + b_r[j * nb:(j + 1) * nb]
            n_i = abr * s_i + abi * s_r + b_i[j * nb:(j + 1) * nb]
            s_r, s_i = n_r, n_i
            outs_r.append(n_r)
            outs_i.append(n_i)
        xr[pl.ds(r0, rows), :] = outs_r[0] if spi == 1 else jnp.concatenate(outs_r, axis=0)
        xi[pl.ds(r0, rows), :] = outs_i[0] if spi == 1 else jnp.concatenate(outs_i, axis=0)
        str_[...] = s_r
        sti[...] = s_i
        return carry

    start = jnp.clip(npad - c * TC, 0, TC) // spi
    lax.fori_loop(start, TC // spi, body, 0)

    ys = []
    for g in range(nblk):
        xrb = xr[:, g * sb:(g + 1) * sb].astype(_BF16)
        xib = xi[:, g * sb:(g + 1) * sb].astype(_BF16)
        ys.append(jnp.dot(xrb, wcr_ref[g], preferred_element_type=_F32)
                  - jnp.dot(xib, wci_ref[g], preferred_element_type=_F32))
    y = jnp.concatenate(ys, axis=-1) + d_ref[...] * u
    gy = _gelu_tanh(y)
    out = gy * jax.nn.sigmoid(jnp.dot(gy.astype(_BF16), gluw_ref[...], preferred_element_type=_F32) + glub_ref[...])
    out = (out * lax.rsqrt(jnp.mean(out * out, axis=-1, keepdims=True) + EPS) * nw_ref[...]).astype(y_ref.dtype)
    if perm:
        out = lax.dot_general(pmat.astype(out.dtype), out, _TN, preferred_element_type=_F32).astype(y_ref.dtype)
        for b in range(nb):
            y_ref[b] = out[b * TC:(b + 1) * TC]
    else:
        y_ref[...] = out

    @pl.when(c == nch - 1)
    def _fin():
        sor_ref[...] = str_[...]
        soi_ref[...] = sti[...]


def _s5_mixer(u, s0_re, s0_im, a_re, a_im, log_dt, b_re, b_im, c_re, c_im, d, glu_w, glu_b, norm_w, *,
              nseq, nch, T, npad, nb, TC, perm):
    groups, ns = a_re.shape
    gw = groups * S5_GROUP
    gpb = S5_CB // S5_GROUP
    nblk = gw // S5_CB
    sb = gpb * ns
    nst = groups * ns
    nsb = nseq // nb
    ncc = nch * T // TC
    assert npad % max(1, 8 // nb) == 0
    R = TC * nb
    if perm:
        assert nsb == 1
        u_args = [u] * nb
        u_specs = [pl.BlockSpec((TC, gw), functools.partial(lambda s, c, b: (b * ncc + c, 0), b=b)) for b in range(nb)]
        y_spec = pl.BlockSpec((nb, TC, gw), lambda s, c: (0, c, 0))
        y_shape = jax.ShapeDtypeStruct((nb, ncc * TC, gw), _BF16)
    else:
        u_args = [u[:, :gw].reshape(nsb, nb, ncc, TC, gw).transpose(0, 2, 3, 1, 4).reshape(nseq * nch * T, gw)]
        u_specs = [pl.BlockSpec((R, gw), lambda s, c: (s * ncc + c, 0))]
        y_spec = pl.BlockSpec((R, gw), lambda s, c: (s * ncc + c, 0))
        y_shape = jax.ShapeDtypeStruct((nseq * nch * T, gw), _BF16)
    eye = jnp.eye(gpb, dtype=_F32)

    def bd_in(w):
        w4 = w.reshape(nblk, gpb, ns, S5_GROUP)
        return jnp.einsum('bgnc,gh->bgchn', w4, eye).reshape(nblk, S5_CB, sb).astype(_BF16)

    def bd_out(w):
        w4 = w.reshape(nblk, gpb, S5_GROUP, ns)
        return jnp.einsum('bgcn,gh->bgnhc', w4, eye).reshape(nblk, sb, S5_CB).astype(_BF16)

    row = lambda v: v.reshape(1, -1).astype(_F32)
    kern = functools.partial(_s5_kernel, TC=TC, nb=nb, npad=npad, nch=ncc, gw=gw, sb=sb, perm=perm)
    y, so_r, so_i = pl.pallas_call(
        kern, grid=(nsb, ncc),
        in_specs=u_specs + [
            pl.BlockSpec((nb, nst), lambda s, c: (s, 0)), pl.BlockSpec((nb, nst), lambda s, c: (s, 0)),
            _full((1, nst)), _full((1, nst)), _full((1, nst)),
            _full((nblk, S5_CB, sb)), _full((nblk, S5_CB, sb)), _full((nblk, sb, S5_CB)),
            _full((nblk, sb, S5_CB)), _full((1, gw)), _full((gw, gw)), _full((1, gw)), _full((1, gw))],
        out_specs=[y_spec, pl.BlockSpec((nb, nst), lambda s, c: (s, 0)), pl.BlockSpec((nb, nst), lambda s, c: (s, 0))],
        out_shape=[y_shape, jax.ShapeDtypeStruct((nseq, nst), _F32), jax.ShapeDtypeStruct((nseq, nst), _F32)],
        scratch_shapes=[pltpu.VMEM((R, nst), _F32), pltpu.VMEM((R, nst), _F32),
                        pltpu.VMEM((nb, nst), _F32), pltpu.VMEM((nb, nst), _F32)],
        compiler_params=_cparams(("parallel", "arbitrary"), 2 * R * gw * 4, R * gw * 2, 4 * nblk * S5_CB * sb * 2,
                                 gw * gw * 2, R * nst * 4, 6 * max(nb, 8) * nst * 4),
        name="s5_mixer",
    )(*u_args, s0_re.reshape(nseq, nst), s0_im.reshape(nseq, nst), row(a_re), row(a_im),
      row(jnp.broadcast_to(log_dt[:, None], (groups, ns))), bd_in(b_re), bd_in(b_im), bd_out(c_re), bd_out(c_im),
      row(d), glu_w.astype(_BF16), row(glu_b), row(norm_w))
    if perm:
        y = y.reshape(nseq * nch * T, gw)
    else:
        y = y.reshape(nsb, ncc, TC, nb, gw).transpose(0, 3, 1, 2, 4).reshape(nseq * nch * T, gw)
    return y, so_r.reshape(nseq, groups, ns), so_i.reshape(nseq, groups, ns)


def _head_sums(x):
    ri = lax.broadcasted_iota(jnp.int32, (LANE, LANE), 0) // RWKV_HEAD_DIM
    ci = lax.broadcasted_iota(jnp.int32, (LANE, LANE), 1) // RWKV_HEAD_DIM
    e = (ri == ci).astype(_F32)
    nn = (((1,), (0,)), ((), ()))
    return jnp.concatenate([_dot_sel_r(x[:, j:j + LANE], e, nn) for j in range(0, x.shape[1], LANE)], axis=-1)


def _rwkv_pre_kernel(p_ref, mu_ref, w0_ref, w2_ref, a0_ref, a2_ref, g2_ref, kk_ref, ka_ref, rk_ref,
                     r_o, k_o, v_o, w_o, kk_o, b_o, g_o, bon_o, xbuf, *, T, gw, dl, da, dg):
    c = pl.program_id(1)

    @pl.when(c == 0)
    def _init():
        xbuf[0:8, :] = jnp.zeros((8, xbuf.shape[1]), _F32)

    p = p_ref[...]
    xbuf[8:8 + T, :] = p
    prev = xbuf[7:7 + T, :]
    last = xbuf[7 + T:8 + T, :]
    xbuf[7:8, :] = last
    pm = p + (prev - p) * mu_ref[...]
    r, k, v = pm[:, :gw], pm[:, gw:2 * gw], pm[:, 2 * gw:3 * gw]
    o1 = 3 * gw
    wl, al, gl = pm[:, o1:o1 + dl], pm[:, o1 + dl:o1 + dl + da], pm[:, o1 + dl + da:o1 + dl + da + dg]
    wx = w0_ref[...] + jnp.dot(jnp.tanh(wl).astype(_BF16), w2_ref[...], preferred_element_type=_F32)
    decay = jnp.exp(-jnp.exp(-_softplus(-wx) - 0.5))
    a = jax.nn.sigmoid(a0_ref[...] + jnp.dot(al.astype(_BF16), a2_ref[...], preferred_element_type=_F32))
    g = jnp.dot(jax.nn.sigmoid(gl).astype(_BF16), g2_ref[...], preferred_element_type=_F32)
    kkf = k * kk_ref[...]
    kk = kkf / jnp.maximum(jnp.sqrt(_head_sums(kkf * kkf)), 1e-12)
    k2 = k * (1.0 + (a - 1.0) * ka_ref[...])
    r_o[...] = r
    k_o[...] = k2
    v_o[...] = v
    w_o[...] = decay
    kk_o[...] = kk
    b_o[...] = kk * a
    g_o[...] = g
    bon_o[...] = _head_sums(r * k2 * rk_ref[...]) * v


def _rwkv_scan_kernel(r_ref, k_ref, v_ref, w_ref, kk_ref, b_ref, s0_ref, o_ref, so_ref, st, *, TC, K, npad, nch):
    c = pl.program_id(1)

    @pl.when(c == 0)
    def _init():
        st[...] = s0_ref[...]

    start = jnp.clip(npad - c * TC, 0, TC)

    @pl.when(start > 0)
    def _zero():
        o_ref[...] = jnp.zeros(o_ref.shape, _F32)

    vecs = (kk_ref, w_ref, b_ref, k_ref, r_ref)
    NACC = 4

    def tree(acc):
        return (acc[0] + acc[1]) + (acc[2] + acc[3])

    def madd(acc, k, x):
        acc[k % NACC] = x if acc[k % NACC] is None else acc[k % NACC] + x

    def sa_of(t):
        acc = [None] * NACC
        for k in range(K):
            madd(acc, k, st[k] * vecs[0][k, pl.ds(t, 1), :])
        return tree(acc)

    def body(t, sa):
        row = lambda i, k: vecs[i][k, pl.ds(t, 1), :]
        tn = jnp.minimum(t + 1, TC - 1)
        v_t = v_ref[t]
        acc_o, acc_s = [None] * NACC, [None] * NACC
        for k in range(K):
            s = st[k] * row(1, k) - sa * row(2, k) + v_t * row(3, k)
            st[k] = s
            madd(acc_o, k, s * row(4, k))
            madd(acc_s, k, s * vecs[0][k, pl.ds(tn, 1), :])
        o_ref[t] = tree(acc_o)
        return tree(acc_s)

    lax.fori_loop(start, TC, body, sa_of(jnp.minimum(start, TC - 1)))

    @pl.when(c == nch - 1)
    def _fin():
        so_ref[...] = st[...]


def _rwkv_relayout_kernel(*refs, nseq, H, K, J):
    x_refs, o_ref, zs = refs[:nseq], refs[nseq], refs[nseq + 1]
    gw = H * K
    for n in range(nseq):
        x = x_refs[n][...]
        for c0 in range(0, gw, LANE):
            zs[n * gw + c0:n * gw + c0 + LANE, :] = x[:, c0:c0 + LANE].T
    for k in range(K):
        q = zs[pl.ds(k, nseq * H, stride=K), :]
        o_ref[k] = jnp.concatenate([q] * J, axis=0).T


def _rwkv_relayout(x, *, nseq, nch, H, K, J):
    gw = H * K
    nl = J * nseq * H
    assert nl == LANE
    specs = [pl.BlockSpec((CHUNK, gw), functools.partial(lambda c, n: (n * nch + c, 0), n=n)) for n in range(nseq)]
    return pl.pallas_call(
        functools.partial(_rwkv_relayout_kernel, nseq=nseq, H=H, K=K, J=J), grid=(nch,),
        in_specs=specs, out_specs=pl.BlockSpec((K, CHUNK, nl), lambda c: (0, c, 0)),
        out_shape=jax.ShapeDtypeStruct((K, nch * CHUNK, nl), _F32),
        scratch_shapes=[pltpu.VMEM((nseq * gw, CHUNK), _F32)],
        compiler_params=_cparams(("parallel",), nseq * CHUNK * gw * 4, K * CHUNK * nl * 4, nseq * gw * CHUNK * 4),
        name="rwkv_relayout",
    )(*([x] * nseq))


def _rwkv_post_kernel(o_ref, bon_ref, g_ref, lw_ref, lb_ref, y_ref):
    o = o_ref[...]
    oc = o - _head_sums(o) * (1.0 / RWKV_HEAD_DIM)
    var = _head_sums(oc * oc) * (1.0 / RWKV_HEAD_DIM)
    y = (oc * lax.rsqrt(var + RWKV_GN_EPS) * lw_ref[...] + lb_ref[...] + bon_ref[...]) * g_ref[...]
    y_ref[...] = y.astype(y_ref.dtype)


def _rwkv_mixer(p, s0, mu_p, w0, w2_p, a0, a2_p, g2_p, k_k, k_a, r_k, ln_w, ln_b, *, nseq, nch, T, npad, gw,
                dl, da, dg, J, TC):
    H, V, K = s0.shape[1:]
    wp = p.shape[1]
    rows = nseq * nch * T
    L = nch * T
    row = lambda v: v.reshape(1, -1).astype(_F32)
    f32rows = jax.ShapeDtypeStruct((rows, gw), _F32)
    blk = pl.BlockSpec((T, gw), lambda b, c: (b * nch + c, 0))
    pre = pl.pallas_call(
        functools.partial(_rwkv_pre_kernel, T=T, gw=gw, dl=dl, da=da, dg=dg), grid=(nseq, nch),
        in_specs=[pl.BlockSpec((T, wp), lambda b, c: (b * nch + c, 0)), _full((1, wp)), _full((1, gw)),
                  _full((dl, gw)), _full((1, gw)), _full((da, gw)), _full((dg, gw)), _full((1, gw)), _full((1, gw)),
                  _full((1, gw))],
        out_specs=[blk] * 8, out_shape=[f32rows] * 8,
        scratch_shapes=[pltpu.VMEM((T + 8, wp), _F32)],
        compiler_params=_cparams(("parallel", "arbitrary"), 2 * T * wp * 4, 8 * T * gw * 4, 12 * T * gw * 4),
        name="rwkv_pre",
    )(p, row(mu_p), row(w0), w2_p.astype(_BF16), row(a0), a2_p.astype(_BF16), g2_p.astype(_BF16), row(k_k),
      row(k_a), row(r_k))
    r, k2, v, decay, kk, bvec, g, bonus = pre

    VI = V // J
    NL = J * nseq * H
    assert NL % LANE == 0 or NL < LANE

    def kvec(x):
        if T == CHUNK and NL == LANE:
            return _rwkv_relayout(x, nseq=nseq, nch=nch, H=H, K=K, J=J)
        y = x.reshape(nseq, L, H, K).transpose(3, 1, 0, 2).reshape(K, L, 1, nseq * H)
        return jnp.broadcast_to(y, (K, L, J, nseq * H)).reshape(K, L, NL)

    vv = v.reshape(nseq, L, H, J, VI).transpose(1, 4, 3, 0, 2).reshape(L, VI, NL)
    st0 = s0.reshape(nseq, H, J, VI, K).transpose(4, 3, 2, 0, 1).reshape(K, VI, NL)
    lb = min(NL, LANE)
    ncc = L // TC
    kspec = pl.BlockSpec((K, TC, lb), lambda n, c: (0, c, n))
    vspec = pl.BlockSpec((TC, VI, lb), lambda n, c: (c, 0, n))
    sspec = pl.BlockSpec((K, VI, lb), lambda n, c: (0, 0, n))
    o, st1 = pl.pallas_call(
        functools.partial(_rwkv_scan_kernel, TC=TC, K=K, npad=npad, nch=ncc), grid=(NL // lb, ncc),
        in_specs=[kspec, kspec, vspec, kspec, kspec, kspec, sspec],
        out_specs=[vspec, sspec],
        out_shape=[jax.ShapeDtypeStruct((L, VI, NL), _F32), jax.ShapeDtypeStruct((K, VI, NL), _F32)],
        scratch_shapes=[pltpu.VMEM((K, VI, lb), _F32)],
        compiler_params=_cparams(("parallel", "arbitrary"), 5 * TC * K * lb * 4, 2 * TC * VI * lb * 4,
                                 3 * VI * K * lb * 4),
        name="rwkv_scan",
    )(kvec(r), kvec(k2), vv, kvec(decay), kvec(kk), kvec(bvec), st0)
    o = o.reshape(L, VI, J, nseq, H).transpose(3, 0, 4, 2, 1).reshape(rows, gw)
    s_new = st1.reshape(K, VI, J, nseq, H).transpose(3, 4, 2, 1, 0).reshape(nseq, H, V, K)

    tr = _pick(rows, (256, 128, 64, 32, 16, 8))
    rblk = pl.BlockSpec((tr, gw), lambda i: (i, 0))
    y = pl.pallas_call(
        _rwkv_post_kernel, grid=(rows // tr,),
        in_specs=[rblk, rblk, rblk, _full((1, gw)), _full((1, gw))],
        out_specs=rblk, out_shape=jax.ShapeDtypeStruct((rows, gw), _BF16),
        compiler_params=_cparams(("parallel",), 4 * tr * gw * 4, 8 * tr * gw * 4),
        name="rwkv_post",
    )(o, bonus, g, row(ln_w), row(ln_b))
    return y, s_new


def kernel(x_prompt, x_sample, state_ssd, state_ssd_conv, state_rwkv, state_rwkv_shift, state_ret, state_s5_re,
           state_s5_im, meta, ln_mix, w_in, ssd_conv_w, ssd_conv_b, ssd_dt_bias, ssd_a_log, ssd_d, ssd_norm, rwkv_mu,
           rwkv_w0, rwkv_w2, rwkv_a0, rwkv_a2, rwkv_g2, rwkv_k_k, rwkv_k_a, rwkv_r_k, rwkv_ln_w, rwkv_ln_b, ret_gn_w,
           ret_gn_b, s5_a_re, s5_a_im, s5_log_dt, s5_b_re, s5_b_im, s5_c_re, s5_c_im, s5_d, s5_glu_w, s5_glu_b,
           s5_norm, w_out, ln_ffn, w_gate, w_up, w_down, ln_f):
    bp, sp, d = x_prompt.shape
    bs, ls, _ = x_sample.shape
    depth = w_in.shape[0]
    gw = d // 4
    ssd_heads, ssd_p, ssd_n = state_ssd.shape[2:]
    xbc_w = state_ssd_conv.shape[-1]
    rwkv_proj = state_rwkv_shift.shape[-1]
    ret_heads, ret_dk, ret_dv = state_ret.shape[2:]
    qkw = ret_heads * ret_dk
    dff = w_gate.shape[-1]

    lreal = N_META + sp
    lp = _round_up(lreal, CHUNK)
    pad = lp - lreal
    mp, ms = bp * lp, bs * ls
    m = mp + ms

    head = jnp.concatenate([jnp.zeros((pad, d), _F32), meta.astype(_F32)], axis=0)
    h = jnp.concatenate([piece for b in range(bp) for piece in (head, x_prompt[b])] + [x_sample.reshape(ms, d)],
                        axis=0)

    in_splits = (gw, xbc_w, ssd_heads, rwkv_proj, qkw, qkw, gw, gw, gw)
    offs = [0]
    for s in in_splits:
        offs.append(offs[-1] + s)
    ntile = 512 if gw % 512 == 0 else LANE

    layers = range(depth)

    lora_c = (rwkv_w2.shape[1], rwkv_a2.shape[1], rwkv_g2.shape[1])
    dl, da, dg = (_round_up(n, LANE) for n in lora_c)
    rw_w = _round_up(3 * gw + dl + da + dg, ntile)

    def rw_pad(x):
        parts, o = [x[..., :3 * gw]], 3 * gw
        for n, npd in zip(lora_c, (dl, da, dg)):
            parts.append(jnp.pad(x[..., o:o + n], [(0, 0)] * (x.ndim - 1) + [(0, npd - n)]))
            o += n
        y = jnp.concatenate(parts, axis=-1)
        return jnp.pad(y, [(0, 0)] * (x.ndim - 1) + [(0, rw_w - y.shape[-1])])

    def rw_compact(x):
        parts, o = [x[..., :3 * gw]], 3 * gw
        for n, npd in zip(lora_c, (dl, da, dg)):
            parts.append(x[..., o:o + n])
            o += npd
        return jnp.concatenate(parts, axis=-1)

    assert (gw + xbc_w) % LANE == 0 and gw % LANE == 0
    o_rw = offs[3] + 3 * gw
    plans = (
        ((0, 0, gw + xbc_w), (gw + xbc_w, gw + xbc_w, ssd_heads)),
        ((0, offs[3], 3 * gw), (3 * gw, o_rw, lora_c[0]), (3 * gw + dl, o_rw + lora_c[0], lora_c[1]),
         (3 * gw + dl + da, o_rw + lora_c[0] + lora_c[1], lora_c[2])),
        ((0, offs[4], offs[8] - offs[4]),),
        ((0, offs[8], gw),),
    )
    widths = (_round_up(gw + xbc_w + LANE, ntile), rw_w, _round_up(offs[8] - offs[4], ntile), _round_up(gw, ntile))
    packed = [_pack_w_in(w_in, l, plans, widths) for l in layers]
    w_ssd_b, w_rw_b, w_ret_b, w_s5_b = ([p[i] for p in packed] for i in range(4))
    mu_p = rw_pad(rwkv_mu)
    padrows = lambda w, n: jnp.pad(w, ((0, 0), (0, n - w.shape[1]), (0, 0)))
    w2_p, a2_p, g2_p = padrows(rwkv_w2, dl), padrows(rwkv_a2, da), padrows(rwkv_g2, dg)
    w_out_b = [w_out[l].astype(_BF16) for l in layers]
    w_gate_b = [w_gate[l].astype(_BF16) for l in layers]
    w_up_b = [w_up[l].astype(_BF16) for l in layers]
    w_down_b = [w_down[l].astype(_BF16) for l in layers]
    half = dff // 2

    zeros_b = lambda s: jnp.zeros((bp,) + s.shape[2:], _F32)
    nch_p = lp // CHUNK
    ts = _round_up(ls + 3, 8)
    spad = ts - ls

    def sample_rows(rows, hist=None, c0=0):
        r3 = jnp.pad(rows.reshape(bs, ls, -1), ((0, 0), (spad, 0), (0, 0)))
        if hist is not None:
            k, w = hist.shape[1:]
            r3 = r3.at[:, spad - k:spad, c0:c0 + w].set(hist)
        return r3.reshape(bs * ts, -1)

    def unsample(y):
        return y.reshape(bs, ts, -1)[:, spad:].reshape(ms, -1)

    def last_rows(rows, k):
        return jnp.stack([rows[(b + 1) * lp - k:(b + 1) * lp] for b in range(bp)])

    outs_p = [[] for _ in range(7)]
    outs_s = [[] for _ in range(7)]
    for l in range(depth):
        hn = _rmsnorm(h, ln_mix[l], _BF16, lp=lp, pad=pad, mp=mp)

        pa = _mm(hn, w_ssd_b[l], name="in_proj_ssd")
        ssd_w = (ssd_conv_w[l], ssd_conv_b[l], ssd_dt_bias[l], ssd_a_log[l], ssd_d[l], ssd_norm[l])
        ya_p, hp_new = _ssd_mixer(pa, zeros_b(state_ssd), *ssd_w, nseq=bp, nch=nch_p, T=CHUNK, npad=pad, gw=gw)
        pa_s = sample_rows(pa[mp:], state_ssd_conv[l], gw)
        ya_s, hs_new = _ssd_mixer(pa_s, state_ssd[l], *ssd_w, nseq=bs, nch=1, T=ts, npad=spad, gw=gw)
        kc = ssd_conv_w.shape[1] - 1
        outs_p[0].append(hp_new)
        outs_p[1].append(last_rows(pa, kc)[..., gw:gw + xbc_w])
        outs_s[0].append(hs_new)
        outs_s[1].append(pa_s.reshape(bs, ts, -1)[:, ts - kc:, gw:gw + xbc_w])

        pb = _mm(hn, w_rw_b[l], name="in_proj_rwkv")
        rw_args = (mu_p[l], rwkv_w0[l], w2_p[l], rwkv_a0[l], a2_p[l], g2_p[l], rwkv_k_k[l], rwkv_k_a[l],
                   rwkv_r_k[l], rwkv_ln_w[l], rwkv_ln_b[l])
        rw_kw = dict(gw=gw, dl=dl, da=da, dg=dg)
        rw_heads = state_rwkv.shape[2]
        yb_p, sp_new = _rwkv_mixer(pb, zeros_b(state_rwkv), *rw_args, nseq=bp, nch=nch_p, T=CHUNK, npad=pad,
                                   J=max(1, LANE // (bp * rw_heads)), TC=CHUNK // 4, **rw_kw)
        pb_s = sample_rows(pb[mp:], rw_pad(state_rwkv_shift[l])[:, None, :], 0)
        yb_s, ss_new = _rwkv_mixer(pb_s, state_rwkv[l], *rw_args, nseq=bs, nch=1, T=ts, npad=spad,
                                   J=max(1, LANE // (bs * rw_heads)), TC=ts, **rw_kw)
        outs_p[2].append(sp_new)
        outs_p[3].append(rw_compact(last_rows(pb, 1)[:, 0]))
        outs_s[2].append(ss_new)
        outs_s[3].append(rw_compact(pb_s.reshape(bs, ts, -1)[:, -1]))

        pc = _mm(hn, w_ret_b[l], name="in_proj_ret")
        yc_p, rp_new = _ret_mixer(pc, zeros_b(state_ret), ret_gn_w[l], ret_gn_b[l], 0,
                                  nseq=bp, nch=nch_p, T=CHUNK, npad=pad, gw=gw)
        yc_s, rs_new = _ret_mixer(sample_rows(pc[mp:]), state_ret[l], ret_gn_w[l], ret_gn_b[l], PAST_LEN,
                                  nseq=bs, nch=1, T=ts, npad=spad, gw=gw)
        outs_p[4].append(rp_new)
        outs_s[4].append(rs_new)

        pd = _mm(hn, w_s5_b[l], name="in_proj_s5")
        s5_w = (s5_a_re[l], s5_a_im[l], s5_log_dt[l], s5_b_re[l], s5_b_im[l], s5_c_re[l], s5_c_im[l], s5_d[l],
                s5_glu_w[l], s5_glu_b[l], s5_norm[l])
        yd_p, s5r_p, s5i_p = _s5_mixer(pd, zeros_b(state_s5_re), zeros_b(state_s5_im), *s5_w, nseq=bp, nch=nch_p,
                                       T=CHUNK, npad=pad, nb=bp, TC=CHUNK // 2, perm=True)
        yd_s, s5r_s, s5i_s = _s5_mixer(sample_rows(pd[mp:]), state_s5_re[l], state_s5_im[l], *s5_w, nseq=bs, nch=1,
                                       T=ts, npad=spad, nb=min(bs, 32), TC=ts, perm=False)
        outs_p[5].append(s5r_p)
        outs_p[6].append(s5i_p)
        outs_s[5].append(s5r_s)
        outs_s[6].append(s5i_s)

        h = _outproj((ya_p, yb_p, yc_p, yd_p), w_out_b[l], h, 0)
        h = _outproj(tuple(unsample(y) for y in (ya_s, yb_s, yc_s, yd_s)), w_out_b[l], h, mp)
        hn = _rmsnorm(h, ln_ffn[l], _BF16)
        ff = _swiglu(hn, w_gate_b[l], w_up_b[l])
        h = _mm(ff, w_down_b[l], res=h, kb=0, tk=half, name="ffn_down0")
        h = _mm(ff, w_down_b[l], res=h, kb=1, tk=half, name="ffn_down1")

    y_prompt, y_sample = _final_norm(h, ln_f, bp=bp, sp=sp, lp=lp, ms=ms)
    return ((y_prompt.reshape(bp, sp, d), y_sample.reshape(bs, ls, d))
            + tuple(jnp.stack(a) for a in outs_p) + tuple(jnp.stack(a) for a in outs_s))
```

```python
import functools
import math

import jax
import jax.numpy as jnp
from jax import lax
from jax.experimental import pallas as pl
from jax.experimental.pallas import tpu as pltpu

N_META = 16
EPS = 1e-6
SSD_GROUPS = 2
RWKV_HEAD_DIM = 64
RWKV_GN_EPS = 64e-5
RET_GN_EPS = 1e-5
ROPE_BASE = 10000.0
S5_GROUP = 16
PAST_LEN = 16384
CHUNK = 128
LANE = 128
V7X_VMEM_CAP = 60 * 1024 * 1024

_F32 = jnp.float32
_BF16 = jnp.bfloat16


def _pick(n, cands):
    for c in cands:
        if n % c == 0:
            return c
    raise ValueError(f"no tile in {cands} divides {n}")


def _round_up(n, m):
    return -(-n // m) * m


def _cparams(sem, *block_bytes):
    need = 2 * sum(block_bytes) + (6 << 20)
    return pltpu.CompilerParams(dimension_semantics=sem, vmem_limit_bytes=int(min(max(need, 16 << 20), V7X_VMEM_CAP)))


def _rmsnorm_kernel(x_ref, g_ref, o_ref, *, tr, lp, pad, mp):
    x = x_ref[...]
    y = x * lax.rsqrt(jnp.mean(x * x, axis=-1, keepdims=True) + EPS) * g_ref[...]
    if pad:
        row0 = pl.program_id(0) * tr
        pos0 = lax.rem(row0, lp)
        rows = lax.broadcasted_iota(jnp.int32, (tr, 1), 0)
        is_pad = jnp.logical_and(row0 < mp, pos0 + rows < pad)
        y = jnp.where(is_pad, 0.0, y)
    o_ref[...] = y.astype(o_ref.dtype)


def _rmsnorm(x, g, out_dtype, lp=0, pad=0, mp=0):
    m, d = x.shape
    tr = _pick(math.gcd(m, lp) if pad else m, (256, 128, 64, 32, 16, 8))
    kern = functools.partial(_rmsnorm_kernel, tr=tr, lp=lp, pad=pad, mp=mp)
    return pl.pallas_call(
        kern, grid=(m // tr,),
        in_specs=[pl.BlockSpec((tr, d), lambda i: (i, 0)), pl.BlockSpec((1, d), lambda i: (0, 0))],
        out_specs=pl.BlockSpec((tr, d), lambda i: (i, 0)),
        out_shape=jax.ShapeDtypeStruct((m, d), out_dtype),
        compiler_params=_cparams(("parallel",), tr * d * 4, tr * d * 4),
        name="rmsnorm",
    )(x, g.reshape(1, d).astype(_F32))


def _final_norm(h, g, *, bp, sp, lp, ms):
    d = h.shape[1]
    assert lp - sp == CHUNK and sp % CHUNK == 0
    nch, nout = lp // CHUNK, sp // CHUNK
    kern = functools.partial(_rmsnorm_kernel, tr=CHUNK, lp=0, pad=0, mp=0)
    g2 = g.reshape(1, d).astype(_F32)
    yp = pl.pallas_call(
        kern, grid=(bp, nout),
        in_specs=[pl.BlockSpec((CHUNK, d), lambda b, j: (b * nch + 1 + j, 0)), pl.BlockSpec((1, d), lambda b, j: (0, 0))],
        out_specs=pl.BlockSpec((CHUNK, d), lambda b, j: (b * nout + j, 0)),
        out_shape=jax.ShapeDtypeStruct((bp * sp, d), _F32),
        compiler_params=_cparams(("parallel", "parallel"), CHUNK * d * 4, CHUNK * d * 4),
        name="final_norm_prompt",
    )(h, g2)
    tr = _pick(math.gcd(ms, bp * lp), (256, 128, 64, 32, 16, 8))
    off = bp * lp // tr
    ys = pl.pallas_call(
        functools.partial(_rmsnorm_kernel, tr=tr, lp=0, pad=0, mp=0), grid=(ms // tr,),
        in_specs=[pl.BlockSpec((tr, d), lambda i: (off + i, 0)), pl.BlockSpec((1, d), lambda i: (0, 0))],
        out_specs=pl.BlockSpec((tr, d), lambda i: (i, 0)),
        out_shape=jax.ShapeDtypeStruct((ms, d), _F32),
        compiler_params=_cparams(("parallel",), tr * d * 4, tr * d * 4),
        name="final_norm_sample",
    )(h, g2)
    return yp, ys


def _pack_kernel(w_ref, *o_refs, plans):
    x = w_ref[...]
    tr = x.shape[0]
    for o_ref, plan in zip(o_refs, plans):
        done = 0
        for dst, src, n in plan:
            npd = _round_up(n, LANE)
            assert dst % LANE == 0 and dst >= done
            if dst > done:
                o_ref[:, done:dst] = jnp.zeros((tr, dst - done), o_ref.dtype)
            seg = x[:, src:src + npd]
            if npd > n:
                seg = jnp.where(lax.broadcasted_iota(jnp.int32, (tr, npd), 1) < n, seg, 0.0)
            o_ref[:, dst:dst + npd] = seg.astype(o_ref.dtype)
            done = dst + npd
        if done < o_ref.shape[1]:
            o_ref[:, done:] = jnp.zeros((tr, o_ref.shape[1] - done), o_ref.dtype)


def _pack_w_in(w_in, l, plans, widths):
    _, d, nin = w_in.shape
    assert all(src + _round_up(n, LANE) <= nin for plan in plans for _, src, n in plan)
    tr = _pick(d, (128, 64, 32, 16, 8))
    return pl.pallas_call(
        functools.partial(_pack_kernel, plans=plans), grid=(d // tr,),
        in_specs=[pl.BlockSpec((tr, nin), lambda i: (l * (d // tr) + i, 0))],
        out_specs=[pl.BlockSpec((tr, w), lambda i: (i, 0)) for w in widths],
        out_shape=[jax.ShapeDtypeStruct((d, w), _BF16) for w in widths],
        compiler_params=_cparams(("parallel",), 2 * tr * nin * 4, tr * sum(widths) * 2),
        name="pack_w_in",
    )(w_in.reshape(-1, nin))


def _mm_kernel(*refs, has_res):
    if has_res:
        x_ref, w_ref, r_ref, o_ref = refs
    else:
        x_ref, w_ref, o_ref = refs
    acc = jnp.dot(x_ref[...], w_ref[...], preferred_element_type=_F32)
    if has_res:
        acc = acc + r_ref[...]
    o_ref[...] = acc.astype(o_ref.dtype)


def _wspec(w, wl, rows, cols, imap):
    if w.ndim == 2:
        return pl.BlockSpec((rows, cols), imap)
    return pl.BlockSpec((None, rows, cols), lambda i, j: (wl,) + imap(i, j))


def _mm(x, w, res=None, out_dtype=_F32, kb=0, tk=None, name="matmul", wl=None):
    m = x.shape[0]
    n = w.shape[-1]
    tk = tk or w.shape[-2]
    tm = _pick(m, (1024, 512, 256, 128))
    if tk > 4096:
        tm = _pick(m, (512, 256, 128))
    tn = _pick(n, (512, 256, 128))
    in_specs = [pl.BlockSpec((tm, tk), lambda i, j: (i, kb)), _wspec(w, wl, tk, tn, lambda i, j: (kb, j))]
    args = [x, w]
    blocks = [tm * tk * 2, tk * tn * 2, tm * tn * 4]
    if res is not None:
        in_specs.append(pl.BlockSpec((tm, tn), lambda i, j: (i, j)))
        args.append(res)
        blocks.append(tm * tn * 4)
    return pl.pallas_call(
        functools.partial(_mm_kernel, has_res=res is not None), grid=(m // tm, n // tn),
        in_specs=in_specs, out_specs=pl.BlockSpec((tm, tn), lambda i, j: (i, j)),
        out_shape=jax.ShapeDtypeStruct((m, n), out_dtype),
        compiler_params=_cparams(("parallel", "arbitrary"), *blocks),
        name=name,
    )(*args)


def _outproj_kernel(*refs, nmix, gw):
    xs = refs[:nmix]
    w_ref, r_ref, o_ref = refs[nmix:]
    acc = r_ref[...]
    for j, x in enumerate(xs):
        acc = acc + jnp.dot(x[...], w_ref[j * gw:(j + 1) * gw, :], preferred_element_type=_F32)
    o_ref[...] = acc


def _outproj(ys, w, h, row0, wl=None):
    nrows, gw = ys[0].shape
    nmix = len(ys)
    d, n = w.shape[-2:]
    tm = _pick(math.gcd(nrows, row0) if row0 else nrows, (1088, 1024, 512, 256, 128))
    tn = _pick(n, (512, 256, 128))
    blk0 = row0 // tm
    hspec = pl.BlockSpec((tm, tn), lambda i, j: (blk0 + i, j))
    return pl.pallas_call(
        functools.partial(_outproj_kernel, nmix=nmix, gw=gw), grid=(nrows // tm, n // tn),
        in_specs=[pl.BlockSpec((tm, gw), lambda i, j: (i, 0))] * nmix + [_wspec(w, wl, d, tn, lambda i, j: (0, j)),
                                                                          hspec],
        out_specs=hspec, out_shape=jax.ShapeDtypeStruct(h.shape, _F32),
        input_output_aliases={nmix + 1: 0},
        compiler_params=_cparams(("parallel", "arbitrary"), nmix * tm * gw * 2, d * tn * 2, 2 * tm * tn * 4),
        name="out_proj",
    )(*ys, w, h)


def _swiglu_kernel(x_ref, wg_ref, wu_ref, o_ref):
    x = x_ref[...]
    g = jnp.dot(x, wg_ref[...], preferred_element_type=_F32)
    u = jnp.dot(x, wu_ref[...], preferred_element_type=_F32)
    o_ref[...] = (g * jax.nn.sigmoid(g) * u).astype(o_ref.dtype)


def _swiglu(x, wg, wu, wl=None):
    m, k = x.shape
    n = wg.shape[-1]
    tm = _pick(m, (1024, 512, 256, 128))
    tn = _pick(n, (256, 128))
    return pl.pallas_call(
        _swiglu_kernel, grid=(m // tm, n // tn),
        in_specs=[pl.BlockSpec((tm, k), lambda i, j: (i, 0)), _wspec(wg, wl, k, tn, lambda i, j: (0, j)),
                  _wspec(wu, wl, k, tn, lambda i, j: (0, j))],
        out_specs=pl.BlockSpec((tm, tn), lambda i, j: (i, j)),
        out_shape=jax.ShapeDtypeStruct((m, n), _BF16),
        compiler_params=_cparams(("parallel", "arbitrary"), tm * k * 2, 2 * k * tn * 2, tm * tn * 2),
        name="swiglu",
    )(x, wg, wu)


_NT = (((1,), (1,)), ((), ()))
_TN = (((0,), (0,)), ((), ()))


def _silu(x):
    return x * jax.nn.sigmoid(x)


def _softplus(x):
    return jnp.maximum(x, 0.0) + jnp.log(1.0 + jnp.exp(-jnp.abs(x)))


def _split3(x):
    hi = x.astype(_BF16)
    r = x - hi.astype(_F32)
    mid = r.astype(_BF16)
    lo = (r - mid.astype(_F32)).astype(_BF16)
    return hi, mid, lo


def _dot_sel(sel, x, dims):
    parts = _split3(x) if _BF16 == jnp.bfloat16 else (x,)
    out = None
    for p in parts:
        t = lax.dot_general(sel.astype(p.dtype), p, dims, preferred_element_type=_F32)
        out = t if out is None else out + t
    return out


def _dot_sel_r(x, sel, dims):
    parts = _split3(x) if _BF16 == jnp.bfloat16 else (x,)
    out = None
    for p in parts:
        t = lax.dot_general(p, sel.astype(p.dtype), dims, preferred_element_type=_F32)
        out = t if out is None else out + t
    return out


def _full(shape):
    nd = len(shape)
    return pl.BlockSpec(shape, lambda *_: (0,) * nd)


def _ssd_kernel(p_ref, h0_ref, cw_ref, cb_ref, dtb_ref, alog_ref, dsk_ref, nw_ref, y_ref, ho_ref, hst, xbuf, *,
                T, npad, nch, gw, xbc_w, heads, P, N, K):
    c = pl.program_id(1)
    G = SSD_GROUPS
    hpg = heads // G

    @pl.when(c == 0)
    def _init():
        hst[...] = h0_ref[0]
        xbuf[0:8, :] = jnp.zeros((8, xbc_w), _F32)

    xbuf[8:8 + T, :] = p_ref[:, gw:gw + xbc_w]
    conv = cb_ref[...]
    for j in range(K):
        conv = conv + cw_ref[j:j + 1, :] * xbuf[8 - (K - 1) + j:8 - (K - 1) + j + T, :]
    hist = xbuf[8 + T - (K - 1):8 + T, :]
    xbuf[8 - (K - 1):8, :] = hist
    xbc = _silu(conv)
    xs = xbc[:, :gw]
    bm = xbc[:, gw:gw + G * N]
    cm = xbc[:, gw + G * N:gw + 2 * G * N]
    z = p_ref[:, 0:gw]

    dt = _softplus(p_ref[:, gw + xbc_w:gw + xbc_w + LANE] + dtb_ref[...])
    if npad:
        rows = lax.broadcasted_iota(jnp.int32, (T, 1), 0)
        dt = jnp.where(jnp.logical_and(c == 0, rows < npad), 0.0, dt)
    la = dt * (-jnp.exp(alog_ref[...]))
    ri = lax.broadcasted_iota(jnp.int32, (T, T), 0)
    ci = lax.broadcasted_iota(jnp.int32, (T, T), 1)
    causal = ri >= ci
    cum = _dot_sel(causal.astype(_F32), la, (((1,), (0,)), ((), ())))
    hq = lax.broadcasted_iota(jnp.int32, (heads * T, LANE), 0) // T
    ln = lax.broadcasted_iota(jnp.int32, (heads * T, LANE), 1)
    rowb = _dot_sel((ln == hq).astype(_F32), cum, _NT)
    ecum = jnp.exp(cum)
    clast = cum[T - 1:T, :]
    cdec = jnp.exp(clast)
    dte = jnp.exp(clast - cum) * dt

    ys = []
    for g in range(G):
        bm_g = bm[:, g * N:(g + 1) * N].astype(_BF16)
        cm_g = cm[:, g * N:(g + 1) * N].astype(_BF16)
        cb = lax.dot_general(cm_g, bm_g, _NT, preferred_element_type=_F32)
        for hh in range(hpg):
            h = g * hpg + hh
            seg = cum[:, h:h + 1] - rowb[h * T:(h + 1) * T, :]
            lm = jnp.exp(jnp.where(causal, seg, -jnp.inf))
            xh = xs[:, h * P:(h + 1) * P]
            y_diag = jnp.dot((cb * lm).astype(_BF16), (xh * dt[:, h:h + 1]).astype(_BF16),
                             preferred_element_type=_F32)
            hprev = hst[h]
            y_off = lax.dot_general(cm_g, hprev.astype(_BF16), _NT, preferred_element_type=_F32) * ecum[:, h:h + 1]
            xw = (xh * dte[:, h:h + 1]).astype(_BF16)
            hst[h] = hprev * cdec[:, h:h + 1] + lax.dot_general(xw, bm_g, _TN, preferred_element_type=_F32)
            ys.append(y_diag + y_off + xh * dsk_ref[:, h:h + 1])
    y = jnp.concatenate(ys, axis=-1) * _silu(z)
    y = y * lax.rsqrt(jnp.mean(y * y, axis=-1, keepdims=True) + EPS) * nw_ref[...]
    y_ref[...] = y.astype(y_ref.dtype)

    @pl.when(c == nch - 1)
    def _fin():
        ho_ref[0] = hst[...]


def _state_spec(s0, sl):
    tail = s0.shape[-3:]
    if sl is None:
        return pl.BlockSpec((1,) + tail, lambda b, c: (b, 0, 0, 0))
    return pl.BlockSpec((None, 1) + tail, lambda b, c: (sl, b, 0, 0, 0))


def _ssd_mixer(p, h0, conv_w, conv_b, dt_bias, a_log, d_skip, norm_w, *, nseq, nch, T, npad, gw, sl=None):
    heads, P, N = h0.shape[-3:]
    K, xbc_w = conv_w.shape
    wp = p.shape[1]
    padl = lambda v: jnp.pad(v.astype(_F32), (0, LANE - v.shape[0])).reshape(1, LANE)
    kern = functools.partial(_ssd_kernel, T=T, npad=npad, nch=nch, gw=gw, xbc_w=xbc_w, heads=heads, P=P, N=N, K=K)
    return pl.pallas_call(
        kern, grid=(nseq, nch),
        in_specs=[pl.BlockSpec((T, wp), lambda b, c: (b * nch + c, 0)),
                  _state_spec(h0, sl),
                  _full((K, xbc_w)), _full((1, xbc_w)), _full((1, LANE)), _full((1, LANE)), _full((1, LANE)),
                  _full((1, gw))],
        out_specs=[pl.BlockSpec((T, gw), lambda b, c: (b * nch + c, 0)),
                   pl.BlockSpec((1, heads, P, N), lambda b, c: (b, 0, 0, 0))],
        out_shape=[jax.ShapeDtypeStruct((nseq * nch * T, gw), _BF16),
                   jax.ShapeDtypeStruct((nseq, heads, P, N), _F32)],
        scratch_shapes=[pltpu.VMEM((heads, P, N), _F32), pltpu.VMEM((T + 8, xbc_w), _F32)],
        compiler_params=_cparams(("parallel", "arbitrary"), T * wp * 4, 3 * heads * P * N * 4, T * gw * 2,
                                 (T + 8) * xbc_w * 4, 24 * T * max(T, LANE) * 4),
        name="ssd_mixer",
    )(p, h0, conv_w.astype(_F32), conv_b.reshape(1, xbc_w).astype(_F32), padl(dt_bias), padl(a_log), padl(d_skip),
      norm_w.reshape(1, gw).astype(_F32))


def _ret_kernel(p_ref, cc_ref, ss_ref, r0_ref, gw_ref, gb_ref, y_ref, ro_ref, rst, *, T, npad, nch, gw, heads, dk, dv):
    c = pl.program_id(1)
    qkw = heads * dk

    @pl.when(c == 0)
    def _init():
        rst[...] = r0_ref[0]

    npc = jnp.where(c == 0, npad, 0).astype(_F32)
    ri = lax.broadcasted_iota(jnp.int32, (T, T), 0)
    ci = lax.broadcasted_iota(jnp.int32, (T, T), 1)
    causal = ri >= ci
    dlt = (ri - ci).astype(_F32)
    idx = lax.broadcasted_iota(jnp.int32, (T, 1), 0).astype(_F32)
    cc = cc_ref[...]
    ss = ss_ref[...]
    ys = []
    for h in range(heads):
        lg = math.log(1.0 - 2.0 ** (-5.0 - h))
        qh = p_ref[:, h * dk:(h + 1) * dk]
        kh = p_ref[:, qkw + h * dk:qkw + (h + 1) * dk]
        vh = p_ref[:, 2 * qkw + h * dv:2 * qkw + (h + 1) * dv].astype(_BF16)
        qh = (qh * cc + pltpu.roll(qh, dk // 2, 1) * ss)
        kh = (kh * cc + pltpu.roll(kh, dk // 2, 1) * ss) * (dk ** -0.5)
        qb = qh.astype(_BF16)
        dmat = jnp.exp(jnp.where(causal, dlt * lg, -jnp.inf))
        inner = lax.dot_general(qb, kh.astype(_BF16), _NT, preferred_element_type=_F32) * dmat
        y_in = jnp.dot(inner.astype(_BF16), vh, preferred_element_type=_F32)
        rprev = rst[h]
        y_x = jnp.dot(qb, rprev.astype(_BF16), preferred_element_type=_F32) * jnp.exp((idx + 1.0 - npc) * lg)
        kw = (kh * jnp.exp((T - 1.0 - idx) * lg)).astype(_BF16)
        rst[h] = rprev * jnp.exp((T - npc) * lg) + lax.dot_general(kw, vh, _TN, preferred_element_type=_F32)
        o = y_in + y_x
        oc = o - jnp.mean(o, axis=-1, keepdims=True)
        ys.append(oc * lax.rsqrt(jnp.mean(oc * oc, axis=-1, keepdims=True) + RET_GN_EPS))
    gate = p_ref[:, 2 * qkw + gw:2 * qkw + 2 * gw]
    y = _silu(gate) * (jnp.concatenate(ys, axis=-1) * gw_ref[...] + gb_ref[...])
    y_ref[...] = y.astype(y_ref.dtype)

    @pl.when(c == nch - 1)
    def _fin():
        ro_ref[0] = rst[...]


def _ret_mixer(p, r0, gn_w, gn_b, pos0, *, nseq, nch, T, npad, gw, sl=None):
    heads, dk, dv = r0.shape[-3:]
    wp = p.shape[1]
    theta = 1.0 / (ROPE_BASE ** jnp.linspace(0.0, 1.0, dk // 2, dtype=_F32))
    ang = (pos0 + jnp.arange(nch * T) - npad).astype(_F32)[:, None] * theta
    cos, sin = jnp.cos(ang), jnp.sin(ang)
    cc = jnp.concatenate([cos, cos], axis=-1)
    ss = jnp.concatenate([-sin, sin], axis=-1)
    kern = functools.partial(_ret_kernel, T=T, npad=npad, nch=nch, gw=gw, heads=heads, dk=dk, dv=dv)
    return pl.pallas_call(
        kern, grid=(nseq, nch),
        in_specs=[pl.BlockSpec((T, wp), lambda b, c: (b * nch + c, 0)),
                  pl.BlockSpec((T, dk), lambda b, c: (c, 0)), pl.BlockSpec((T, dk), lambda b, c: (c, 0)),
                  _state_spec(r0, sl), _full((1, gw)), _full((1, gw))],
        out_specs=[pl.BlockSpec((T, gw), lambda b, c: (b * nch + c, 0)),
                   pl.BlockSpec((1, heads, dk, dv), lambda b, c: (b, 0, 0, 0))],
        out_shape=[jax.ShapeDtypeStruct((nseq * nch * T, gw), _BF16),
                   jax.ShapeDtypeStruct((nseq, heads, dk, dv), _F32)],
        scratch_shapes=[pltpu.VMEM((heads, dk, dv), _F32)],
        compiler_params=_cparams(("parallel", "arbitrary"), T * wp * 4, 3 * heads * dk * dv * 4, T * gw * 2,
                                 16 * T * max(T, dv) * 4),
        name="ret_mixer",
    )(p, cc, ss, r0, gn_w.reshape(1, gw).astype(_F32), gn_b.reshape(1, gw).astype(_F32))


S5_CB = 128


def _gelu_tanh(x):
    return 0.5 * x * (1.0 + jnp.tanh(math.sqrt(2.0 / math.pi) * (x + 0.044715 * (x * x * x))))


def _s5_kernel(*refs, TC, nb, npad, nch, gw, sb, perm):
    nu = nb if perm else 1
    u_refs = refs[:nu]
    (s0r_ref, s0i_ref, are_ref, aim_ref, ldt_ref, wbr_ref, wbi_ref, wcr_ref, wci_ref, d_ref, gluw_ref, glub_ref,
     nw_ref, y_ref, sor_ref, soi_ref, xr, xi, str_, sti) = refs[nu:]
    c = pl.program_id(1)
    nblk = gw // S5_CB
    R = TC * nb

    @pl.when(c == 0)
    def _init():
        str_[...] = s0r_ref[...]
        sti[...] = s0i_ref[...]

    dt = jnp.exp(ldt_ref[...])
    lr, li = are_ref[...], aim_ref[...]
    mag = jnp.exp(lr * dt)
    abr, abi = mag * jnp.cos(li * dt), mag * jnp.sin(li * dt)
    den = lr * lr + li * li
    nr, ni = abr - 1.0, abi
    er, ei = (nr * lr + ni * li) / den, (ni * lr - nr * li) / den

    if perm:
        ti = lax.broadcasted_iota(jnp.int32, (R, R), 0)
        bi_ = lax.broadcasted_iota(jnp.int32, (R, R), 1)
        pmat = jnp.logical_and(ti // nb == bi_ % TC, ti % nb == bi_ // TC).astype(_F32)
        u = _dot_sel(pmat, jnp.concatenate([r[...] for r in u_refs], axis=0), (((1,), (0,)), ((), ())))
    else:
        u = u_refs[0][...]
    ub = u.astype(_BF16)
    for g in range(nblk):
        us = ub[:, g * S5_CB:(g + 1) * S5_CB]
        br = jnp.dot(us, wbr_ref[g], preferred_element_type=_F32)
        bi = jnp.dot(us, wbi_ref[g], preferred_element_type=_F32)
        e_r, e_i = er[:, g * sb:(g + 1) * sb], ei[:, g * sb:(g + 1) * sb]
        xr[:, g * sb:(g + 1) * sb] = e_r * br - e_i * bi
        xi[:, g * sb:(g + 1) * sb] = e_r * bi + e_i * br

    rows = max(nb, 8)
    spi = rows // nb

    def body(i, carry):
        r0 = pl.multiple_of(i * rows, 8)
        b_r, b_i = xr[pl.ds(r0, rows), :], xi[pl.ds(r0, rows), :]
        s_r, s_i = str_[...], sti[...]
        outs_r, outs_i = [], []
        for j in range(spi):
            n_r = abr * s_r - abi * s_i + b_r[j * nb:(j + 1) * nb]
            n_i = abr * s_i + abi * s_r + b_i[j * nb:(j + 1) * nb]
            s_r, s_i = n_r, n_i
            outs_r.append(n_r)
            outs_i.append(n_i)
        xr[pl.ds(r0, rows), :] = outs_r[0] if spi == 1 else jnp.concatenate(outs_r, axis=0)
        xi[pl.ds(r0, rows), :] = outs_i[0] if spi == 1 else jnp.concatenate(outs_i, axis=0)
        str_[...] = s_r
        sti[...] = s_i
        return carry

    start = jnp.clip(npad - c * TC, 0, TC) // spi
    lax.fori_loop(start, TC // spi, body, 0)

    ys = []
    for g in range(nblk):
        xrb = xr[:, g * sb:(g + 1) * sb].astype(_BF16)
        xib = xi[:, g * sb:(g + 1) * sb].astype(_BF16)
        ys.append(jnp.dot(xrb, wcr_ref[g], preferred_element_type=_F32)
                  - jnp.dot(xib, wci_ref[g], preferred_element_type=_F32))
    y = jnp.concatenate(ys, axis=-1) + d_ref[...] * u
    gy = _gelu_tanh(y)
    out = gy * jax.nn.sigmoid(jnp.dot(gy.astype(_BF16), gluw_ref[...], preferred_element_type=_F32) + glub_ref[...])
    out = (out * lax.rsqrt(jnp.mean(out * out, axis=-1, keepdims=True) + EPS) * nw_ref[...]).astype(y_ref.dtype)
    if perm:
        out = lax.dot_general(pmat.astype(out.dtype), out, _TN, preferred_element_type=_F32).astype(y_ref.dtype)
        for b in range(nb):
            y_ref[b] = out[b * TC:(b + 1) * TC]
    else:
        y_ref[...] = out

    @pl.when(c == nch - 1)
    def _fin():
        sor_ref[...] = str_[...]
        soi_ref[...] = sti[...]


def _s5_mixer(u, s0_re, s0_im, a_re, a_im, log_dt, b_re, b_im, c_re, c_im, d, glu_w, glu_b, norm_w, *,
              nseq, nch, T, npad, nb, TC, perm):
    groups, ns = a_re.shape
    gw = groups * S5_GROUP
    gpb = S5_CB // S5_GROUP
    nblk = gw // S5_CB
    sb = gpb * ns
    nst = groups * ns
    nsb = nseq // nb
    ncc = nch * T // TC
    assert npad % max(1, 8 // nb) == 0
    R = TC * nb
    if perm:
        assert nsb == 1
        u_args = [u] * nb
        u_specs = [pl.BlockSpec((TC, gw), functools.partial(lambda s, c, b: (b * ncc + c, 0), b=b)) for b in range(nb)]
        y_spec = pl.BlockSpec((nb, TC, gw), lambda s, c: (0, c, 0))
        y_shape = jax.ShapeDtypeStruct((nb, ncc * TC, gw), _BF16)
    else:
        u_args = [u[:, :gw].reshape(nsb, nb, ncc, TC, gw).transpose(0, 2, 3, 1, 4).reshape(nseq * nch * T, gw)]
        u_specs = [pl.BlockSpec((R, gw), lambda s, c: (s * ncc + c, 0))]
        y_spec = pl.BlockSpec((R, gw), lambda s, c: (s * ncc + c, 0))
        y_shape = jax.ShapeDtypeStruct((nseq * nch * T, gw), _BF16)
    eye = jnp.eye(gpb, dtype=_F32)

    def bd_in(w):
        w4 = w.reshape(nblk, gpb, ns, S5_GROUP)
        return jnp.einsum('bgnc,gh->bgchn', w4, eye).reshape(nblk, S5_CB, sb).astype(_BF16)

    def bd_out(w):
        w4 = w.reshape(nblk, gpb, S5_GROUP, ns)
        return jnp.einsum('bgcn,gh->bgnhc', w4, eye).reshape(nblk, sb, S5_CB).astype(_BF16)

    row = lambda v: v.reshape(1, -1).astype(_F32)
    kern = functools.partial(_s5_kernel, TC=TC, nb=nb, npad=npad, nch=ncc, gw=gw, sb=sb, perm=perm)
    y, so_r, so_i = pl.pallas_call(
        kern, grid=(nsb, ncc),
        in_specs=u_specs + [
            pl.BlockSpec((nb, nst), lambda s, c: (s, 0)), pl.BlockSpec((nb, nst), lambda s, c: (s, 0)),
            _full((1, nst)), _full((1, nst)), _full((1, nst)),
            _full((nblk, S5_CB, sb)), _full((nblk, S5_CB, sb)), _full((nblk, sb, S5_CB)),
            _full((nblk, sb, S5_CB)), _full((1, gw)), _full((gw, gw)), _full((1, gw)), _full((1, gw))],
        out_specs=[y_spec, pl.BlockSpec((nb, nst), lambda s, c: (s, 0)), pl.BlockSpec((nb, nst), lambda s, c: (s, 0))],
        out_shape=[y_shape, jax.ShapeDtypeStruct((nseq, nst), _F32), jax.ShapeDtypeStruct((nseq, nst), _F32)],
        scratch_shapes=[pltpu.VMEM((R, nst), _F32), pltpu.VMEM((R, nst), _F32),
                        pltpu.VMEM((nb, nst), _F32), pltpu.VMEM((nb, nst), _F32)],
        compiler_params=_cparams(("parallel", "arbitrary"), 2 * R * gw * 4, R * gw * 2, 4 * nblk * S5_CB * sb * 2,
                                 gw * gw * 2, R * nst * 4, 6 * max(nb, 8) * nst * 4),
        name="s5_mixer",
    )(*u_args, s0_re.reshape(nseq, nst), s0_im.reshape(nseq, nst), row(a_re), row(a_im),
      row(jnp.broadcast_to(log_dt[:, None], (groups, ns))), bd_in(b_re), bd_in(b_im), bd_out(c_re), bd_out(c_im),
      row(d), glu_w.astype(_BF16), row(glu_b), row(norm_w))
    if perm:
        y = y.reshape(nseq * nch * T, gw)
    else:
        y = y.reshape(nsb, ncc, TC, nb, gw).transpose(0, 3, 1, 2, 4).reshape(nseq * nch * T, gw)
    return y, so_r.reshape(nseq, groups, ns), so_i.reshape(nseq, groups, ns)


def _head_sums(x):
    ri = lax.broadcasted_iota(jnp.int32, (LANE, LANE), 0) // RWKV_HEAD_DIM
    ci = lax.broadcasted_iota(jnp.int32, (LANE, LANE), 1) // RWKV_HEAD_DIM
    e = (ri == ci).astype(_F32)
    nn = (((1,), (0,)), ((), ()))
    return jnp.concatenate([_dot_sel_r(x[:, j:j + LANE], e, nn) for j in range(0, x.shape[1], LANE)], axis=-1)


def _rwkv_pre_kernel(p_ref, mu_ref, w0_ref, w2_ref, a0_ref, a2_ref, g2_ref, kk_ref, ka_ref, rk_ref,
                     r_o, k_o, v_o, w_o, kk_o, b_o, g_o, bon_o, xbuf, *, T, gw, dl, da, dg):
    c = pl.program_id(1)

    @pl.when(c == 0)
    def _init():
        xbuf[0:8, :] = jnp.zeros((8, xbuf.shape[1]), _F32)

    p = p_ref[...]
    xbuf[8:8 + T, :] = p
    prev = xbuf[7:7 + T, :]
    last = xbuf[7 + T:8 + T, :]
    xbuf[7:8, :] = last
    pm = p + (prev - p) * mu_ref[...]
    r, k, v = pm[:, :gw], pm[:, gw:2 * gw], pm[:, 2 * gw:3 * gw]
    o1 = 3 * gw
    wl, al, gl = pm[:, o1:o1 + dl], pm[:, o1 + dl:o1 + dl + da], pm[:, o1 + dl + da:o1 + dl + da + dg]
    wx = w0_ref[...] + jnp.dot(jnp.tanh(wl).astype(_BF16), w2_ref[...], preferred_element_type=_F32)
    decay = jnp.exp(-jnp.exp(-_softplus(-wx) - 0.5))
    a = jax.nn.sigmoid(a0_ref[...] + jnp.dot(al.astype(_BF16), a2_ref[...], preferred_element_type=_F32))
    g = jnp.dot(jax.nn.sigmoid(gl).astype(_BF16), g2_ref[...], preferred_element_type=_F32)
    kkf = k * kk_ref[...]
    kk = kkf / jnp.maximum(jnp.sqrt(_head_sums(kkf * kkf)), 1e-12)
    k2 = k * (1.0 + (a - 1.0) * ka_ref[...])
    r_o[...] = r
    k_o[...] = k2
    v_o[...] = v
    w_o[...] = decay
    kk_o[...] = kk
    b_o[...] = kk * a
    g_o[...] = g
    bon_o[...] = _head_sums(r * k2 * rk_ref[...]) * v


def _rwkv_scan_kernel(r_ref, k_ref, v_ref, w_ref, kk_ref, b_ref, s0_ref, o_ref, so_ref, st, *, TC, K, npad, nch):
    c = pl.program_id(1)

    @pl.when(c == 0)
    def _init():
        st[...] = s0_ref[...]

    start = jnp.clip(npad - c * TC, 0, TC)

    @pl.when(start > 0)
    def _zero():
        o_ref[...] = jnp.zeros(o_ref.shape, _F32)

    vecs = (kk_ref, w_ref, b_ref, k_ref, r_ref)
    NACC = 4

    def tree(acc):
        return (acc[0] + acc[1]) + (acc[2] + acc[3])

    def madd(acc, k, x):
        acc[k % NACC] = x if acc[k % NACC] is None else acc[k % NACC] + x

    def sa_of(t):
        acc = [None] * NACC
        for k in range(K):
            madd(acc, k, st[k] * vecs[0][k, pl.ds(t, 1), :])
        return tree(acc)

    def body(t, sa):
        row = lambda i, k: vecs[i][k, pl.ds(t, 1), :]
        tn = jnp.minimum(t + 1, TC - 1)
        v_t = v_ref[t]
        acc_o, acc_s = [None] * NACC, [None] * NACC
        for k in range(K):
            s = st[k] * row(1, k) - sa * row(2, k) + v_t * row(3, k)
            st[k] = s
            madd(acc_o, k, s * row(4, k))
            madd(acc_s, k, s * vecs[0][k, pl.ds(tn, 1), :])
        o_ref[t] = tree(acc_o)
        return tree(acc_s)

    lax.fori_loop(start, TC, body, sa_of(jnp.minimum(start, TC - 1)))

    @pl.when(c == nch - 1)
    def _fin():
        so_ref[...] = st[...]


def _rwkv_relayout_kernel(*refs, nseq, H, K, J):
    x_refs, o_ref, zs = refs[:nseq], refs[nseq], refs[nseq + 1]
    gw = H * K
    for n in range(nseq):
        x = x_refs[n][...]
        for c0 in range(0, gw, LANE):
            zs[n * gw + c0:n * gw + c0 + LANE, :] = x[:, c0:c0 + LANE].T
    for k in range(K):
        q = zs[pl.ds(k, nseq * H, stride=K), :]
        o_ref[k] = jnp.concatenate([q] * J, axis=0).T


def _rwkv_relayout(x, *, nseq, nch, H, K, J):
    gw = H * K
    nl = J * nseq * H
    assert nl == LANE
    specs = [pl.BlockSpec((CHUNK, gw), functools.partial(lambda c, n: (n * nch + c, 0), n=n)) for n in range(nseq)]
    return pl.pallas_call(
        functools.partial(_rwkv_relayout_kernel, nseq=nseq, H=H, K=K, J=J), grid=(nch,),
        in_specs=specs, out_specs=pl.BlockSpec((K, CHUNK, nl), lambda c: (0, c, 0)),
        out_shape=jax.ShapeDtypeStruct((K, nch * CHUNK, nl), _F32),
        scratch_shapes=[pltpu.VMEM((nseq * gw, CHUNK), _F32)],
        compiler_params=_cparams(("parallel",), nseq * CHUNK * gw * 4, K * CHUNK * nl * 4, nseq * gw * CHUNK * 4),
        name="rwkv_relayout",
    )(*([x] * nseq))


def _rwkv_post_kernel(o_ref, bon_ref, g_ref, lw_ref, lb_ref, y_ref):
    o = o_ref[...]
    oc = o - _head_sums(o) * (1.0 / RWKV_HEAD_DIM)
    var = _head_sums(oc * oc) * (1.0 / RWKV_HEAD_DIM)
    y = (oc * lax.rsqrt(var + RWKV_GN_EPS) * lw_ref[...] + lb_ref[...] + bon_ref[...]) * g_ref[...]
    y_ref[...] = y.astype(y_ref.dtype)


def _rwkv_mixer(p, s0, mu_p, w0, w2_p, a0, a2_p, g2_p, k_k, k_a, r_k, ln_w, ln_b, *, nseq, nch, T, npad, gw,
                dl, da, dg, J, TC):
    H, V, K = s0.shape[1:]
    wp = p.shape[1]
    rows = nseq * nch * T
    L = nch * T
    row = lambda v: v.reshape(1, -1).astype(_F32)
    f32rows = jax.ShapeDtypeStruct((rows, gw), _F32)
    blk = pl.BlockSpec((T, gw), lambda b, c: (b * nch + c, 0))
    pre = pl.pallas_call(
        functools.partial(_rwkv_pre_kernel, T=T, gw=gw, dl=dl, da=da, dg=dg), grid=(nseq, nch),
        in_specs=[pl.BlockSpec((T, wp), lambda b, c: (b * nch + c, 0)), _full((1, wp)), _full((1, gw)),
                  _full((dl, gw)), _full((1, gw)), _full((da, gw)), _full((dg, gw)), _full((1, gw)), _full((1, gw)),
                  _full((1, gw))],
        out_specs=[blk] * 8, out_shape=[f32rows] * 8,
        scratch_shapes=[pltpu.VMEM((T + 8, wp), _F32)],
        compiler_params=_cparams(("parallel", "arbitrary"), 2 * T * wp * 4, 8 * T * gw * 4, 12 * T * gw * 4),
        name="rwkv_pre",
    )(p, row(mu_p), row(w0), w2_p.astype(_BF16), row(a0), a2_p.astype(_BF16), g2_p.astype(_BF16), row(k_k),
      row(k_a), row(r_k))
    r, k2, v, decay, kk, bvec, g, bonus = pre

    VI = V // J
    NL = J * nseq * H
    assert NL % LANE == 0 or NL < LANE

    def kvec(x):
        if T == CHUNK and NL == LANE:
            return _rwkv_relayout(x, nseq=nseq, nch=nch, H=H, K=K, J=J)
        y = x.reshape(nseq, L, H, K).transpose(3, 1, 0, 2).reshape(K, L, 1, nseq * H)
        return jnp.broadcast_to(y, (K, L, J, nseq * H)).reshape(K, L, NL)

    vv = v.reshape(nseq, L, H, J, VI).transpose(1, 4, 3, 0, 2).reshape(L, VI, NL)
    st0 = s0.reshape(nseq, H, J, VI, K).transpose(4, 3, 2, 0, 1).reshape(K, VI, NL)
    lb = min(NL, LANE)
    ncc = L // TC
    kspec = pl.BlockSpec((K, TC, lb), lambda n, c: (0, c, n))
    vspec = pl.BlockSpec((TC, VI, lb), lambda n, c: (c, 0, n))
    sspec = pl.BlockSpec((K, VI, lb), lambda n, c: (0, 0, n))
    o, st1 = pl.pallas_call(
        functools.partial(_rwkv_scan_kernel, TC=TC, K=K, npad=npad, nch=ncc), grid=(NL // lb, ncc),
        in_specs=[kspec, kspec, vspec, kspec, kspec, kspec, sspec],
        out_specs=[vspec, sspec],
        out_shape=[jax.ShapeDtypeStruct((L, VI, NL), _F32), jax.ShapeDtypeStruct((K, VI, NL), _F32)],
        scratch_shapes=[pltpu.VMEM((K, VI, lb), _F32)],
        compiler_params=_cparams(("parallel", "arbitrary"), 5 * TC * K * lb * 4, 2 * TC * VI * lb * 4,
                                 3 * VI * K * lb * 4),
        name="rwkv_scan",
    )(kvec(r), kvec(k2), vv, kvec(decay), kvec(kk), kvec(bvec), st0)
    o = o.reshape(L, VI, J, nseq, H).transpose(3, 0, 4, 2, 1).reshape(rows, gw)
    s_new = st1.reshape(K, VI, J, nseq, H).transpose(3, 4, 2, 1, 0).reshape(nseq, H, V, K)

    tr = _pick(rows, (256, 128, 64, 32, 16, 8))
    rblk = pl.BlockSpec((tr, gw), lambda i: (i, 0))
    y = pl.pallas_call(
        _rwkv_post_kernel, grid=(rows // tr,),
        in_specs=[rblk, rblk, rblk, _full((1, gw)), _full((1, gw))],
        out_specs=rblk, out_shape=jax.ShapeDtypeStruct((rows, gw), _BF16),
        compiler_params=_cparams(("parallel",), 4 * tr * gw * 4, 8 * tr * gw * 4),
        name="rwkv_post",
    )(o, bonus, g, row(ln_w), row(ln_b))
    return y, s_new


def kernel(x_prompt, x_sample, state_ssd, state_ssd_conv, state_rwkv, state_rwkv_shift, state_ret, state_s5_re,
           state_s5_im, meta, ln_mix, w_in, ssd_conv_w, ssd_conv_b, ssd_dt_bias, ssd_a_log, ssd_d, ssd_norm, rwkv_mu,
           rwkv_w0, rwkv_w2, rwkv_a0, rwkv_a2, rwkv_g2, rwkv_k_k, rwkv_k_a, rwkv_r_k, rwkv_ln_w, rwkv_ln_b, ret_gn_w,
           ret_gn_b, s5_a_re, s5_a_im, s5_log_dt, s5_b_re, s5_b_im, s5_c_re, s5_c_im, s5_d, s5_glu_w, s5_glu_b,
           s5_norm, w_out, ln_ffn, w_gate, w_up, w_down, ln_f):
    bp, sp, d = x_prompt.shape
    bs, ls, _ = x_sample.shape
    depth = w_in.shape[0]
    gw = d // 4
    ssd_heads, ssd_p, ssd_n = state_ssd.shape[2:]
    xbc_w = state_ssd_conv.shape[-1]
    rwkv_proj = state_rwkv_shift.shape[-1]
    ret_heads, ret_dk, ret_dv = state_ret.shape[2:]
    qkw = ret_heads * ret_dk
    dff = w_gate.shape[-1]

    lreal = N_META + sp
    lp = _round_up(lreal, CHUNK)
    pad = lp - lreal
    mp, ms = bp * lp, bs * ls
    m = mp + ms

    head = jnp.concatenate([jnp.zeros((pad, d), _F32), meta.astype(_F32)], axis=0)
    h = jnp.concatenate([piece for b in range(bp) for piece in (head, x_prompt[b])] + [x_sample.reshape(ms, d)],
                        axis=0)

    in_splits = (gw, xbc_w, ssd_heads, rwkv_proj, qkw, qkw, gw, gw, gw)
    offs = [0]
    for s in in_splits:
        offs.append(offs[-1] + s)
    ntile = 512 if gw % 512 == 0 else LANE

    layers = range(depth)

    lora_c = (rwkv_w2.shape[1], rwkv_a2.shape[1], rwkv_g2.shape[1])
    dl, da, dg = (_round_up(n, LANE) for n in lora_c)
    rw_w = _round_up(3 * gw + dl + da + dg, ntile)

    def rw_pad(x):
        parts, o = [x[..., :3 * gw]], 3 * gw
        for n, npd in zip(lora_c, (dl, da, dg)):
            parts.append(jnp.pad(x[..., o:o + n], [(0, 0)] * (x.ndim - 1) + [(0, npd - n)]))
            o += n
        y = jnp.concatenate(parts, axis=-1)
        return jnp.pad(y, [(0, 0)] * (x.ndim - 1) + [(0, rw_w - y.shape[-1])])

    def rw_compact(x):
        parts, o = [x[..., :3 * gw]], 3 * gw
        for n, npd in zip(lora_c, (dl, da, dg)):
            parts.append(x[..., o:o + n])
            o += npd
        return jnp.concatenate(parts, axis=-1)

    assert (gw + xbc_w) % LANE == 0 and gw % LANE == 0
    o_rw = offs[3] + 3 * gw
    plans = (
        ((0, 0, gw + xbc_w), (gw + xbc_w, gw + xbc_w, ssd_heads)),
        ((0, offs[3], 3 * gw), (3 * gw, o_rw, lora_c[0]), (3 * gw + dl, o_rw + lora_c[0], lora_c[1]),
         (3 * gw + dl + da, o_rw + lora_c[0] + lora_c[1], lora_c[2])),
        ((0, offs[4], offs[8] - offs[4]),),
        ((0, offs[8], gw),),
    )
    widths = (_round_up(gw + xbc_w + LANE, ntile), rw_w, _round_up(offs[8] - offs[4], ntile), _round_up(gw, ntile))
    packed = [_pack_w_in(w_in, l, plans, widths) for l in layers]
    w_ssd_b, w_rw_b, w_ret_b, w_s5_b = ([p[i] for p in packed] for i in range(4))
    mu_p = rw_pad(rwkv_mu)
    padrows = lambda w, n: jnp.pad(w, ((0, 0), (0, n - w.shape[1]), (0, 0)))
    w2_p, a2_p, g2_p = padrows(rwkv_w2, dl), padrows(rwkv_a2, da), padrows(rwkv_g2, dg)
    w_out_b, w_gate_b, w_up_b, w_down_b = (w.astype(_BF16) for w in (w_out, w_gate, w_up, w_down))
    half = dff // 2

    zeros_b = lambda s: jnp.zeros((bp,) + s.shape[2:], _F32)
    nch_p = lp // CHUNK
    ts = _round_up(ls + 3, 8)
    spad = ts - ls

    def sample_rows(rows, hist=None, c0=0):
        r3 = jnp.pad(rows.reshape(bs, ls, -1), ((0, 0), (spad, 0), (0, 0)))
        if hist is not None:
            k, w = hist.shape[1:]
            r3 = r3.at[:, spad - k:spad, c0:c0 + w].set(hist)
        return r3.reshape(bs * ts, -1)

    def unsample(y):
        return y.reshape(bs, ts, -1)[:, spad:].reshape(ms, -1)

    def last_rows(rows, k):
        return jnp.stack([rows[(b + 1) * lp - k:(b + 1) * lp] for b in range(bp)])

    outs_p = [[] for _ in range(7)]
    outs_s = [[] for _ in range(7)]
    for l in range(depth):
        hn = _rmsnorm(h, ln_mix[l], _BF16, lp=lp, pad=pad, mp=mp)

        pa = _mm(hn, w_ssd_b[l], name="in_proj_ssd")
        ssd_w = (ssd_conv_w[l], ssd_conv_b[l], ssd_dt_bias[l], ssd_a_log[l], ssd_d[l], ssd_norm[l])
        ya_p, hp_new = _ssd_mixer(pa, zeros_b(state_ssd), *ssd_w, nseq=bp, nch=nch_p, T=CHUNK, npad=pad, gw=gw)
        pa_s = sample_rows(pa[mp:], state_ssd_conv[l], gw)
        ya_s, hs_new = _ssd_mixer(pa_s, state_ssd, *ssd_w, nseq=bs, nch=1, T=ts, npad=spad, gw=gw, sl=l)
        kc = ssd_conv_w.shape[1] - 1
        outs_p[0].append(hp_new)
        outs_p[1].append(last_rows(pa, kc)[..., gw:gw + xbc_w])
        outs_s[0].append(hs_new)
        outs_s[1].append(pa_s.reshape(bs, ts, -1)[:, ts - kc:, gw:gw + xbc_w])

        pb = _mm(hn, w_rw_b[l], name="in_proj_rwkv")
        rw_args = (mu_p[l], rwkv_w0[l], w2_p[l], rwkv_a0[l], a2_p[l], g2_p[l], rwkv_k_k[l], rwkv_k_a[l],
                   rwkv_r_k[l], rwkv_ln_w[l], rwkv_ln_b[l])
        rw_kw = dict(gw=gw, dl=dl, da=da, dg=dg)
        rw_heads = state_rwkv.shape[2]
        yb_p, sp_new = _rwkv_mixer(pb, zeros_b(state_rwkv), *rw_args, nseq=bp, nch=nch_p, T=CHUNK, npad=pad,
                                   J=max(1, LANE // (bp * rw_heads)), TC=CHUNK // 4, **rw_kw)
        pb_s = sample_rows(pb[mp:], rw_pad(state_rwkv_shift[l])[:, None, :], 0)
        yb_s, ss_new = _rwkv_mixer(pb_s, state_rwkv[l], *rw_args, nseq=bs, nch=1, T=ts, npad=spad,
                                   J=max(1, LANE // (bs * rw_heads)), TC=ts, **rw_kw)
        outs_p[2].append(sp_new)
        outs_p[3].append(rw_compact(last_rows(pb, 1)[:, 0]))
        outs_s[2].append(ss_new)
        outs_s[3].append(rw_compact(pb_s.reshape(bs, ts, -1)[:, -1]))

        pc = _mm(hn, w_ret_b[l], name="in_proj_ret")
        yc_p, rp_new = _ret_mixer(pc, zeros_b(state_ret), ret_gn_w[l], ret_gn_b[l], 0,
                                  nseq=bp, nch=nch_p, T=CHUNK, npad=pad, gw=gw)
        yc_s, rs_new = _ret_mixer(sample_rows(pc[mp:]), state_ret, ret_gn_w[l], ret_gn_b[l], PAST_LEN,
                                  nseq=bs, nch=1, T=ts, npad=spad, gw=gw, sl=l)
        outs_p[4].append(rp_new)
        outs_s[4].append(rs_new)

        pd = _mm(hn, w_s5_b[l], name="in_proj_s5")
        s5_w = (s5_a_re[l], s5_a_im[l], s5_log_dt[l], s5_b_re[l], s5_b_im[l], s5_c_re[l], s5_c_im[l], s5_d[l],
                s5_glu_w[l], s5_glu_b[l], s5_norm[l])
        yd_p, s5r_p, s5i_p = _s5_mixer(pd, zeros_b(state_s5_re), zeros_b(state_s5_im), *s5_w, nseq=bp, nch=nch_p,
                                       T=CHUNK, npad=pad, nb=bp, TC=CHUNK // 2, perm=True)
        yd_s, s5r_s, s5i_s = _s5_mixer(sample_rows(pd[mp:]), state_s5_re[l], state_s5_im[l], *s5_w, nseq=bs, nch=1,
                                       T=ts, npad=spad, nb=min(bs, 32), TC=ts, perm=False)
        outs_p[5].append(s5r_p)
        outs_p[6].append(s5i_p)
        outs_s[5].append(s5r_s)
        outs_s[6].append(s5i_s)

        h = _outproj((ya_p, yb_p, yc_p, yd_p), w_out_b, h, 0, wl=l)
        h = _outproj(tuple(unsample(y) for y in (ya_s, yb_s, yc_s, yd_s)), w_out_b, h, mp, wl=l)
        hn = _rmsnorm(h, ln_ffn[l], _BF16)
        ff = _swiglu(hn, w_gate_b, w_up_b, wl=l)
        h = _mm(ff, w_down_b, res=h, kb=0, tk=half, name="ffn_down0", wl=l)
        h = _mm(ff, w_down_b, res=h, kb=1, tk=half, name="ffn_down1", wl=l)

    y_prompt, y_sample = _final_norm(h, ln_f, bp=bp, sp=sp, lp=lp, ms=ms)
    return ((y_prompt.reshape(bp, sp, d), y_sample.reshape(bs, ls, d))
            + tuple(jnp.stack(a) for a in outs_p) + tuple(jnp.stack(a) for a in outs_s))
```

```python
import functools
import math

import jax
import jax.numpy as jnp
from jax import lax
from jax.experimental import pallas as pl
from jax.experimental.pallas import tpu as pltpu

N_META = 16
EPS = 1e-6
SSD_GROUPS = 2
RWKV_HEAD_DIM = 64
RWKV_GN_EPS = 64e-5
RET_GN_EPS = 1e-5
ROPE_BASE = 10000.0
S5_GROUP = 16
PAST_LEN = 16384
CHUNK = 128
LANE = 128
V7X_VMEM_CAP = 60 * 1024 * 1024

_F32 = jnp.float32
_BF16 = jnp.bfloat16


def _pick(n, cands):
    for c in cands:
        if n % c == 0:
            return c
    raise ValueError(f"no tile in {cands} divides {n}")


def _round_up(n, m):
    return -(-n // m) * m


def _cparams(sem, *block_bytes):
    need = 2 * sum(block_bytes) + (6 << 20)
    return pltpu.CompilerParams(dimension_semantics=sem, vmem_limit_bytes=int(min(max(need, 16 << 20), V7X_VMEM_CAP)))


def _rmsnorm_kernel(x_ref, g_ref, o_ref, *, tr, lp, pad, mp):
    x = x_ref[...]
    y = x * lax.rsqrt(jnp.mean(x * x, axis=-1, keepdims=True) + EPS) * g_ref[...]
    if pad:
        row0 = pl.program_id(0) * tr
        pos0 = lax.rem(row0, lp)
        rows = lax.broadcasted_iota(jnp.int32, (tr, 1), 0)
        is_pad = jnp.logical_and(row0 < mp, pos0 + rows < pad)
        y = jnp.where(is_pad, 0.0, y)
    o_ref[...] = y.astype(o_ref.dtype)


def _rmsnorm(x, g, out_dtype, lp=0, pad=0, mp=0):
    m, d = x.shape
    tr = _pick(math.gcd(m, lp) if pad else m, (256, 128, 64, 32, 16, 8))
    kern = functools.partial(_rmsnorm_kernel, tr=tr, lp=lp, pad=pad, mp=mp)
    return pl.pallas_call(
        kern, grid=(m // tr,),
        in_specs=[pl.BlockSpec((tr, d), lambda i: (i, 0)), pl.BlockSpec((1, d), lambda i: (0, 0))],
        out_specs=pl.BlockSpec((tr, d), lambda i: (i, 0)),
        out_shape=jax.ShapeDtypeStruct((m, d), out_dtype),
        compiler_params=_cparams(("parallel",), tr * d * 4, tr * d * 4),
        name="rmsnorm",
    )(x, g.reshape(1, d).astype(_F32))


def _final_norm(h, g, *, bp, sp, lp, ms):
    d = h.shape[1]
    assert lp - sp == CHUNK and sp % CHUNK == 0
    nch, nout = lp // CHUNK, sp // CHUNK
    kern = functools.partial(_rmsnorm_kernel, tr=CHUNK, lp=0, pad=0, mp=0)
    g2 = g.reshape(1, d).astype(_F32)
    yp = pl.pallas_call(
        kern, grid=(bp, nout),
        in_specs=[pl.BlockSpec((CHUNK, d), lambda b, j: (b * nch + 1 + j, 0)), pl.BlockSpec((1, d), lambda b, j: (0, 0))],
        out_specs=pl.BlockSpec((CHUNK, d), lambda b, j: (b * nout + j, 0)),
        out_shape=jax.ShapeDtypeStruct((bp * sp, d), _F32),
        compiler_params=_cparams(("parallel", "parallel"), CHUNK * d * 4, CHUNK * d * 4),
        name="final_norm_prompt",
    )(h, g2)
    tr = _pick(math.gcd(ms, bp * lp), (256, 128, 64, 32, 16, 8))
    off = bp * lp // tr
    ys = pl.pallas_call(
        functools.partial(_rmsnorm_kernel, tr=tr, lp=0, pad=0, mp=0), grid=(ms // tr,),
        in_specs=[pl.BlockSpec((tr, d), lambda i: (off + i, 0)), pl.BlockSpec((1, d), lambda i: (0, 0))],
        out_specs=pl.BlockSpec((tr, d), lambda i: (i, 0)),
        out_shape=jax.ShapeDtypeStruct((ms, d), _F32),
        compiler_params=_cparams(("parallel",), tr * d * 4, tr * d * 4),
        name="final_norm_sample",
    )(h, g2)
    return yp, ys


def _pack_kernel(w_ref, *o_refs, plans):
    x = w_ref[...]
    tr = x.shape[0]
    for o_ref, plan in zip(o_refs, plans):
        done = 0
        for dst, src, n in plan:
            npd = _round_up(n, LANE)
            assert dst % LANE == 0 and dst >= done
            if dst > done:
                o_ref[:, done:dst] = jnp.zeros((tr, dst - done), o_ref.dtype)
            seg = x[:, src:src + npd]
            if npd > n:
                seg = jnp.where(lax.broadcasted_iota(jnp.int32, (tr, npd), 1) < n, seg, 0.0)
            o_ref[:, dst:dst + npd] = seg.astype(o_ref.dtype)
            done = dst + npd
        if done < o_ref.shape[1]:
            o_ref[:, done:] = jnp.zeros((tr, o_ref.shape[1] - done), o_ref.dtype)


def _pack_w_in(w_in, l, plans, widths):
    _, d, nin = w_in.shape
    assert all(src + _round_up(n, LANE) <= nin for plan in plans for _, src, n in plan)
    tr = _pick(d, (128, 64, 32, 16, 8))
    return pl.pallas_call(
        functools.partial(_pack_kernel, plans=plans), grid=(d // tr,),
        in_specs=[pl.BlockSpec((tr, nin), lambda i: (l * (d // tr) + i, 0))],
        out_specs=[pl.BlockSpec((tr, w), lambda i: (i, 0)) for w in widths],
        out_shape=[jax.ShapeDtypeStruct((d, w), _BF16) for w in widths],
        compiler_params=_cparams(("parallel",), 2 * tr * nin * 4, tr * sum(widths) * 2),
        name="pack_w_in",
    )(w_in.reshape(-1, nin))


def _mm_kernel(*refs, has_res):
    if has_res:
        x_ref, w_ref, r_ref, o_ref = refs
    else:
        x_ref, w_ref, o_ref = refs
    acc = jnp.dot(x_ref[...], w_ref[...], preferred_element_type=_F32)
    if has_res:
        acc = acc + r_ref[...]
    o_ref[...] = acc.astype(o_ref.dtype)


def _wspec(w, wl, rows, cols, imap):
    if w.ndim == 2:
        return pl.BlockSpec((rows, cols), imap)
    return pl.BlockSpec((None, rows, cols), lambda i, j: (wl,) + imap(i, j))


def _mm(x, w, res=None, out_dtype=_F32, kb=0, tk=None, name="matmul", wl=None):
    m = x.shape[0]
    n = w.shape[-1]
    tk = tk or w.shape[-2]
    tm = _pick(m, (1024, 512, 256, 128))
    tn = _pick(n, (512, 256, 128))
    in_specs = [pl.BlockSpec((tm, tk), lambda i, j: (i, kb)), _wspec(w, wl, tk, tn, lambda i, j: (kb, j))]
    args = [x, w]
    blocks = [tm * tk * 2, tk * tn * 2, tm * tn * 4]
    if res is not None:
        in_specs.append(pl.BlockSpec((tm, tn), lambda i, j: (i, j)))
        args.append(res)
        blocks.append(tm * tn * 4)
    return pl.pallas_call(
        functools.partial(_mm_kernel, has_res=res is not None), grid=(m // tm, n // tn),
        in_specs=in_specs, out_specs=pl.BlockSpec((tm, tn), lambda i, j: (i, j)),
        out_shape=jax.ShapeDtypeStruct((m, n), out_dtype),
        compiler_params=_cparams(("parallel", "arbitrary"), *blocks),
        name=name,
    )(*args)


def _outproj_kernel(*refs, nmix, gw):
    xs = refs[:nmix]
    w_ref, r_ref, o_ref = refs[nmix:]
    acc = r_ref[...]
    for j, x in enumerate(xs):
        acc = acc + jnp.dot(x[...], w_ref[j * gw:(j + 1) * gw, :], preferred_element_type=_F32)
    o_ref[...] = acc


def _outproj(ys, w, h, row0, wl=None):
    nrows, gw = ys[0].shape
    nmix = len(ys)
    d, n = w.shape[-2:]
    tm = _pick(math.gcd(nrows, row0) if row0 else nrows, (1088, 1024, 512, 256, 128))
    tn = _pick(n, (512, 256, 128))
    blk0 = row0 // tm
    hspec = pl.BlockSpec((tm, tn), lambda i, j: (blk0 + i, j))
    return pl.pallas_call(
        functools.partial(_outproj_kernel, nmix=nmix, gw=gw), grid=(nrows // tm, n // tn),
        in_specs=[pl.BlockSpec((tm, gw), lambda i, j: (i, 0))] * nmix + [_wspec(w, wl, d, tn, lambda i, j: (0, j)),
                                                                          hspec],
        out_specs=hspec, out_shape=jax.ShapeDtypeStruct(h.shape, _F32),
        input_output_aliases={nmix + 1: 0},
        compiler_params=_cparams(("parallel", "arbitrary"), nmix * tm * gw * 2, d * tn * 2, 2 * tm * tn * 4),
        name="out_proj",
    )(*ys, w, h)


def _swiglu_kernel(x_ref, wg_ref, wu_ref, o_ref):
    x = x_ref[...]
    g = jnp.dot(x, wg_ref[...], preferred_element_type=_F32)
    u = jnp.dot(x, wu_ref[...], preferred_element_type=_F32)
    o_ref[...] = (g * jax.nn.sigmoid(g) * u).astype(o_ref.dtype)


def _swiglu(x, wg, wu, wl=None):
    m, k = x.shape
    n = wg.shape[-1]
    tm = _pick(m, (1024, 512, 256, 128))
    tn = _pick(n, (256, 128))
    return pl.pallas_call(
        _swiglu_kernel, grid=(m // tm, n // tn),
        in_specs=[pl.BlockSpec((tm, k), lambda i, j: (i, 0)), _wspec(wg, wl, k, tn, lambda i, j: (0, j)),
                  _wspec(wu, wl, k, tn, lambda i, j: (0, j))],
        out_specs=pl.BlockSpec((tm, tn), lambda i, j: (i, j)),
        out_shape=jax.ShapeDtypeStruct((m, n), _BF16),
        compiler_params=_cparams(("parallel", "arbitrary"), tm * k * 2, 2 * k * tn * 2, tm * tn * 2),
        name="swiglu",
    )(x, wg, wu)


_NT = (((1,), (1,)), ((), ()))
_TN = (((0,), (0,)), ((), ()))


def _silu(x):
    return x * jax.nn.sigmoid(x)


def _softplus(x):
    return jnp.maximum(x, 0.0) + jnp.log(1.0 + jnp.exp(-jnp.abs(x)))


def _split3(x):
    hi = x.astype(_BF16)
    r = x - hi.astype(_F32)
    mid = r.astype(_BF16)
    lo = (r - mid.astype(_F32)).astype(_BF16)
    return hi, mid, lo


def _dot_sel(sel, x, dims):
    parts = _split3(x) if _BF16 == jnp.bfloat16 else (x,)
    out = None
    for p in parts:
        t = lax.dot_general(sel.astype(p.dtype), p, dims, preferred_element_type=_F32)
        out = t if out is None else out + t
    return out


def _dot_sel_r(x, sel, dims):
    parts = _split3(x) if _BF16 == jnp.bfloat16 else (x,)
    out = None
    for p in parts:
        t = lax.dot_general(p, sel.astype(p.dtype), dims, preferred_element_type=_F32)
        out = t if out is None else out + t
    return out


def _full(shape):
    nd = len(shape)
    return pl.BlockSpec(shape, lambda *_: (0,) * nd)


def _ssd_kernel(p_ref, h0_ref, cw_ref, cb_ref, dtb_ref, alog_ref, dsk_ref, nw_ref, y_ref, ho_ref, hst, xbuf, *,
                T, SB, npad, nch, gw, xbc_w, heads, P, N, K):
    c = pl.program_id(1)
    G = SSD_GROUPS
    hpg = heads // G

    @pl.when(c == 0)
    def _init():
        hst[...] = h0_ref[...]
        xbuf[:, 0:8, :] = jnp.zeros((SB, 8, xbc_w), _F32)

    ri = lax.broadcasted_iota(jnp.int32, (T, T), 0)
    ci = lax.broadcasted_iota(jnp.int32, (T, T), 1)
    causal = ri >= ci
    tril = causal.astype(_F32)
    hq = lax.broadcasted_iota(jnp.int32, (heads * T, LANE), 0) // T
    ln = lax.broadcasted_iota(jnp.int32, (heads * T, LANE), 1)
    headsel = (ln == hq).astype(_F32)
    nalog = -jnp.exp(alog_ref[...])

    outs = []
    for s in range(SB):
        r0 = s * T
        xbuf[s, 8:8 + T, :] = p_ref[r0:r0 + T, gw:gw + xbc_w]
        conv = cb_ref[...]
        for j in range(K):
            conv = conv + cw_ref[j:j + 1, :] * xbuf[s, 8 - (K - 1) + j:8 - (K - 1) + j + T, :]
        hist = xbuf[s, 8 + T - (K - 1):8 + T, :]
        xbuf[s, 8 - (K - 1):8, :] = hist
        xbc = _silu(conv)
        xs = xbc[:, :gw]
        bm = xbc[:, gw:gw + G * N]
        cm = xbc[:, gw + G * N:gw + 2 * G * N]
        z = p_ref[r0:r0 + T, 0:gw]

        dt = _softplus(p_ref[r0:r0 + T, gw + xbc_w:gw + xbc_w + LANE] + dtb_ref[...])
        if npad:
            rows = lax.broadcasted_iota(jnp.int32, (T, 1), 0)
            dt = jnp.where(jnp.logical_and(c == 0, rows < npad), 0.0, dt)
        la = dt * nalog
        cum = _dot_sel(tril, la, (((1,), (0,)), ((), ())))
        rowb = _dot_sel(headsel, cum, _NT)
        ecum = jnp.exp(cum)
        clast = cum[T - 1:T, :]
        cdec = jnp.exp(clast)
        dte = jnp.exp(clast - cum) * dt

        ys = []
        for g in range(G):
            bm_g = bm[:, g * N:(g + 1) * N].astype(_BF16)
            cm_g = cm[:, g * N:(g + 1) * N].astype(_BF16)
            cb = lax.dot_general(cm_g, bm_g, _NT, preferred_element_type=_F32)
            for hh in range(hpg):
                h = g * hpg + hh
                seg = cum[:, h:h + 1] - rowb[h * T:(h + 1) * T, :]
                lm = jnp.exp(jnp.where(causal, seg, -jnp.inf))
                xh = xs[:, h * P:(h + 1) * P]
                y_diag = jnp.dot((cb * lm).astype(_BF16), (xh * dt[:, h:h + 1]).astype(_BF16),
                                 preferred_element_type=_F32)
                hprev = hst[s, h]
                y_off = lax.dot_general(cm_g, hprev.astype(_BF16), _NT,
                                        preferred_element_type=_F32) * ecum[:, h:h + 1]
                xw = (xh * dte[:, h:h + 1]).astype(_BF16)
                hst[s, h] = hprev * cdec[:, h:h + 1] + lax.dot_general(xw, bm_g, _TN, preferred_element_type=_F32)
                ys.append(y_diag + y_off + xh * dsk_ref[:, h:h + 1])
        y = jnp.concatenate(ys, axis=-1) * _silu(z)
        outs.append(y * lax.rsqrt(jnp.mean(y * y, axis=-1, keepdims=True) + EPS) * nw_ref[...])
    y_ref[...] = (outs[0] if SB == 1 else jnp.concatenate(outs, axis=0)).astype(y_ref.dtype)

    @pl.when(c == nch - 1)
    def _fin():
        ho_ref[...] = hst[...]


def _state_spec(s0, sl, sb=1):
    tail = s0.shape[-3:]
    if sl is None:
        return pl.BlockSpec((sb,) + tail, lambda b, c: (b, 0, 0, 0))
    return pl.BlockSpec((None, sb) + tail, lambda b, c: (sl, b, 0, 0, 0))


def _seq_block(nseq, nch):
    return _pick(nseq, (4, 2, 1)) if nch == 1 else 1


def _ssd_mixer(p, h0, conv_w, conv_b, dt_bias, a_log, d_skip, norm_w, *, nseq, nch, T, npad, gw, sl=None):
    heads, P, N = h0.shape[-3:]
    K, xbc_w = conv_w.shape
    wp = p.shape[1]
    sb = 1
    padl = lambda v: jnp.pad(v.astype(_F32), (0, LANE - v.shape[0])).reshape(1, LANE)
    kern = functools.partial(_ssd_kernel, T=T, SB=sb, npad=npad, nch=nch, gw=gw, xbc_w=xbc_w, heads=heads, P=P, N=N,
                             K=K)
    return pl.pallas_call(
        kern, grid=(nseq // sb, nch),
        in_specs=[pl.BlockSpec((sb * T, wp), lambda b, c: (b * nch + c, 0)),
                  _state_spec(h0, sl, sb),
                  _full((K, xbc_w)), _full((1, xbc_w)), _full((1, LANE)), _full((1, LANE)), _full((1, LANE)),
                  _full((1, gw))],
        out_specs=[pl.BlockSpec((sb * T, gw), lambda b, c: (b * nch + c, 0)),
                   pl.BlockSpec((sb, heads, P, N), lambda b, c: (b, 0, 0, 0))],
        out_shape=[jax.ShapeDtypeStruct((nseq * nch * T, gw), _BF16),
                   jax.ShapeDtypeStruct((nseq, heads, P, N), _F32)],
        scratch_shapes=[pltpu.VMEM((sb, heads, P, N), _F32), pltpu.VMEM((sb, T + 8, xbc_w), _F32)],
        compiler_params=_cparams(("parallel", "arbitrary"), sb * T * wp * 4, 3 * sb * heads * P * N * 4,
                                 sb * T * gw * 2, sb * (T + 8) * xbc_w * 4, 24 * T * max(T, LANE) * 4),
        name="ssd_mixer",
    )(p, h0, conv_w.astype(_F32), conv_b.reshape(1, xbc_w).astype(_F32), padl(dt_bias), padl(a_log), padl(d_skip),
      norm_w.reshape(1, gw).astype(_F32))


def _ret_kernel(p_ref, cc_ref, ss_ref, r0_ref, gw_ref, gb_ref, y_ref, ro_ref, rst, *, T, SB, npad, nch, gw, heads, dk,
                dv):
    c = pl.program_id(1)
    qkw = heads * dk

    @pl.when(c == 0)
    def _init():
        rst[...] = r0_ref[...]

    npc = jnp.where(c == 0, npad, 0).astype(_F32)
    ri = lax.broadcasted_iota(jnp.int32, (T, T), 0)
    ci = lax.broadcasted_iota(jnp.int32, (T, T), 1)
    causal = ri >= ci
    dlt = (ri - ci).astype(_F32)
    idx = lax.broadcasted_iota(jnp.int32, (T, 1), 0).astype(_F32)
    cc = cc_ref[...]
    ss = ss_ref[...]
    outs = []
    for s in range(SB):
        r0 = s * T
        ys = []
        for h in range(heads):
            lg = math.log(1.0 - 2.0 ** (-5.0 - h))
            qh = p_ref[r0:r0 + T, h * dk:(h + 1) * dk]
            kh = p_ref[r0:r0 + T, qkw + h * dk:qkw + (h + 1) * dk]
            vh = p_ref[r0:r0 + T, 2 * qkw + h * dv:2 * qkw + (h + 1) * dv].astype(_BF16)
            qh = (qh * cc + pltpu.roll(qh, dk // 2, 1) * ss)
            kh = (kh * cc + pltpu.roll(kh, dk // 2, 1) * ss) * (dk ** -0.5)
            qb = qh.astype(_BF16)
            dmat = jnp.exp(jnp.where(causal, dlt * lg, -jnp.inf))
            inner = lax.dot_general(qb, kh.astype(_BF16), _NT, preferred_element_type=_F32) * dmat
            y_in = jnp.dot(inner.astype(_BF16), vh, preferred_element_type=_F32)
            rprev = rst[s, h]
            y_x = jnp.dot(qb, rprev.astype(_BF16), preferred_element_type=_F32) * jnp.exp((idx + 1.0 - npc) * lg)
            kw = (kh * jnp.exp((T - 1.0 - idx) * lg)).astype(_BF16)
            rst[s, h] = rprev * jnp.exp((T - npc) * lg) + lax.dot_general(kw, vh, _TN, preferred_element_type=_F32)
            o = y_in + y_x
            oc = o - jnp.mean(o, axis=-1, keepdims=True)
            ys.append(oc * lax.rsqrt(jnp.mean(oc * oc, axis=-1, keepdims=True) + RET_GN_EPS))
        gate = p_ref[r0:r0 + T, 2 * qkw + gw:2 * qkw + 2 * gw]
        outs.append(_silu(gate) * (jnp.concatenate(ys, axis=-1) * gw_ref[...] + gb_ref[...]))
    y_ref[...] = (outs[0] if SB == 1 else jnp.concatenate(outs, axis=0)).astype(y_ref.dtype)

    @pl.when(c == nch - 1)
    def _fin():
        ro_ref[...] = rst[...]


def _ret_mixer(p, r0, gn_w, gn_b, pos0, *, nseq, nch, T, npad, gw, sl=None):
    heads, dk, dv = r0.shape[-3:]
    wp = p.shape[1]
    theta = 1.0 / (ROPE_BASE ** jnp.linspace(0.0, 1.0, dk // 2, dtype=_F32))
    ang = (pos0 + jnp.arange(nch * T) - npad).astype(_F32)[:, None] * theta
    cos, sin = jnp.cos(ang), jnp.sin(ang)
    cc = jnp.concatenate([cos, cos], axis=-1)
    ss = jnp.concatenate([-sin, sin], axis=-1)
    sb = _seq_block(nseq, nch)
    kern = functools.partial(_ret_kernel, T=T, SB=sb, npad=npad, nch=nch, gw=gw, heads=heads, dk=dk, dv=dv)
    return pl.pallas_call(
        kern, grid=(nseq // sb, nch),
        in_specs=[pl.BlockSpec((sb * T, wp), lambda b, c: (b * nch + c, 0)),
                  pl.BlockSpec((T, dk), lambda b, c: (c, 0)), pl.BlockSpec((T, dk), lambda b, c: (c, 0)),
                  _state_spec(r0, sl, sb), _full((1, gw)), _full((1, gw))],
        out_specs=[pl.BlockSpec((sb * T, gw), lambda b, c: (b * nch + c, 0)),
                   pl.BlockSpec((sb, heads, dk, dv), lambda b, c: (b, 0, 0, 0))],
        out_shape=[jax.ShapeDtypeStruct((nseq * nch * T, gw), _BF16),
                   jax.ShapeDtypeStruct((nseq, heads, dk, dv), _F32)],
        scratch_shapes=[pltpu.VMEM((sb, heads, dk, dv), _F32)],
        compiler_params=_cparams(("parallel", "arbitrary"), sb * T * wp * 4, 3 * sb * heads * dk * dv * 4,
                                 sb * T * gw * 2, 16 * T * max(T, dv) * 4),
        name="ret_mixer",
    )(p, cc, ss, r0, gn_w.reshape(1, gw).astype(_F32), gn_b.reshape(1, gw).astype(_F32))


S5_CB = 128


def _gelu_tanh(x):
    return 0.5 * x * (1.0 + jnp.tanh(math.sqrt(2.0 / math.pi) * (x + 0.044715 * (x * x * x))))


def _s5_kernel(*refs, TC, nb, npad, nch, gw, sb, perm):
    nu = nb if perm else 1
    u_refs = refs[:nu]
    (s0r_ref, s0i_ref, are_ref, aim_ref, ldt_ref, wbr_ref, wbi_ref, wcr_ref, wci_ref, d_ref, gluw_ref, glub_ref,
     nw_ref, y_ref, sor_ref, soi_ref, xr, xi, str_, sti) = refs[nu:]
    c = pl.program_id(1)
    nblk = gw // S5_CB
    R = TC * nb

    @pl.when(c == 0)
    def _init():
        str_[...] = s0r_ref[...]
        sti[...] = s0i_ref[...]

    dt = jnp.exp(ldt_ref[...])
    lr, li = are_ref[...], aim_ref[...]
    mag = jnp.exp(lr * dt)
    abr, abi = mag * jnp.cos(li * dt), mag * jnp.sin(li * dt)
    den = lr * lr + li * li
    nr, ni = abr - 1.0, abi
    er, ei = (nr * lr + ni * li) / den, (ni * lr - nr * li) / den

    if perm:
        ti = lax.broadcasted_iota(jnp.int32, (R, R), 0)
        bi_ = lax.broadcasted_iota(jnp.int32, (R, R), 1)
        pmat = jnp.logical_and(ti // nb == bi_ % TC, ti % nb == bi_ // TC).astype(_F32)
        u = _dot_sel(pmat, jnp.concatenate([r[...] for r in u_refs], axis=0), (((1,), (0,)), ((), ())))
    else:
        u = u_refs[0][...]
    ub = u.astype(_BF16)
    for g in range(nblk):
        us = ub[:, g * S5_CB:(g + 1) * S5_CB]
        br = jnp.dot(us, wbr_ref[g], preferred_element_type=_F32)
        bi = jnp.dot(us, wbi_ref[g], preferred_element_type=_F32)
        e_r, e_i = er[:, g * sb:(g + 1) * sb], ei[:, g * sb:(g + 1) * sb]
        xr[:, g * sb:(g + 1) * sb] = e_r * br - e_i * bi
        xi[:, g * sb:(g + 1) * sb] = e_r * bi + e_i * br

    rows = max(nb, 8)
    spi = rows // nb

    def body(i, carry):
        r0 = pl.multiple_of(i * rows, 8)
        b_r, b_i = xr[pl.ds(r0, rows), :], xi[pl.ds(r0, rows), :]
        s_r, s_i = str_[...], sti[...]
        outs_r, outs_i = [], []
        for j in range(spi):
            n_r = abr * s_r - abi * s_i + b_r[j * nb:(j + 1) * nb]
            n_i = abr * s_i + abi * s_r + b_i[j * nb:(j + 1) * nb]
            s_r, s_i = n_r, n_i
            outs_r.append(n_r)
            outs_i.append(n_i)
        xr[pl.ds(r0, rows), :] = outs_r[0] if spi == 1 else jnp.concatenate(outs_r, axis=0)
        xi[pl.ds(r0, rows), :] = outs_i[0] if spi == 1 else jnp.concatenate(outs_i, axis=0)
        str_[...] = s_r
        sti[...] = s_i
        return carry

    start = jnp.clip(npad - c * TC, 0, TC) // spi
    lax.fori_loop(start, TC // spi, body, 0)

    ys = []
    for g in range(nblk):
        xrb = xr[:, g * sb:(g + 1) * sb].astype(_BF16)
        xib = xi[:, g * sb:(g + 1) * sb].astype(_BF16)
        ys.append(jnp.dot(xrb, wcr_ref[g], preferred_element_type=_F32)
                  - jnp.dot(xib, wci_ref[g], preferred_element_type=_F32))
    y = jnp.concatenate(ys, axis=-1) + d_ref[...] * u
    gy = _gelu_tanh(y)
    out = gy * jax.nn.sigmoid(jnp.dot(gy.astype(_BF16), gluw_ref[...], preferred_element_type=_F32) + glub_ref[...])
    out = (out * lax.rsqrt(jnp.mean(out * out, axis=-1, keepdims=True) + EPS) * nw_ref[...]).astype(y_ref.dtype)
    if perm:
        out = lax.dot_general(pmat.astype(out.dtype), out, _TN, preferred_element_type=_F32).astype(y_ref.dtype)
        for b in range(nb):
            y_ref[b] = out[b * TC:(b + 1) * TC]
    else:
        y_ref[...] = out

    @pl.when(c == nch - 1)
    def _fin():
        sor_ref[...] = str_[...]
        soi_ref[...] = sti[...]


def _s5_mixer(u, s0_re, s0_im, a_re, a_im, log_dt, b_re, b_im, c_re, c_im, d, glu_w, glu_b, norm_w, *,
              nseq, nch, T, npad, nb, TC, perm):
    groups, ns = a_re.shape
    gw = groups * S5_GROUP
    gpb = S5_CB // S5_GROUP
    nblk = gw // S5_CB
    sb = gpb * ns
    nst = groups * ns
    nsb = nseq // nb
    ncc = nch * T // TC
    assert npad % max(1, 8 // nb) == 0
    R = TC * nb
    if perm:
        assert nsb == 1
        u_args = [u] * nb
        u_specs = [pl.BlockSpec((TC, gw), functools.partial(lambda s, c, b: (b * ncc + c, 0), b=b)) for b in range(nb)]
        y_spec = pl.BlockSpec((nb, TC, gw), lambda s, c: (0, c, 0))
        y_shape = jax.ShapeDtypeStruct((nb, ncc * TC, gw), _BF16)
    else:
        u_args = [u[:, :gw].reshape(nsb, nb, ncc, TC, gw).transpose(0, 2, 3, 1, 4).reshape(nseq * nch * T, gw)]
        u_specs = [pl.BlockSpec((R, gw), lambda s, c: (s * ncc + c, 0))]
        y_spec = pl.BlockSpec((R, gw), lambda s, c: (s * ncc + c, 0))
        y_shape = jax.ShapeDtypeStruct((nseq * nch * T, gw), _BF16)
    eye = jnp.eye(gpb, dtype=_F32)

    def bd_in(w):
        w4 = w.reshape(nblk, gpb, ns, S5_GROUP)
        return jnp.einsum('bgnc,gh->bgchn', w4, eye).reshape(nblk, S5_CB, sb).astype(_BF16)

    def bd_out(w):
        w4 = w.reshape(nblk, gpb, S5_GROUP, ns)
        return jnp.einsum('bgcn,gh->bgnhc', w4, eye).reshape(nblk, sb, S5_CB).astype(_BF16)

    row = lambda v: v.reshape(1, -1).astype(_F32)
    kern = functools.partial(_s5_kernel, TC=TC, nb=nb, npad=npad, nch=ncc, gw=gw, sb=sb, perm=perm)
    y, so_r, so_i = pl.pallas_call(
        kern, grid=(nsb, ncc),
        in_specs=u_specs + [
            pl.BlockSpec((nb, nst), lambda s, c: (s, 0)), pl.BlockSpec((nb, nst), lambda s, c: (s, 0)),
            _full((1, nst)), _full((1, nst)), _full((1, nst)),
            _full((nblk, S5_CB, sb)), _full((nblk, S5_CB, sb)), _full((nblk, sb, S5_CB)),
            _full((nblk, sb, S5_CB)), _full((1, gw)), _full((gw, gw)), _full((1, gw)), _full((1, gw))],
        out_specs=[y_spec, pl.BlockSpec((nb, nst), lambda s, c: (s, 0)), pl.BlockSpec((nb, nst), lambda s, c: (s, 0))],
        out_shape=[y_shape, jax.ShapeDtypeStruct((nseq, nst), _F32), jax.ShapeDtypeStruct((nseq, nst), _F32)],
        scratch_shapes=[pltpu.VMEM((R, nst), _F32), pltpu.VMEM((R, nst), _F32),
                        pltpu.VMEM((nb, nst), _F32), pltpu.VMEM((nb, nst), _F32)],
        compiler_params=_cparams(("parallel", "arbitrary"), 2 * R * gw * 4, R * gw * 2, 4 * nblk * S5_CB * sb * 2,
                                 gw * gw * 2, R * nst * 4, 6 * max(nb, 8) * nst * 4),
        name="s5_mixer",
    )(*u_args, s0_re.reshape(nseq, nst), s0_im.reshape(nseq, nst), row(a_re), row(a_im),
      row(jnp.broadcast_to(log_dt[:, None], (groups, ns))), bd_in(b_re), bd_in(b_im), bd_out(c_re), bd_out(c_im),
      row(d), glu_w.astype(_BF16), row(glu_b), row(norm_w))
    if perm:
        y = y.reshape(nseq * nch * T, gw)
    else:
        y = y.reshape(nsb, ncc, TC, nb, gw).transpose(0, 3, 1, 2, 4).reshape(nseq * nch * T, gw)
    return y, so_r.reshape(nseq, groups, ns), so_i.reshape(nseq, groups, ns)


def _head_sums(x):
    ri = lax.broadcasted_iota(jnp.int32, (LANE, LANE), 0) // RWKV_HEAD_DIM
    ci = lax.broadcasted_iota(jnp.int32, (LANE, LANE), 1) // RWKV_HEAD_DIM
    e = (ri == ci).astype(_F32)
    nn = (((1,), (0,)), ((), ()))
    return jnp.concatenate([_dot_sel_r(x[:, j:j + LANE], e, nn) for j in range(0, x.shape[1], LANE)], axis=-1)


def _rwkv_pre_kernel(p_ref, mu_ref, w0_ref, w2_ref, a0_ref, a2_ref, g2_ref, kk_ref, ka_ref, rk_ref,
                     r_o, k_o, v_o, w_o, kk_o, b_o, g_o, bon_o, xbuf, *, T, gw, dl, da, dg):
    c = pl.program_id(1)

    @pl.when(c == 0)
    def _init():
        xbuf[0:8, :] = jnp.zeros((8, xbuf.shape[1]), _F32)

    p = p_ref[...]
    xbuf[8:8 + T, :] = p
    prev = xbuf[7:7 + T, :]
    last = xbuf[7 + T:8 + T, :]
    xbuf[7:8, :] = last
    pm = p + (prev - p) * mu_ref[...]
    r, k, v = pm[:, :gw], pm[:, gw:2 * gw], pm[:, 2 * gw:3 * gw]
    o1 = 3 * gw
    wl, al, gl = pm[:, o1:o1 + dl], pm[:, o1 + dl:o1 + dl + da], pm[:, o1 + dl + da:o1 + dl + da + dg]
    wx = w0_ref[...] + jnp.dot(jnp.tanh(wl).astype(_BF16), w2_ref[...], preferred_element_type=_F32)
    decay = jnp.exp(-jnp.exp(-_softplus(-wx) - 0.5))
    a = jax.nn.sigmoid(a0_ref[...] + jnp.dot(al.astype(_BF16), a2_ref[...], preferred_element_type=_F32))
    g = jnp.dot(jax.nn.sigmoid(gl).astype(_BF16), g2_ref[...], preferred_element_type=_F32)
    kkf = k * kk_ref[...]
    kk = kkf / jnp.maximum(jnp.sqrt(_head_sums(kkf * kkf)), 1e-12)
    k2 = k * (1.0 + (a - 1.0) * ka_ref[...])
    r_o[...] = r
    k_o[...] = k2
    v_o[...] = v
    w_o[...] = decay
    kk_o[...] = kk
    b_o[...] = kk * a
    g_o[...] = g
    bon_o[...] = _head_sums(r * k2 * rk_ref[...]) * v


def _rwkv_scan_kernel(r_ref, k_ref, v_ref, w_ref, kk_ref, b_ref, s0_ref, o_ref, so_ref, st, *, TC, K, npad, nch):
    c = pl.program_id(1)

    @pl.when(c == 0)
    def _init():
        st[...] = s0_ref[...]

    start = jnp.clip(npad - c * TC, 0, TC)

    @pl.when(start > 0)
    def _zero():
        o_ref[...] = jnp.zeros(o_ref.shape, _F32)

    vecs = (kk_ref, w_ref, b_ref, k_ref, r_ref)
    NACC = 4

    def tree(acc):
        return (acc[0] + acc[1]) + (acc[2] + acc[3])

    def madd(acc, k, x):
        acc[k % NACC] = x if acc[k % NACC] is None else acc[k % NACC] + x

    def sa_of(t):
        acc = [None] * NACC
        for k in range(K):
            madd(acc, k, st[k] * vecs[0][k, pl.ds(t, 1), :])
        return tree(acc)

    def body(t, sa):
        row = lambda i, k: vecs[i][k, pl.ds(t, 1), :]
        tn = jnp.minimum(t + 1, TC - 1)
        v_t = v_ref[t]
        acc_o, acc_s = [None] * NACC, [None] * NACC
        for k in range(K):
            s = st[k] * row(1, k) - sa * row(2, k) + v_t * row(3, k)
            st[k] = s
            madd(acc_o, k, s * row(4, k))
            madd(acc_s, k, s * vecs[0][k, pl.ds(tn, 1), :])
        o_ref[t] = tree(acc_o)
        return tree(acc_s)

    lax.fori_loop(start, TC, body, sa_of(jnp.minimum(start, TC - 1)))

    @pl.when(c == nch - 1)
    def _fin():
        so_ref[...] = st[...]


def _rwkv_relayout_kernel(*refs, nseq, H, K, J):
    x_refs, o_ref, zs = refs[:nseq], refs[nseq], refs[nseq + 1]
    gw = H * K
    for n in range(nseq):
        x = x_refs[n][...]
        for c0 in range(0, gw, LANE):
            zs[n * gw + c0:n * gw + c0 + LANE, :] = x[:, c0:c0 + LANE].T
    for k in range(K):
        q = zs[pl.ds(k, nseq * H, stride=K), :]
        o_ref[k] = jnp.concatenate([q] * J, axis=0).T


def _rwkv_relayout(x, *, nseq, nch, H, K, J):
    gw = H * K
    nl = J * nseq * H
    assert nl == LANE
    specs = [pl.BlockSpec((CHUNK, gw), functools.partial(lambda c, n: (n * nch + c, 0), n=n)) for n in range(nseq)]
    return pl.pallas_call(
        functools.partial(_rwkv_relayout_kernel, nseq=nseq, H=H, K=K, J=J), grid=(nch,),
        in_specs=specs, out_specs=pl.BlockSpec((K, CHUNK, nl), lambda c: (0, c, 0)),
        out_shape=jax.ShapeDtypeStruct((K, nch * CHUNK, nl), _F32),
        scratch_shapes=[pltpu.VMEM((nseq * gw, CHUNK), _F32)],
        compiler_params=_cparams(("parallel",), nseq * CHUNK * gw * 4, K * CHUNK * nl * 4, nseq * gw * CHUNK * 4),
        name="rwkv_relayout",
    )(*([x] * nseq))


def _rwkv_post_kernel(o_ref, bon_ref, g_ref, lw_ref, lb_ref, y_ref):
    o = o_ref[...]
    oc = o - _head_sums(o) * (1.0 / RWKV_HEAD_DIM)
    var = _head_sums(oc * oc) * (1.0 / RWKV_HEAD_DIM)
    y = (oc * lax.rsqrt(var + RWKV_GN_EPS) * lw_ref[...] + lb_ref[...] + bon_ref[...]) * g_ref[...]
    y_ref[...] = y.astype(y_ref.dtype)


def _rwkv_mixer(p, s0, mu_p, w0, w2_p, a0, a2_p, g2_p, k_k, k_a, r_k, ln_w, ln_b, *, nseq, nch, T, npad, gw,
                dl, da, dg, J, TC):
    H, V, K = s0.shape[1:]
    wp = p.shape[1]
    rows = nseq * nch * T
    L = nch * T
    row = lambda v: v.reshape(1, -1).astype(_F32)
    f32rows = jax.ShapeDtypeStruct((rows, gw), _F32)
    blk = pl.BlockSpec((T, gw), lambda b, c: (b * nch + c, 0))
    pre = pl.pallas_call(
        functools.partial(_rwkv_pre_kernel, T=T, gw=gw, dl=dl, da=da, dg=dg), grid=(nseq, nch),
        in_specs=[pl.BlockSpec((T, wp), lambda b, c: (b * nch + c, 0)), _full((1, wp)), _full((1, gw)),
                  _full((dl, gw)), _full((1, gw)), _full((da, gw)), _full((dg, gw)), _full((1, gw)), _full((1, gw)),
                  _full((1, gw))],
        out_specs=[blk] * 8, out_shape=[f32rows] * 8,
        scratch_shapes=[pltpu.VMEM((T + 8, wp), _F32)],
        compiler_params=_cparams(("parallel", "arbitrary"), 2 * T * wp * 4, 8 * T * gw * 4, 12 * T * gw * 4),
        name="rwkv_pre",
    )(p, row(mu_p), row(w0), w2_p.astype(_BF16), row(a0), a2_p.astype(_BF16), g2_p.astype(_BF16), row(k_k),
      row(k_a), row(r_k))
    r, k2, v, decay, kk, bvec, g, bonus = pre

    VI = V // J
    NL = J * nseq * H
    assert NL % LANE == 0 or NL < LANE

    def kvec(x):
        if T == CHUNK and NL == LANE:
            return _rwkv_relayout(x, nseq=nseq, nch=nch, H=H, K=K, J=J)
        y = x.reshape(nseq, L, H, K).transpose(3, 1, 0, 2).reshape(K, L, 1, nseq * H)
        return jnp.broadcast_to(y, (K, L, J, nseq * H)).reshape(K, L, NL)

    vv = v.reshape(nseq, L, H, J, VI).transpose(1, 4, 3, 0, 2).reshape(L, VI, NL)
    st0 = s0.reshape(nseq, H, J, VI, K).transpose(4, 3, 2, 0, 1).reshape(K, VI, NL)
    lb = min(NL, LANE)
    ncc = L // TC
    kspec = pl.BlockSpec((K, TC, lb), lambda n, c: (0, c, n))
    vspec = pl.BlockSpec((TC, VI, lb), lambda n, c: (c, 0, n))
    sspec = pl.BlockSpec((K, VI, lb), lambda n, c: (0, 0, n))
    o, st1 = pl.pallas_call(
        functools.partial(_rwkv_scan_kernel, TC=TC, K=K, npad=npad, nch=ncc), grid=(NL // lb, ncc),
        in_specs=[kspec, kspec, vspec, kspec, kspec, kspec, sspec],
        out_specs=[vspec, sspec],
        out_shape=[jax.ShapeDtypeStruct((L, VI, NL), _F32), jax.ShapeDtypeStruct((K, VI, NL), _F32)],
        scratch_shapes=[pltpu.VMEM((K, VI, lb), _F32)],
        compiler_params=_cparams(("parallel", "arbitrary"), 5 * TC * K * lb * 4, 2 * TC * VI * lb * 4,
                                 3 * VI * K * lb * 4),
        name="rwkv_scan",
    )(kvec(r), kvec(k2), vv, kvec(decay), kvec(kk), kvec(bvec), st0)
    o = o.reshape(L, VI, J, nseq, H).transpose(3, 0, 4, 2, 1).reshape(rows, gw)
    s_new = st1.reshape(K, VI, J, nseq, H).transpose(3, 4, 2, 1, 0).reshape(nseq, H, V, K)

    tr = _pick(rows, (256, 128, 64, 32, 16, 8))
    rblk = pl.BlockSpec((tr, gw), lambda i: (i, 0))
    y = pl.pallas_call(
        _rwkv_post_kernel, grid=(rows // tr,),
        in_specs=[rblk, rblk, rblk, _full((1, gw)), _full((1, gw))],
        out_specs=rblk, out_shape=jax.ShapeDtypeStruct((rows, gw), _BF16),
        compiler_params=_cparams(("parallel",), 4 * tr * gw * 4, 8 * tr * gw * 4),
        name="rwkv_post",
    )(o, bonus, g, row(ln_w), row(ln_b))
    return y, s_new


def kernel(x_prompt, x_sample, state_ssd, state_ssd_conv, state_rwkv, state_rwkv_shift, state_ret, state_s5_re,
           state_s5_im, meta, ln_mix, w_in, ssd_conv_w, ssd_conv_b, ssd_dt_bias, ssd_a_log, ssd_d, ssd_norm, rwkv_mu,
           rwkv_w0, rwkv_w2, rwkv_a0, rwkv_a2, rwkv_g2, rwkv_k_k, rwkv_k_a, rwkv_r_k, rwkv_ln_w, rwkv_ln_b, ret_gn_w,
           ret_gn_b, s5_a_re, s5_a_im, s5_log_dt, s5_b_re, s5_b_im, s5_c_re, s5_c_im, s5_d, s5_glu_w, s5_glu_b,
           s5_norm, w_out, ln_ffn, w_gate, w_up, w_down, ln_f):
    bp, sp, d = x_prompt.shape
    bs, ls, _ = x_sample.shape
    depth = w_in.shape[0]
    gw = d // 4
    ssd_heads, ssd_p, ssd_n = state_ssd.shape[2:]
    xbc_w = state_ssd_conv.shape[-1]
    rwkv_proj = state_rwkv_shift.shape[-1]
    ret_heads, ret_dk, ret_dv = state_ret.shape[2:]
    qkw = ret_heads * ret_dk
    dff = w_gate.shape[-1]

    lreal = N_META + sp
    lp = _round_up(lreal, CHUNK)
    pad = lp - lreal
    mp, ms = bp * lp, bs * ls
    m = mp + ms

    head = jnp.concatenate([jnp.zeros((pad, d), _F32), meta.astype(_F32)], axis=0)
    h = jnp.concatenate([piece for b in range(bp) for piece in (head, x_prompt[b])] + [x_sample.reshape(ms, d)],
                        axis=0)

    in_splits = (gw, xbc_w, ssd_heads, rwkv_proj, qkw, qkw, gw, gw, gw)
    offs = [0]
    for s in in_splits:
        offs.append(offs[-1] + s)
    ntile = 512 if gw % 512 == 0 else LANE

    layers = range(depth)

    lora_c = (rwkv_w2.shape[1], rwkv_a2.shape[1], rwkv_g2.shape[1])
    dl, da, dg = (_round_up(n, LANE) for n in lora_c)
    rw_w = _round_up(3 * gw + dl + da + dg, ntile)

    def rw_pad(x):
        parts, o = [x[..., :3 * gw]], 3 * gw
        for n, npd in zip(lora_c, (dl, da, dg)):
            parts.append(jnp.pad(x[..., o:o + n], [(0, 0)] * (x.ndim - 1) + [(0, npd - n)]))
            o += n
        y = jnp.concatenate(parts, axis=-1)
        return jnp.pad(y, [(0, 0)] * (x.ndim - 1) + [(0, rw_w - y.shape[-1])])

    def rw_compact(x):
        parts, o = [x[..., :3 * gw]], 3 * gw
        for n, npd in zip(lora_c, (dl, da, dg)):
            parts.append(x[..., o:o + n])
            o += npd
        return jnp.concatenate(parts, axis=-1)

    assert (gw + xbc_w) % LANE == 0 and gw % LANE == 0
    o_rw = offs[3] + 3 * gw
    plans = (
        ((0, 0, gw + xbc_w), (gw + xbc_w, gw + xbc_w, ssd_heads)),
        ((0, offs[3], 3 * gw), (3 * gw, o_rw, lora_c[0]), (3 * gw + dl, o_rw + lora_c[0], lora_c[1]),
         (3 * gw + dl + da, o_rw + lora_c[0] + lora_c[1], lora_c[2])),
        ((0, offs[4], offs[8] - offs[4]),),
        ((0, offs[8], gw),),
    )
    widths = (_round_up(gw + xbc_w + LANE, ntile), rw_w, _round_up(offs[8] - offs[4], ntile), _round_up(gw, ntile))
    packed = [_pack_w_in(w_in, l, plans, widths) for l in layers]
    w_ssd_b, w_rw_b, w_ret_b, w_s5_b = ([p[i] for p in packed] for i in range(4))
    mu_p = rw_pad(rwkv_mu)
    padrows = lambda w, n: jnp.pad(w, ((0, 0), (0, n - w.shape[1]), (0, 0)))
    w2_p, a2_p, g2_p = padrows(rwkv_w2, dl), padrows(rwkv_a2, da), padrows(rwkv_g2, dg)
    w_out_b, w_gate_b, w_up_b, w_down_b = (w.astype(_BF16) for w in (w_out, w_gate, w_up, w_down))
    half = dff // 2

    zeros_b = lambda s: jnp.zeros((bp,) + s.shape[2:], _F32)
    nch_p = lp // CHUNK
    ts = _round_up(ls + 3, 8)
    spad = ts - ls

    def sample_rows(rows, hist=None, c0=0):
        r3 = jnp.pad(rows.reshape(bs, ls, -1), ((0, 0), (spad, 0), (0, 0)))
        if hist is not None:
            k, w = hist.shape[1:]
            r3 = r3.at[:, spad - k:spad, c0:c0 + w].set(hist)
        return r3.reshape(bs * ts, -1)

    def unsample(y):
        return y.reshape(bs, ts, -1)[:, spad:].reshape(ms, -1)

    def last_rows(rows, k):
        return jnp.stack([rows[(b + 1) * lp - k:(b + 1) * lp] for b in range(bp)])

    outs_p = [[] for _ in range(7)]
    outs_s = [[] for _ in range(7)]
    for l in range(depth):
        hn = _rmsnorm(h, ln_mix[l], _BF16, lp=lp, pad=pad, mp=mp)

        pa = _mm(hn, w_ssd_b[l], name="in_proj_ssd")
        ssd_w = (ssd_conv_w[l], ssd_conv_b[l], ssd_dt_bias[l], ssd_a_log[l], ssd_d[l], ssd_norm[l])
        ya_p, hp_new = _ssd_mixer(pa, zeros_b(state_ssd), *ssd_w, nseq=bp, nch=nch_p, T=CHUNK, npad=pad, gw=gw)
        pa_s = sample_rows(pa[mp:], state_ssd_conv[l], gw)
        ya_s, hs_new = _ssd_mixer(pa_s, state_ssd, *ssd_w, nseq=bs, nch=1, T=ts, npad=spad, gw=gw, sl=l)
        kc = ssd_conv_w.shape[1] - 1
        outs_p[0].append(hp_new)
        outs_p[1].append(last_rows(pa, kc)[..., gw:gw + xbc_w])
        outs_s[0].append(hs_new)
        outs_s[1].append(pa_s.reshape(bs, ts, -1)[:, ts - kc:, gw:gw + xbc_w])

        pb = _mm(hn, w_rw_b[l], name="in_proj_rwkv")
        rw_args = (mu_p[l], rwkv_w0[l], w2_p[l], rwkv_a0[l], a2_p[l], g2_p[l], rwkv_k_k[l], rwkv_k_a[l],
                   rwkv_r_k[l], rwkv_ln_w[l], rwkv_ln_b[l])
        rw_kw = dict(gw=gw, dl=dl, da=da, dg=dg)
        rw_heads = state_rwkv.shape[2]
        yb_p, sp_new = _rwkv_mixer(pb, zeros_b(state_rwkv), *rw_args, nseq=bp, nch=nch_p, T=CHUNK, npad=pad,
                                   J=max(1, LANE // (bp * rw_heads)), TC=CHUNK // 4, **rw_kw)
        pb_s = sample_rows(pb[mp:], rw_pad(state_rwkv_shift[l])[:, None, :], 0)
        yb_s, ss_new = _rwkv_mixer(pb_s, state_rwkv[l], *rw_args, nseq=bs, nch=1, T=ts, npad=spad,
                                   J=max(1, LANE // (bs * rw_heads)), TC=ts, **rw_kw)
        outs_p[2].append(sp_new)
        outs_p[3].append(rw_compact(last_rows(pb, 1)[:, 0]))
        outs_s[2].append(ss_new)
        outs_s[3].append(rw_compact(pb_s.reshape(bs, ts, -1)[:, -1]))

        pc = _mm(hn, w_ret_b[l], name="in_proj_ret")
        yc_p, rp_new = _ret_mixer(pc, zeros_b(state_ret), ret_gn_w[l], ret_gn_b[l], 0,
                                  nseq=bp, nch=nch_p, T=CHUNK, npad=pad, gw=gw)
        yc_s, rs_new = _ret_mixer(sample_rows(pc[mp:]), state_ret, ret_gn_w[l], ret_gn_b[l], PAST_LEN,
                                  nseq=bs, nch=1, T=ts, npad=spad, gw=gw, sl=l)
        outs_p[4].append(rp_new)
        outs_s[4].append(rs_new)

        pd = _mm(hn, w_s5_b[l], name="in_proj_s5")
        s5_w = (s5_a_re[l], s5_a_im[l], s5_log_dt[l], s5_b_re[l], s5_b_im[l], s5_c_re[l], s5_c_im[l], s5_d[l],
                s5_glu_w[l], s5_glu_b[l], s5_norm[l])
        yd_p, s5r_p, s5i_p = _s5_mixer(pd, zeros_b(state_s5_re), zeros_b(state_s5_im), *s5_w, nseq=bp, nch=nch_p,
                                       T=CHUNK, npad=pad, nb=bp, TC=CHUNK // 2, perm=True)
        yd_s, s5r_s, s5i_s = _s5_mixer(sample_rows(pd[mp:]), state_s5_re[l], state_s5_im[l], *s5_w, nseq=bs, nch=1,
                                       T=ts, npad=spad, nb=min(bs, 32), TC=ts, perm=False)
        outs_p[5].append(s5r_p)
        outs_p[6].append(s5i_p)
        outs_s[5].append(s5r_s)
        outs_s[6].append(s5i_s)

        h = _outproj((ya_p, yb_p, yc_p, yd_p), w_out_b, h, 0, wl=l)
        h = _outproj(tuple(unsample(y) for y in (ya_s, yb_s, yc_s, yd_s)), w_out_b, h, mp, wl=l)
        hn = _rmsnorm(h, ln_ffn[l], _BF16)
        ff = _swiglu(hn, w_gate_b, w_up_b, wl=l)
        h = _mm(ff, w_down_b, res=h, kb=0, tk=half, name="ffn_down0", wl=l)
        h = _mm(ff, w_down_b, res=h, kb=1, tk=half, name="ffn_down1", wl=l)

    y_prompt, y_sample = _final_norm(h, ln_f, bp=bp, sp=sp, lp=lp, ms=ms)
    return ((y_prompt.reshape(bp, sp, d), y_sample.reshape(bs, ls, d))
            + tuple(jnp.stack(a) for a in outs_p) + tuple(jnp.stack(a) for a in outs_s))
```

```python
import functools
import math

import jax
import jax.numpy as jnp
from jax import lax
from jax.experimental import pallas as pl
from jax.experimental.pallas import tpu as pltpu

N_META = 16
EPS = 1e-6
SSD_GROUPS = 2
RWKV_HEAD_DIM = 64
RWKV_GN_EPS = 64e-5
RET_GN_EPS = 1e-5
ROPE_BASE = 10000.0
S5_GROUP = 16
PAST_LEN = 16384
CHUNK = 128
LANE = 128
V7X_VMEM_CAP = 60 * 1024 * 1024

_F32 = jnp.float32
_BF16 = jnp.bfloat16


def _pick(n, cands):
    for c in cands:
        if n % c == 0:
            return c
    raise ValueError(f"no tile in {cands} divides {n}")


def _round_up(n, m):
    return -(-n // m) * m


def _cparams(sem, *block_bytes):
    need = 2 * sum(block_bytes) + (6 << 20)
    return pltpu.CompilerParams(dimension_semantics=sem, vmem_limit_bytes=int(min(max(need, 16 << 20), V7X_VMEM_CAP)))


def _rmsnorm_kernel(x_ref, g_ref, o_ref, *, tr, lp, pad, mp):
    x = x_ref[...]
    y = x * lax.rsqrt(jnp.mean(x * x, axis=-1, keepdims=True) + EPS) * g_ref[...]
    if pad:
        row0 = pl.program_id(0) * tr
        pos0 = lax.rem(row0, lp)
        rows = lax.broadcasted_iota(jnp.int32, (tr, 1), 0)
        is_pad = jnp.logical_and(row0 < mp, pos0 + rows < pad)
        y = jnp.where(is_pad, 0.0, y)
    o_ref[...] = y.astype(o_ref.dtype)


def _rmsnorm(x, g, out_dtype, lp=0, pad=0, mp=0):
    m, d = x.shape
    tr = _pick(math.gcd(m, lp) if pad else m, (256, 128, 64, 32, 16, 8))
    kern = functools.partial(_rmsnorm_kernel, tr=tr, lp=lp, pad=pad, mp=mp)
    return pl.pallas_call(
        kern, grid=(m // tr,),
        in_specs=[pl.BlockSpec((tr, d), lambda i: (i, 0)), pl.BlockSpec((1, d), lambda i: (0, 0))],
        out_specs=pl.BlockSpec((tr, d), lambda i: (i, 0)),
        out_shape=jax.ShapeDtypeStruct((m, d), out_dtype),
        compiler_params=_cparams(("parallel",), tr * d * 4, tr * d * 4),
        name="rmsnorm",
    )(x, g.reshape(1, d).astype(_F32))


def _final_norm(h, g, *, bp, sp, lp, ms):
    d = h.shape[1]
    assert lp - sp == CHUNK and sp % CHUNK == 0
    nch, nout = lp // CHUNK, sp // CHUNK
    kern = functools.partial(_rmsnorm_kernel, tr=CHUNK, lp=0, pad=0, mp=0)
    g2 = g.reshape(1, d).astype(_F32)
    yp = pl.pallas_call(
        kern, grid=(bp, nout),
        in_specs=[pl.BlockSpec((CHUNK, d), lambda b, j: (b * nch + 1 + j, 0)), pl.BlockSpec((1, d), lambda b, j: (0, 0))],
        out_specs=pl.BlockSpec((CHUNK, d), lambda b, j: (b * nout + j, 0)),
        out_shape=jax.ShapeDtypeStruct((bp * sp, d), _F32),
        compiler_params=_cparams(("parallel", "parallel"), CHUNK * d * 4, CHUNK * d * 4),
        name="final_norm_prompt",
    )(h, g2)
    tr = _pick(math.gcd(ms, bp * lp), (256, 128, 64, 32, 16, 8))
    off = bp * lp // tr
    ys = pl.pallas_call(
        functools.partial(_rmsnorm_kernel, tr=tr, lp=0, pad=0, mp=0), grid=(ms // tr,),
        in_specs=[pl.BlockSpec((tr, d), lambda i: (off + i, 0)), pl.BlockSpec((1, d), lambda i: (0, 0))],
        out_specs=pl.BlockSpec((tr, d), lambda i: (i, 0)),
        out_shape=jax.ShapeDtypeStruct((ms, d), _F32),
        compiler_params=_cparams(("parallel",), tr * d * 4, tr * d * 4),
        name="final_norm_sample",
    )(h, g2)
    return yp, ys


def _pack_kernel(w_ref, *o_refs, plans):
    x = w_ref[...]
    tr = x.shape[0]
    for o_ref, plan in zip(o_refs, plans):
        done = 0
        for dst, src, n in plan:
            npd = _round_up(n, LANE)
            assert dst % LANE == 0 and dst >= done
            if dst > done:
                o_ref[:, done:dst] = jnp.zeros((tr, dst - done), o_ref.dtype)
            seg = x[:, src:src + npd]
            if npd > n:
                seg = jnp.where(lax.broadcasted_iota(jnp.int32, (tr, npd), 1) < n, seg, 0.0)
            o_ref[:, dst:dst + npd] = seg.astype(o_ref.dtype)
            done = dst + npd
        if done < o_ref.shape[1]:
            o_ref[:, done:] = jnp.zeros((tr, o_ref.shape[1] - done), o_ref.dtype)


def _pack_w_in(w_in, l, plans, widths):
    _, d, nin = w_in.shape
    assert all(src + _round_up(n, LANE) <= nin for plan in plans for _, src, n in plan)
    tr = _pick(d, (128, 64, 32, 16, 8))
    return pl.pallas_call(
        functools.partial(_pack_kernel, plans=plans), grid=(d // tr,),
        in_specs=[pl.BlockSpec((tr, nin), lambda i: (l * (d // tr) + i, 0))],
        out_specs=[pl.BlockSpec((tr, w), lambda i: (i, 0)) for w in widths],
        out_shape=[jax.ShapeDtypeStruct((d, w), _BF16) for w in widths],
        compiler_params=_cparams(("parallel",), 2 * tr * nin * 4, tr * sum(widths) * 2),
        name="pack_w_in",
    )(w_in.reshape(-1, nin))


def _mm_kernel(*refs, has_res):
    if has_res:
        x_ref, w_ref, r_ref, o_ref = refs
    else:
        x_ref, w_ref, o_ref = refs
    acc = jnp.dot(x_ref[...], w_ref[...], preferred_element_type=_F32)
    if has_res:
        acc = acc + r_ref[...]
    o_ref[...] = acc.astype(o_ref.dtype)


def _wspec(w, wl, rows, cols, imap):
    if w.ndim == 2:
        return pl.BlockSpec((rows, cols), imap)
    return pl.BlockSpec((None, rows, cols), lambda i, j: (wl,) + imap(i, j))


def _mm(x, w, res=None, out_dtype=_F32, kb=0, tk=None, name="matmul", wl=None):
    m = x.shape[0]
    n = w.shape[-1]
    tk = tk or w.shape[-2]
    tm = _pick(m, (1024, 512, 256, 128))
    tn = _pick(n, (512, 256, 128))
    in_specs = [pl.BlockSpec((tm, tk), lambda i, j: (i, kb)), _wspec(w, wl, tk, tn, lambda i, j: (kb, j))]
    args = [x, w]
    blocks = [tm * tk * 2, tk * tn * 2, tm * tn * 4]
    if res is not None:
        in_specs.append(pl.BlockSpec((tm, tn), lambda i, j: (i, j)))
        args.append(res)
        blocks.append(tm * tn * 4)
    return pl.pallas_call(
        functools.partial(_mm_kernel, has_res=res is not None), grid=(m // tm, n // tn),
        in_specs=in_specs, out_specs=pl.BlockSpec((tm, tn), lambda i, j: (i, j)),
        out_shape=jax.ShapeDtypeStruct((m, n), out_dtype),
        compiler_params=_cparams(("parallel", "arbitrary"), *blocks),
        name=name,
    )(*args)


def _outproj_kernel(*refs, nmix, gw):
    xs = refs[:nmix]
    w_ref, r_ref, o_ref = refs[nmix:]
    acc = r_ref[...]
    for j, x in enumerate(xs):
        acc = acc + jnp.dot(x[...], w_ref[j * gw:(j + 1) * gw, :], preferred_element_type=_F32)
    o_ref[...] = acc


def _outproj(ys, w, h, row0, wl=None):
    nrows, gw = ys[0].shape
    nmix = len(ys)
    d, n = w.shape[-2:]
    tm = _pick(math.gcd(nrows, row0) if row0 else nrows, (1088, 1024, 512, 256, 128))
    tn = _pick(n, (512, 256, 128))
    blk0 = row0 // tm
    hspec = pl.BlockSpec((tm, tn), lambda i, j: (blk0 + i, j))
    return pl.pallas_call(
        functools.partial(_outproj_kernel, nmix=nmix, gw=gw), grid=(nrows // tm, n // tn),
        in_specs=[pl.BlockSpec((tm, gw), lambda i, j: (i, 0))] * nmix + [_wspec(w, wl, d, tn, lambda i, j: (0, j)),
                                                                          hspec],
        out_specs=hspec, out_shape=jax.ShapeDtypeStruct(h.shape, _F32),
        input_output_aliases={nmix + 1: 0},
        compiler_params=_cparams(("parallel", "arbitrary"), nmix * tm * gw * 2, d * tn * 2, 2 * tm * tn * 4),
        name="out_proj",
    )(*ys, w, h)


def _swiglu_kernel(x_ref, wg_ref, wu_ref, o_ref):
    x = x_ref[...]
    g = jnp.dot(x, wg_ref[...], preferred_element_type=_F32)
    u = jnp.dot(x, wu_ref[...], preferred_element_type=_F32)
    o_ref[...] = (g * jax.nn.sigmoid(g) * u).astype(o_ref.dtype)


def _swiglu(x, wg, wu, wl=None):
    m, k = x.shape
    n = wg.shape[-1]
    tm = _pick(m, (1024, 512, 256, 128))
    tn = _pick(n, (256, 128))
    return pl.pallas_call(
        _swiglu_kernel, grid=(m // tm, n // tn),
        in_specs=[pl.BlockSpec((tm, k), lambda i, j: (i, 0)), _wspec(wg, wl, k, tn, lambda i, j: (0, j)),
                  _wspec(wu, wl, k, tn, lambda i, j: (0, j))],
        out_specs=pl.BlockSpec((tm, tn), lambda i, j: (i, j)),
        out_shape=jax.ShapeDtypeStruct((m, n), _BF16),
        compiler_params=_cparams(("parallel", "arbitrary"), tm * k * 2, 2 * k * tn * 2, tm * tn * 2),
        name="swiglu",
    )(x, wg, wu)


_NT = (((1,), (1,)), ((), ()))
_TN = (((0,), (0,)), ((), ()))


def _silu(x):
    return x * jax.nn.sigmoid(x)


def _softplus(x):
    return jnp.maximum(x, 0.0) + jnp.log(1.0 + jnp.exp(-jnp.abs(x)))


def _split3(x):
    hi = x.astype(_BF16)
    r = x - hi.astype(_F32)
    mid = r.astype(_BF16)
    lo = (r - mid.astype(_F32)).astype(_BF16)
    return hi, mid, lo


def _dot_sel(sel, x, dims):
    parts = _split3(x) if _BF16 == jnp.bfloat16 else (x,)
    out = None
    for p in parts:
        t = lax.dot_general(sel.astype(p.dtype), p, dims, preferred_element_type=_F32)
        out = t if out is None else out + t
    return out


def _dot_sel_r(x, sel, dims):
    parts = _split3(x) if _BF16 == jnp.bfloat16 else (x,)
    out = None
    for p in parts:
        t = lax.dot_general(p, sel.astype(p.dtype), dims, preferred_element_type=_F32)
        out = t if out is None else out + t
    return out


def _full(shape):
    nd = len(shape)
    return pl.BlockSpec(shape, lambda *_: (0,) * nd)


def _ssd_kernel(p_ref, h0_ref, cw_ref, cb_ref, dtb_ref, alog_ref, dsk_ref, nw_ref, y_ref, ho_ref, hst, xbuf, *,
                T, SB, npad, nch, gw, xbc_w, heads, P, N, K):
    c = pl.program_id(1)
    G = SSD_GROUPS
    hpg = heads // G

    @pl.when(c == 0)
    def _init():
        hst[...] = h0_ref[...]
        xbuf[:, 0:8, :] = jnp.zeros((SB, 8, xbc_w), _F32)

    ri = lax.broadcasted_iota(jnp.int32, (T, T), 0)
    ci = lax.broadcasted_iota(jnp.int32, (T, T), 1)
    causal = ri >= ci
    tril = causal.astype(_F32)
    hq = lax.broadcasted_iota(jnp.int32, (heads * T, LANE), 0) // T
    ln = lax.broadcasted_iota(jnp.int32, (heads * T, LANE), 1)
    headsel = (ln == hq).astype(_F32)
    er = lax.broadcasted_iota(jnp.int32, (LANE, gw), 0)
    ec = lax.broadcasted_iota(jnp.int32, (LANE, gw), 1) // P
    headexp = (er == ec).astype(_F32)
    nalog = -jnp.exp(alog_ref[...])

    outs = []
    for s in range(SB):
        r0 = s * T
        xbuf[s, 8:8 + T, :] = p_ref[r0:r0 + T, gw:gw + xbc_w]
        conv = cb_ref[...]
        for j in range(K):
            conv = conv + cw_ref[j:j + 1, :] * xbuf[s, 8 - (K - 1) + j:8 - (K - 1) + j + T, :]
        hist = xbuf[s, 8 + T - (K - 1):8 + T, :]
        xbuf[s, 8 - (K - 1):8, :] = hist
        xbc = _silu(conv)
        xs = xbc[:, :gw]
        bm = xbc[:, gw:gw + G * N]
        cm = xbc[:, gw + G * N:gw + 2 * G * N]
        z = p_ref[r0:r0 + T, 0:gw]

        dt = _softplus(p_ref[r0:r0 + T, gw + xbc_w:gw + xbc_w + LANE] + dtb_ref[...])
        if npad:
            rows = lax.broadcasted_iota(jnp.int32, (T, 1), 0)
            dt = jnp.where(jnp.logical_and(c == 0, rows < npad), 0.0, dt)
        la = dt * nalog
        cum = _dot_sel(tril, la, (((1,), (0,)), ((), ())))
        rowb = _dot_sel(headsel, cum, _NT)
        clast = cum[T - 1:T, :]
        cdec = jnp.exp(clast)
        nn = (((1,), (0,)), ((), ()))
        xdt = xs * _dot_sel_r(dt, headexp, nn)
        xdte = xs * _dot_sel_r(jnp.exp(clast - cum) * dt, headexp, nn)
        ecum_x = _dot_sel_r(jnp.exp(cum), headexp, nn)

        ys = []
        for g in range(G):
            bm_g = bm[:, g * N:(g + 1) * N].astype(_BF16)
            cm_g = cm[:, g * N:(g + 1) * N].astype(_BF16)
            cb = lax.dot_general(cm_g, bm_g, _NT, preferred_element_type=_F32)
            for hh in range(hpg):
                h = g * hpg + hh
                hs = slice(h * P, (h + 1) * P)
                seg = cum[:, h:h + 1] - rowb[h * T:(h + 1) * T, :]
                lm = jnp.exp(jnp.where(causal, seg, -jnp.inf))
                y_diag = jnp.dot((cb * lm).astype(_BF16), xdt[:, hs].astype(_BF16), preferred_element_type=_F32)
                hprev = hst[s, h]
                y_off = lax.dot_general(cm_g, hprev.astype(_BF16), _NT, preferred_element_type=_F32) * ecum_x[:, hs]
                hst[s, h] = hprev * cdec[:, h:h + 1] + lax.dot_general(xdte[:, hs].astype(_BF16), bm_g, _TN,
                                                                       preferred_element_type=_F32)
                ys.append(y_diag + y_off)
        y = (jnp.concatenate(ys, axis=-1) + xs * dsk_ref[...]) * _silu(z)
        outs.append(y * lax.rsqrt(jnp.mean(y * y, axis=-1, keepdims=True) + EPS) * nw_ref[...])
    y_ref[...] = (outs[0] if SB == 1 else jnp.concatenate(outs, axis=0)).astype(y_ref.dtype)

    @pl.when(c == nch - 1)
    def _fin():
        ho_ref[...] = hst[...]


def _state_spec(s0, sl, sb=1):
    tail = s0.shape[-3:]
    if sl is None:
        return pl.BlockSpec((sb,) + tail, lambda b, c: (b, 0, 0, 0))
    return pl.BlockSpec((None, sb) + tail, lambda b, c: (sl, b, 0, 0, 0))


def _seq_block(nseq, nch):
    return _pick(nseq, (4, 2, 1)) if nch == 1 else 1


def _ssd_mixer(p, h0, conv_w, conv_b, dt_bias, a_log, d_skip, norm_w, *, nseq, nch, T, npad, gw, sl=None):
    heads, P, N = h0.shape[-3:]
    K, xbc_w = conv_w.shape
    wp = p.shape[1]
    sb = 1
    padl = lambda v: jnp.pad(v.astype(_F32), (0, LANE - v.shape[0])).reshape(1, LANE)
    kern = functools.partial(_ssd_kernel, T=T, SB=sb, npad=npad, nch=nch, gw=gw, xbc_w=xbc_w, heads=heads, P=P, N=N,
                             K=K)
    return pl.pallas_call(
        kern, grid=(nseq // sb, nch),
        in_specs=[pl.BlockSpec((sb * T, wp), lambda b, c: (b * nch + c, 0)),
                  _state_spec(h0, sl, sb),
                  _full((K, xbc_w)), _full((1, xbc_w)), _full((1, LANE)), _full((1, LANE)), _full((1, gw)),
                  _full((1, gw))],
        out_specs=[pl.BlockSpec((sb * T, gw), lambda b, c: (b * nch + c, 0)),
                   pl.BlockSpec((sb, heads, P, N), lambda b, c: (b, 0, 0, 0))],
        out_shape=[jax.ShapeDtypeStruct((nseq * nch * T, gw), _BF16),
                   jax.ShapeDtypeStruct((nseq, heads, P, N), _F32)],
        scratch_shapes=[pltpu.VMEM((sb, heads, P, N), _F32), pltpu.VMEM((sb, T + 8, xbc_w), _F32)],
        compiler_params=_cparams(("parallel", "arbitrary"), sb * T * wp * 4, 3 * sb * heads * P * N * 4,
                                 sb * T * gw * 2, sb * (T + 8) * xbc_w * 4, 24 * T * max(T, LANE) * 4),
        name="ssd_mixer",
    )(p, h0, conv_w.astype(_F32), conv_b.reshape(1, xbc_w).astype(_F32), padl(dt_bias), padl(a_log),
      jnp.repeat(d_skip.astype(_F32), P).reshape(1, gw),
      norm_w.reshape(1, gw).astype(_F32))


def _ret_kernel(p_ref, cc_ref, ss_ref, r0_ref, gw_ref, gb_ref, y_ref, ro_ref, rst, *, T, SB, npad, nch, gw, heads, dk,
                dv):
    c = pl.program_id(1)
    qkw = heads * dk

    @pl.when(c == 0)
    def _init():
        rst[...] = r0_ref[...]

    npc = jnp.where(c == 0, npad, 0).astype(_F32)
    ri = lax.broadcasted_iota(jnp.int32, (T, T), 0)
    ci = lax.broadcasted_iota(jnp.int32, (T, T), 1)
    causal = ri >= ci
    dlt = (ri - ci).astype(_F32)
    idx = lax.broadcasted_iota(jnp.int32, (T, 1), 0).astype(_F32)
    cc = cc_ref[...]
    ss = ss_ref[...]
    outs = []
    for s in range(SB):
        r0 = s * T
        ys = []
        for h in range(heads):
            lg = math.log(1.0 - 2.0 ** (-5.0 - h))
            qh = p_ref[r0:r0 + T, h * dk:(h + 1) * dk]
            kh = p_ref[r0:r0 + T, qkw + h * dk:qkw + (h + 1) * dk]
            vh = p_ref[r0:r0 + T, 2 * qkw + h * dv:2 * qkw + (h + 1) * dv].astype(_BF16)
            qh = (qh * cc + pltpu.roll(qh, dk // 2, 1) * ss)
            kh = (kh * cc + pltpu.roll(kh, dk // 2, 1) * ss) * (dk ** -0.5)
            qb = qh.astype(_BF16)
            dmat = jnp.exp(jnp.where(causal, dlt * lg, -jnp.inf))
            inner = lax.dot_general(qb, kh.astype(_BF16), _NT, preferred_element_type=_F32) * dmat
            y_in = jnp.dot(inner.astype(_BF16), vh, preferred_element_type=_F32)
            rprev = rst[s, h]
            y_x = jnp.dot(qb, rprev.astype(_BF16), preferred_element_type=_F32) * jnp.exp((idx + 1.0 - npc) * lg)
            kw = (kh * jnp.exp((T - 1.0 - idx) * lg)).astype(_BF16)
            rst[s, h] = rprev * jnp.exp((T - npc) * lg) + lax.dot_general(kw, vh, _TN, preferred_element_type=_F32)
            o = y_in + y_x
            oc = o - jnp.mean(o, axis=-1, keepdims=True)
            ys.append(oc * lax.rsqrt(jnp.mean(oc * oc, axis=-1, keepdims=True) + RET_GN_EPS))
        gate = p_ref[r0:r0 + T, 2 * qkw + gw:2 * qkw + 2 * gw]
        outs.append(_silu(gate) * (jnp.concatenate(ys, axis=-1) * gw_ref[...] + gb_ref[...]))
    y_ref[...] = (outs[0] if SB == 1 else jnp.concatenate(outs, axis=0)).astype(y_ref.dtype)

    @pl.when(c == nch - 1)
    def _fin():
        ro_ref[...] = rst[...]


def _ret_mixer(p, r0, gn_w, gn_b, pos0, *, nseq, nch, T, npad, gw, sl=None):
    heads, dk, dv = r0.shape[-3:]
    wp = p.shape[1]
    theta = 1.0 / (ROPE_BASE ** jnp.linspace(0.0, 1.0, dk // 2, dtype=_F32))
    ang = (pos0 + jnp.arange(nch * T) - npad).astype(_F32)[:, None] * theta
    cos, sin = jnp.cos(ang), jnp.sin(ang)
    cc = jnp.concatenate([cos, cos], axis=-1)
    ss = jnp.concatenate([-sin, sin], axis=-1)
    sb = _seq_block(nseq, nch)
    kern = functools.partial(_ret_kernel, T=T, SB=sb, npad=npad, nch=nch, gw=gw, heads=heads, dk=dk, dv=dv)
    return pl.pallas_call(
        kern, grid=(nseq // sb, nch),
        in_specs=[pl.BlockSpec((sb * T, wp), lambda b, c: (b * nch + c, 0)),
                  pl.BlockSpec((T, dk), lambda b, c: (c, 0)), pl.BlockSpec((T, dk), lambda b, c: (c, 0)),
                  _state_spec(r0, sl, sb), _full((1, gw)), _full((1, gw))],
        out_specs=[pl.BlockSpec((sb * T, gw), lambda b, c: (b * nch + c, 0)),
                   pl.BlockSpec((sb, heads, dk, dv), lambda b, c: (b, 0, 0, 0))],
        out_shape=[jax.ShapeDtypeStruct((nseq * nch * T, gw), _BF16),
                   jax.ShapeDtypeStruct((nseq, heads, dk, dv), _F32)],
        scratch_shapes=[pltpu.VMEM((sb, heads, dk, dv), _F32)],
        compiler_params=_cparams(("parallel", "arbitrary"), sb * T * wp * 4, 3 * sb * heads * dk * dv * 4,
                                 sb * T * gw * 2, 16 * T * max(T, dv) * 4),
        name="ret_mixer",
    )(p, cc, ss, r0, gn_w.reshape(1, gw).astype(_F32), gn_b.reshape(1, gw).astype(_F32))


S5_CB = 128


def _gelu_tanh(x):
    return 0.5 * x * (1.0 + jnp.tanh(math.sqrt(2.0 / math.pi) * (x + 0.044715 * (x * x * x))))


def _s5_kernel(*refs, TC, nb, npad, nch, gw, sb, perm):
    nu = nb if perm else 1
    u_refs = refs[:nu]
    (s0r_ref, s0i_ref, are_ref, aim_ref, ldt_ref, wbr_ref, wbi_ref, wcr_ref, wci_ref, d_ref, gluw_ref, glub_ref,
     nw_ref, y_ref, sor_ref, soi_ref, xr, xi, str_, sti) = refs[nu:]
    c = pl.program_id(1)
    nblk = gw // S5_CB
    R = TC * nb

    @pl.when(c == 0)
    def _init():
        str_[...] = s0r_ref[...]
        sti[...] = s0i_ref[...]

    dt = jnp.exp(ldt_ref[...])
    lr, li = are_ref[...], aim_ref[...]
    mag = jnp.exp(lr * dt)
    abr, abi = mag * jnp.cos(li * dt), mag * jnp.sin(li * dt)
    den = lr * lr + li * li
    nr, ni = abr - 1.0, abi
    er, ei = (nr * lr + ni * li) / den, (ni * lr - nr * li) / den

    if perm:
        ti = lax.broadcasted_iota(jnp.int32, (R, R), 0)
        bi_ = lax.broadcasted_iota(jnp.int32, (R, R), 1)
        pmat = jnp.logical_and(ti // nb == bi_ % TC, ti % nb == bi_ // TC).astype(_F32)
        u = _dot_sel(pmat, jnp.concatenate([r[...] for r in u_refs], axis=0), (((1,), (0,)), ((), ())))
    else:
        u = u_refs[0][...]
    ub = u.astype(_BF16)
    for g in range(nblk):
        us = ub[:, g * S5_CB:(g + 1) * S5_CB]
        br = jnp.dot(us, wbr_ref[g], preferred_element_type=_F32)
        bi = jnp.dot(us, wbi_ref[g], preferred_element_type=_F32)
        e_r, e_i = er[:, g * sb:(g + 1) * sb], ei[:, g * sb:(g + 1) * sb]
        xr[:, g * sb:(g + 1) * sb] = e_r * br - e_i * bi
        xi[:, g * sb:(g + 1) * sb] = e_r * bi + e_i * br

    rows = max(nb, 8)
    spi = rows // nb

    def body(i, carry):
        r0 = pl.multiple_of(i * rows, 8)
        b_r, b_i = xr[pl.ds(r0, rows), :], xi[pl.ds(r0, rows), :]
        s_r, s_i = str_[...], sti[...]
        outs_r, outs_i = [], []
        for j in range(spi):
            n_r = abr * s_r - abi * s_i + b_r[j * nb:(j + 1) * nb]
            n_i = abr * s_i + abi * s_r + b_i[j * nb:(j + 1) * nb]
            s_r, s_i = n_r, n_i
            outs_r.append(n_r)
            outs_i.append(n_i)
        xr[pl.ds(r0, rows), :] = outs_r[0] if spi == 1 else jnp.concatenate(outs_r, axis=0)
        xi[pl.ds(r0, rows), :] = outs_i[0] if spi == 1 else jnp.concatenate(outs_i, axis=0)
        str_[...] = s_r
        sti[...] = s_i
        return carry

    start = jnp.clip(npad - c * TC, 0, TC) // spi
    lax.fori_loop(start, TC // spi, body, 0)

    ys = []
    for g in range(nblk):
        xrb = xr[:, g * sb:(g + 1) * sb].astype(_BF16)
        xib = xi[:, g * sb:(g + 1) * sb].astype(_BF16)
        ys.append(jnp.dot(xrb, wcr_ref[g], preferred_element_type=_F32)
                  - jnp.dot(xib, wci_ref[g], preferred_element_type=_F32))
    y = jnp.concatenate(ys, axis=-1) + d_ref[...] * u
    gy = _gelu_tanh(y)
    out = gy * jax.nn.sigmoid(jnp.dot(gy.astype(_BF16), gluw_ref[...], preferred_element_type=_F32) + glub_ref[...])
    out = (out * lax.rsqrt(jnp.mean(out * out, axis=-1, keepdims=True) + EPS) * nw_ref[...]).astype(y_ref.dtype)
    if perm:
        out = lax.dot_general(pmat.astype(out.dtype), out, _TN, preferred_element_type=_F32).astype(y_ref.dtype)
        for b in range(nb):
            y_ref[b] = out[b * TC:(b + 1) * TC]
    else:
        y_ref[...] = out

    @pl.when(c == nch - 1)
    def _fin():
        sor_ref[...] = str_[...]
        soi_ref[...] = sti[...]


def _s5_mixer(u, s0_re, s0_im, a_re, a_im, log_dt, b_re, b_im, c_re, c_im, d, glu_w, glu_b, norm_w, *,
              nseq, nch, T, npad, nb, TC, perm):
    groups, ns = a_re.shape
    gw = groups * S5_GROUP
    gpb = S5_CB // S5_GROUP
    nblk = gw // S5_CB
    sb = gpb * ns
    nst = groups * ns
    nsb = nseq // nb
    ncc = nch * T // TC
    assert npad % max(1, 8 // nb) == 0
    R = TC * nb
    if perm:
        assert nsb == 1
        u_args = [u] * nb
        u_specs = [pl.BlockSpec((TC, gw), functools.partial(lambda s, c, b: (b * ncc + c, 0), b=b)) for b in range(nb)]
        y_spec = pl.BlockSpec((nb, TC, gw), lambda s, c: (0, c, 0))
        y_shape = jax.ShapeDtypeStruct((nb, ncc * TC, gw), _BF16)
    else:
        u_args = [u[:, :gw].reshape(nsb, nb, ncc, TC, gw).transpose(0, 2, 3, 1, 4).reshape(nseq * nch * T, gw)]
        u_specs = [pl.BlockSpec((R, gw), lambda s, c: (s * ncc + c, 0))]
        y_spec = pl.BlockSpec((R, gw), lambda s, c: (s * ncc + c, 0))
        y_shape = jax.ShapeDtypeStruct((nseq * nch * T, gw), _BF16)
    eye = jnp.eye(gpb, dtype=_F32)

    def bd_in(w):
        w4 = w.reshape(nblk, gpb, ns, S5_GROUP)
        return jnp.einsum('bgnc,gh->bgchn', w4, eye).reshape(nblk, S5_CB, sb).astype(_BF16)

    def bd_out(w):
        w4 = w.reshape(nblk, gpb, S5_GROUP, ns)
        return jnp.einsum('bgcn,gh->bgnhc', w4, eye).reshape(nblk, sb, S5_CB).astype(_BF16)

    row = lambda v: v.reshape(1, -1).astype(_F32)
    kern = functools.partial(_s5_kernel, TC=TC, nb=nb, npad=npad, nch=ncc, gw=gw, sb=sb, perm=perm)
    y, so_r, so_i = pl.pallas_call(
        kern, grid=(nsb, ncc),
        in_specs=u_specs + [
            pl.BlockSpec((nb, nst), lambda s, c: (s, 0)), pl.BlockSpec((nb, nst), lambda s, c: (s, 0)),
            _full((1, nst)), _full((1, nst)), _full((1, nst)),
            _full((nblk, S5_CB, sb)), _full((nblk, S5_CB, sb)), _full((nblk, sb, S5_CB)),
            _full((nblk, sb, S5_CB)), _full((1, gw)), _full((gw, gw)), _full((1, gw)), _full((1, gw))],
        out_specs=[y_spec, pl.BlockSpec((nb, nst), lambda s, c: (s, 0)), pl.BlockSpec((nb, nst), lambda s, c: (s, 0))],
        out_shape=[y_shape, jax.ShapeDtypeStruct((nseq, nst), _F32), jax.ShapeDtypeStruct((nseq, nst), _F32)],
        scratch_shapes=[pltpu.VMEM((R, nst), _F32), pltpu.VMEM((R, nst), _F32),
                        pltpu.VMEM((nb, nst), _F32), pltpu.VMEM((nb, nst), _F32)],
        compiler_params=_cparams(("parallel", "arbitrary"), 2 * R * gw * 4, R * gw * 2, 4 * nblk * S5_CB * sb * 2,
                                 gw * gw * 2, R * nst * 4, 6 * max(nb, 8) * nst * 4),
        name="s5_mixer",
    )(*u_args, s0_re.reshape(nseq, nst), s0_im.reshape(nseq, nst), row(a_re), row(a_im),
      row(jnp.broadcast_to(log_dt[:, None], (groups, ns))), bd_in(b_re), bd_in(b_im), bd_out(c_re), bd_out(c_im),
      row(d), glu_w.astype(_BF16), row(glu_b), row(norm_w))
    if perm:
        y = y.reshape(nseq * nch * T, gw)
    else:
        y = y.reshape(nsb, ncc, TC, nb, gw).transpose(0, 3, 1, 2, 4).reshape(nseq * nch * T, gw)
    return y, so_r.reshape(nseq, groups, ns), so_i.reshape(nseq, groups, ns)


def _head_sums(x):
    ri = lax.broadcasted_iota(jnp.int32, (LANE, LANE), 0) // RWKV_HEAD_DIM
    ci = lax.broadcasted_iota(jnp.int32, (LANE, LANE), 1) // RWKV_HEAD_DIM
    e = (ri == ci).astype(_F32)
    nn = (((1,), (0,)), ((), ()))
    return jnp.concatenate([_dot_sel_r(x[:, j:j + LANE], e, nn) for j in range(0, x.shape[1], LANE)], axis=-1)


def _rwkv_pre_kernel(p_ref, mu_ref, w0_ref, w2_ref, a0_ref, a2_ref, g2_ref, kk_ref, ka_ref, rk_ref,
                     r_o, k_o, v_o, w_o, kk_o, b_o, g_o, bon_o, xbuf, *, T, gw, dl, da, dg):
    c = pl.program_id(1)

    @pl.when(c == 0)
    def _init():
        xbuf[0:8, :] = jnp.zeros((8, xbuf.shape[1]), _F32)

    p = p_ref[...]
    xbuf[8:8 + T, :] = p
    prev = xbuf[7:7 + T, :]
    last = xbuf[7 + T:8 + T, :]
    xbuf[7:8, :] = last
    pm = p + (prev - p) * mu_ref[...]
    r, k, v = pm[:, :gw], pm[:, gw:2 * gw], pm[:, 2 * gw:3 * gw]
    o1 = 3 * gw
    wl, al, gl = pm[:, o1:o1 + dl], pm[:, o1 + dl:o1 + dl + da], pm[:, o1 + dl + da:o1 + dl + da + dg]
    wx = w0_ref[...] + jnp.dot(jnp.tanh(wl).astype(_BF16), w2_ref[...], preferred_element_type=_F32)
    decay = jnp.exp(-jnp.exp(-_softplus(-wx) - 0.5))
    a = jax.nn.sigmoid(a0_ref[...] + jnp.dot(al.astype(_BF16), a2_ref[...], preferred_element_type=_F32))
    g = jnp.dot(jax.nn.sigmoid(gl).astype(_BF16), g2_ref[...], preferred_element_type=_F32)
    kkf = k * kk_ref[...]
    kk = kkf / jnp.maximum(jnp.sqrt(_head_sums(kkf * kkf)), 1e-12)
    k2 = k * (1.0 + (a - 1.0) * ka_ref[...])
    r_o[...] = r
    k_o[...] = k2
    v_o[...] = v
    w_o[...] = decay
    kk_o[...] = kk
    b_o[...] = kk * a
    g_o[...] = g
    bon_o[...] = _head_sums(r * k2 * rk_ref[...]) * v


def _rwkv_scan_kernel(r_ref, k_ref, v_ref, w_ref, kk_ref, b_ref, s0_ref, o_ref, so_ref, st, *, TC, K, npad, nch):
    c = pl.program_id(1)

    @pl.when(c == 0)
    def _init():
        st[...] = s0_ref[...]

    start = jnp.clip(npad - c * TC, 0, TC)

    @pl.when(start > 0)
    def _zero():
        o_ref[...] = jnp.zeros(o_ref.shape, _F32)

    vecs = (kk_ref, w_ref, b_ref, k_ref, r_ref)
    NACC = 4

    def tree(acc):
        return (acc[0] + acc[1]) + (acc[2] + acc[3])

    def madd(acc, k, x):
        acc[k % NACC] = x if acc[k % NACC] is None else acc[k % NACC] + x

    def sa_of(t):
        acc = [None] * NACC
        for k in range(K):
            madd(acc, k, st[k] * vecs[0][k, pl.ds(t, 1), :])
        return tree(acc)

    def body(t, sa):
        row = lambda i, k: vecs[i][k, pl.ds(t, 1), :]
        tn = jnp.minimum(t + 1, TC - 1)
        v_t = v_ref[t]
        acc_o, acc_s = [None] * NACC, [None] * NACC
        for k in range(K):
            s = st[k] * row(1, k) - sa * row(2, k) + v_t * row(3, k)
            st[k] = s
            madd(acc_o, k, s * row(4, k))
            madd(acc_s, k, s * vecs[0][k, pl.ds(tn, 1), :])
        o_ref[t] = tree(acc_o)
        return tree(acc_s)

    lax.fori_loop(start, TC, body, sa_of(jnp.minimum(start, TC - 1)))

    @pl.when(c == nch - 1)
    def _fin():
        so_ref[...] = st[...]


def _rwkv_relayout_kernel(*refs, nseq, H, K, J):
    x_refs, o_ref, zs = refs[:nseq], refs[nseq], refs[nseq + 1]
    gw = H * K
    for n in range(nseq):
        x = x_refs[n][...]
        for c0 in range(0, gw, LANE):
            zs[n * gw + c0:n * gw + c0 + LANE, :] = x[:, c0:c0 + LANE].T
    for k in range(K):
        q = zs[pl.ds(k, nseq * H, stride=K), :]
        o_ref[k] = jnp.concatenate([q] * J, axis=0).T


def _rwkv_relayout(x, *, nseq, nch, H, K, J):
    gw = H * K
    nl = J * nseq * H
    assert nl == LANE
    specs = [pl.BlockSpec((CHUNK, gw), functools.partial(lambda c, n: (n * nch + c, 0), n=n)) for n in range(nseq)]
    return pl.pallas_call(
        functools.partial(_rwkv_relayout_kernel, nseq=nseq, H=H, K=K, J=J), grid=(nch,),
        in_specs=specs, out_specs=pl.BlockSpec((K, CHUNK, nl), lambda c: (0, c, 0)),
        out_shape=jax.ShapeDtypeStruct((K, nch * CHUNK, nl), _F32),
        scratch_shapes=[pltpu.VMEM((nseq * gw, CHUNK), _F32)],
        compiler_params=_cparams(("parallel",), nseq * CHUNK * gw * 4, K * CHUNK * nl * 4, nseq * gw * CHUNK * 4),
        name="rwkv_relayout",
    )(*([x] * nseq))


def _rwkv_post_kernel(o_ref, bon_ref, g_ref, lw_ref, lb_ref, y_ref):
    o = o_ref[...]
    oc = o - _head_sums(o) * (1.0 / RWKV_HEAD_DIM)
    var = _head_sums(oc * oc) * (1.0 / RWKV_HEAD_DIM)
    y = (oc * lax.rsqrt(var + RWKV_GN_EPS) * lw_ref[...] + lb_ref[...] + bon_ref[...]) * g_ref[...]
    y_ref[...] = y.astype(y_ref.dtype)


def _rwkv_mixer(p, s0, mu_p, w0, w2_p, a0, a2_p, g2_p, k_k, k_a, r_k, ln_w, ln_b, *, nseq, nch, T, npad, gw,
                dl, da, dg, J, TC):
    H, V, K = s0.shape[1:]
    wp = p.shape[1]
    rows = nseq * nch * T
    L = nch * T
    row = lambda v: v.reshape(1, -1).astype(_F32)
    f32rows = jax.ShapeDtypeStruct((rows, gw), _F32)
    blk = pl.BlockSpec((T, gw), lambda b, c: (b * nch + c, 0))
    pre = pl.pallas_call(
        functools.partial(_rwkv_pre_kernel, T=T, gw=gw, dl=dl, da=da, dg=dg), grid=(nseq, nch),
        in_specs=[pl.BlockSpec((T, wp), lambda b, c: (b * nch + c, 0)), _full((1, wp)), _full((1, gw)),
                  _full((dl, gw)), _full((1, gw)), _full((da, gw)), _full((dg, gw)), _full((1, gw)), _full((1, gw)),
                  _full((1, gw))],
        out_specs=[blk] * 8, out_shape=[f32rows] * 8,
        scratch_shapes=[pltpu.VMEM((T + 8, wp), _F32)],
        compiler_params=_cparams(("parallel", "arbitrary"), 2 * T * wp * 4, 8 * T * gw * 4, 12 * T * gw * 4),
        name="rwkv_pre",
    )(p, row(mu_p), row(w0), w2_p.astype(_BF16), row(a0), a2_p.astype(_BF16), g2_p.astype(_BF16), row(k_k),
      row(k_a), row(r_k))
    r, k2, v, decay, kk, bvec, g, bonus = pre

    VI = V // J
    NL = J * nseq * H
    assert NL % LANE == 0 or NL < LANE

    def kvec(x):
        if T == CHUNK and NL == LANE:
            return _rwkv_relayout(x, nseq=nseq, nch=nch, H=H, K=K, J=J)
        y = x.reshape(nseq, L, H, K).transpose(3, 1, 0, 2).reshape(K, L, 1, nseq * H)
        return jnp.broadcast_to(y, (K, L, J, nseq * H)).reshape(K, L, NL)

    vv = v.reshape(nseq, L, H, J, VI).transpose(1, 4, 3, 0, 2).reshape(L, VI, NL)
    st0 = s0.reshape(nseq, H, J, VI, K).transpose(4, 3, 2, 0, 1).reshape(K, VI, NL)
    lb = min(NL, LANE)
    ncc = L // TC
    kspec = pl.BlockSpec((K, TC, lb), lambda n, c: (0, c, n))
    vspec = pl.BlockSpec((TC, VI, lb), lambda n, c: (c, 0, n))
    sspec = pl.BlockSpec((K, VI, lb), lambda n, c: (0, 0, n))
    o, st1 = pl.pallas_call(
        functools.partial(_rwkv_scan_kernel, TC=TC, K=K, npad=npad, nch=ncc), grid=(NL // lb, ncc),
        in_specs=[kspec, kspec, vspec, kspec, kspec, kspec, sspec],
        out_specs=[vspec, sspec],
        out_shape=[jax.ShapeDtypeStruct((L, VI, NL), _F32), jax.ShapeDtypeStruct((K, VI, NL), _F32)],
        scratch_shapes=[pltpu.VMEM((K, VI, lb), _F32)],
        compiler_params=_cparams(("parallel", "arbitrary"), 5 * TC * K * lb * 4, 2 * TC * VI * lb * 4,
                                 3 * VI * K * lb * 4),
        name="rwkv_scan",
    )(kvec(r), kvec(k2), vv, kvec(decay), kvec(kk), kvec(bvec), st0)
    o = o.reshape(L, VI, J, nseq, H).transpose(3, 0, 4, 2, 1).reshape(rows, gw)
    s_new = st1.reshape(K, VI, J, nseq, H).transpose(3, 4, 2, 1, 0).reshape(nseq, H, V, K)

    tr = _pick(rows, (256, 128, 64, 32, 16, 8))
    rblk = pl.BlockSpec((tr, gw), lambda i: (i, 0))
    y = pl.pallas_call(
        _rwkv_post_kernel, grid=(rows // tr,),
        in_specs=[rblk, rblk, rblk, _full((1, gw)), _full((1, gw))],
        out_specs=rblk, out_shape=jax.ShapeDtypeStruct((rows, gw), _BF16),
        compiler_params=_cparams(("parallel",), 4 * tr * gw * 4, 8 * tr * gw * 4),
        name="rwkv_post",
    )(o, bonus, g, row(ln_w), row(ln_b))
    return y, s_new


def kernel(x_prompt, x_sample, state_ssd, state_ssd_conv, state_rwkv, state_rwkv_shift, state_ret, state_s5_re,
           state_s5_im, meta, ln_mix, w_in, ssd_conv_w, ssd_conv_b, ssd_dt_bias, ssd_a_log, ssd_d, ssd_norm, rwkv_mu,
           rwkv_w0, rwkv_w2, rwkv_a0, rwkv_a2, rwkv_g2, rwkv_k_k, rwkv_k_a, rwkv_r_k, rwkv_ln_w, rwkv_ln_b, ret_gn_w,
           ret_gn_b, s5_a_re, s5_a_im, s5_log_dt, s5_b_re, s5_b_im, s5_c_re, s5_c_im, s5_d, s5_glu_w, s5_glu_b,
           s5_norm, w_out, ln_ffn, w_gate, w_up, w_down, ln_f):
    bp, sp, d = x_prompt.shape
    bs, ls, _ = x_sample.shape
    depth = w_in.shape[0]
    gw = d // 4
    ssd_heads, ssd_p, ssd_n = state_ssd.shape[2:]
    xbc_w = state_ssd_conv.shape[-1]
    rwkv_proj = state_rwkv_shift.shape[-1]
    ret_heads, ret_dk, ret_dv = state_ret.shape[2:]
    qkw = ret_heads * ret_dk
    dff = w_gate.shape[-1]

    lreal = N_META + sp
    lp = _round_up(lreal, CHUNK)
    pad = lp - lreal
    mp, ms = bp * lp, bs * ls
    m = mp + ms

    head = jnp.concatenate([jnp.zeros((pad, d), _F32), meta.astype(_F32)], axis=0)
    h = jnp.concatenate([piece for b in range(bp) for piece in (head, x_prompt[b])] + [x_sample.reshape(ms, d)],
                        axis=0)

    in_splits = (gw, xbc_w, ssd_heads, rwkv_proj, qkw, qkw, gw, gw, gw)
    offs = [0]
    for s in in_splits:
        offs.append(offs[-1] + s)
    ntile = 512 if gw % 512 == 0 else LANE

    layers = range(depth)

    lora_c = (rwkv_w2.shape[1], rwkv_a2.shape[1], rwkv_g2.shape[1])
    dl, da, dg = (_round_up(n, LANE) for n in lora_c)
    rw_w = _round_up(3 * gw + dl + da + dg, ntile)

    def rw_pad(x):
        parts, o = [x[..., :3 * gw]], 3 * gw
        for n, npd in zip(lora_c, (dl, da, dg)):
            parts.append(jnp.pad(x[..., o:o + n], [(0, 0)] * (x.ndim - 1) + [(0, npd - n)]))
            o += n
        y = jnp.concatenate(parts, axis=-1)
        return jnp.pad(y, [(0, 0)] * (x.ndim - 1) + [(0, rw_w - y.shape[-1])])

    def rw_compact(x):
        parts, o = [x[..., :3 * gw]], 3 * gw
        for n, npd in zip(lora_c, (dl, da, dg)):
            parts.append(x[..., o:o + n])
            o += npd
        return jnp.concatenate(parts, axis=-1)

    assert (gw + xbc_w) % LANE == 0 and gw % LANE == 0
    o_rw = offs[3] + 3 * gw
    plans = (
        ((0, 0, gw + xbc_w), (gw + xbc_w, gw + xbc_w, ssd_heads)),
        ((0, offs[3], 3 * gw), (3 * gw, o_rw, lora_c[0]), (3 * gw + dl, o_rw + lora_c[0], lora_c[1]),
         (3 * gw + dl + da, o_rw + lora_c[0] + lora_c[1], lora_c[2])),
        ((0, offs[4], offs[8] - offs[4]),),
        ((0, offs[8], gw),),
    )
    widths = (_round_up(gw + xbc_w + LANE, ntile), rw_w, _round_up(offs[8] - offs[4], ntile), _round_up(gw, ntile))
    packed = [_pack_w_in(w_in, l, plans, widths) for l in layers]
    w_ssd_b, w_rw_b, w_ret_b, w_s5_b = ([p[i] for p in packed] for i in range(4))
    mu_p = rw_pad(rwkv_mu)
    padrows = lambda w, n: jnp.pad(w, ((0, 0), (0, n - w.shape[1]), (0, 0)))
    w2_p, a2_p, g2_p = padrows(rwkv_w2, dl), padrows(rwkv_a2, da), padrows(rwkv_g2, dg)
    w_out_b, w_gate_b, w_up_b, w_down_b = (w.astype(_BF16) for w in (w_out, w_gate, w_up, w_down))
    half = dff // 2

    zeros_b = lambda s: jnp.zeros((bp,) + s.shape[2:], _F32)
    nch_p = lp // CHUNK
    ts = _round_up(ls + 3, 8)
    spad = ts - ls

    def sample_rows(rows, hist=None, c0=0):
        r3 = jnp.pad(rows.reshape(bs, ls, -1), ((0, 0), (spad, 0), (0, 0)))
        if hist is not None:
            k, w = hist.shape[1:]
            r3 = r3.at[:, spad - k:spad, c0:c0 + w].set(hist)
        return r3.reshape(bs * ts, -1)

    def unsample(y):
        return y.reshape(bs, ts, -1)[:, spad:].reshape(ms, -1)

    def last_rows(rows, k):
        return jnp.stack([rows[(b + 1) * lp - k:(b + 1) * lp] for b in range(bp)])

    outs_p = [[] for _ in range(7)]
    outs_s = [[] for _ in range(7)]
    for l in range(depth):
        hn = _rmsnorm(h, ln_mix[l], _BF16, lp=lp, pad=pad, mp=mp)

        pa = _mm(hn, w_ssd_b[l], name="in_proj_ssd")
        ssd_w = (ssd_conv_w[l], ssd_conv_b[l], ssd_dt_bias[l], ssd_a_log[l], ssd_d[l], ssd_norm[l])
        ya_p, hp_new = _ssd_mixer(pa, zeros_b(state_ssd), *ssd_w, nseq=bp, nch=nch_p, T=CHUNK, npad=pad, gw=gw)
        pa_s = sample_rows(pa[mp:], state_ssd_conv[l], gw)
        ya_s, hs_new = _ssd_mixer(pa_s, state_ssd, *ssd_w, nseq=bs, nch=1, T=ts, npad=spad, gw=gw, sl=l)
        kc = ssd_conv_w.shape[1] - 1
        outs_p[0].append(hp_new)
        outs_p[1].append(last_rows(pa, kc)[..., gw:gw + xbc_w])
        outs_s[0].append(hs_new)
        outs_s[1].append(pa_s.reshape(bs, ts, -1)[:, ts - kc:, gw:gw + xbc_w])

        pb = _mm(hn, w_rw_b[l], name="in_proj_rwkv")
        rw_args = (mu_p[l], rwkv_w0[l], w2_p[l], rwkv_a0[l], a2_p[l], g2_p[l], rwkv_k_k[l], rwkv_k_a[l],
                   rwkv_r_k[l], rwkv_ln_w[l], rwkv_ln_b[l])
        rw_kw = dict(gw=gw, dl=dl, da=da, dg=dg)
        rw_heads = state_rwkv.shape[2]
        yb_p, sp_new = _rwkv_mixer(pb, zeros_b(state_rwkv), *rw_args, nseq=bp, nch=nch_p, T=CHUNK, npad=pad,
                                   J=max(1, LANE // (bp * rw_heads)), TC=CHUNK // 4, **rw_kw)
        pb_s = sample_rows(pb[mp:], rw_pad(state_rwkv_shift[l])[:, None, :], 0)
        yb_s, ss_new = _rwkv_mixer(pb_s, state_rwkv[l], *rw_args, nseq=bs, nch=1, T=ts, npad=spad,
                                   J=max(1, LANE // (bs * rw_heads)), TC=ts, **rw_kw)
        outs_p[2].append(sp_new)
        outs_p[3].append(rw_compact(last_rows(pb, 1)[:, 0]))
        outs_s[2].append(ss_new)
        outs_s[3].append(rw_compact(pb_s.reshape(bs, ts, -1)[:, -1]))

        pc = _mm(hn, w_ret_b[l], name="in_proj_ret")
        yc_p, rp_new = _ret_mixer(pc, zeros_b(state_ret), ret_gn_w[l], ret_gn_b[l], 0,
                                  nseq=bp, nch=nch_p, T=CHUNK, npad=pad, gw=gw)
        yc_s, rs_new = _ret_mixer(sample_rows(pc[mp:]), state_ret, ret_gn_w[l], ret_gn_b[l], PAST_LEN,
                                  nseq=bs, nch=1, T=ts, npad=spad, gw=gw, sl=l)
        outs_p[4].append(rp_new)
        outs_s[4].append(rs_new)

        pd = _mm(hn, w_s5_b[l], name="in_proj_s5")
        s5_w = (s5_a_re[l], s5_a_im[l], s5_log_dt[l], s5_b_re[l], s5_b_im[l], s5_c_re[l], s5_c_im[l], s5_d[l],
                s5_glu_w[l], s5_glu_b[l], s5_norm[l])
        yd_p, s5r_p, s5i_p = _s5_mixer(pd, zeros_b(state_s5_re), zeros_b(state_s5_im), *s5_w, nseq=bp, nch=nch_p,
                                       T=CHUNK, npad=pad, nb=bp, TC=CHUNK // 2, perm=True)
        yd_s, s5r_s, s5i_s = _s5_mixer(sample_rows(pd[mp:]), state_s5_re[l], state_s5_im[l], *s5_w, nseq=bs, nch=1,
                                       T=ts, npad=spad, nb=min(bs, 32), TC=ts, perm=False)
        outs_p[5].append(s5r_p)
        outs_p[6].append(s5i_p)
        outs_s[5].append(s5r_s)
        outs_s[6].append(s5i_s)

        h = _outproj((ya_p, yb_p, yc_p, yd_p), w_out_b, h, 0, wl=l)
        h = _outproj(tuple(unsample(y) for y in (ya_s, yb_s, yc_s, yd_s)), w_out_b, h, mp, wl=l)
        hn = _rmsnorm(h, ln_ffn[l], _BF16)
        ff = _swiglu(hn, w_gate_b, w_up_b, wl=l)
        h = _mm(ff, w_down_b, res=h, kb=0, tk=half, name="ffn_down0", wl=l)
        h = _mm(ff, w_down_b, res=h, kb=1, tk=half, name="ffn_down1", wl=l)

    y_prompt, y_sample = _final_norm(h, ln_f, bp=bp, sp=sp, lp=lp, ms=ms)
    return ((y_prompt.reshape(bp, sp, d), y_sample.reshape(bs, ls, d))
            + tuple(jnp.stack(a) for a in outs_p) + tuple(jnp.stack(a) for a in outs_s))
```
